```python
import jax, jax.numpy as jnp
from jax import lax
import numpy as np

D_MODEL = 2048
BATCH = 32
SEQ = 256
DEPTH = 1
DEC_BATCH = 4
DEC_SEQ = 2048
PAST_LEN = 256

GRID_W = 64
MIX_W = D_MODEL
RWKV_W = MIX_W // 2
FOURIER_W = MIX_W - RWKV_W
HEAD_N = 64
N_RWKV_HEADS = RWKV_W // HEAD_N
N_FOURIER_GROUPS = 4
FOURIER_GW = FOURIER_W // N_FOURIER_GROUPS
LORA_R = 96
N_DIRS = 2
SHIFT_W = 3 * RWKV_W + 2 * N_DIRS * LORA_R
IN_W = SHIFT_W + RWKV_W + FOURIER_W
N_EXPERTS = 16
EXPERT_FF = 5504
CAPACITY_FACTOR = 2
N_MOD = 6
NORM_EPS = 1e-6
GN_EPS = 64e-5

kernel_name = "rwkv7_fnet_ec_moe_diffusion_step"


def rms_norm(x, g):
    xf = x.astype(jnp.float32)
    y = xf * lax.rsqrt(jnp.mean(xf * xf, axis=-1, keepdims=True) + NORM_EPS)
    return y.astype(x.dtype) * g


def shift_seq(u):
    prev = jnp.pad(u[:, :-1], ((0, 0), (1, 0), (0, 0)))
    nxt = jnp.pad(u[:, 1:], ((0, 0), (0, 1), (0, 0)))
    return 0.5 * (prev + nxt)


def shift_grid(u):
    B, T, C = u.shape
    rows = T // GRID_W
    g = u.reshape(B, rows, GRID_W, C)
    up = jnp.pad(g[:, :-1], ((0, 0), (1, 0), (0, 0), (0, 0)))
    down = jnp.pad(g[:, 1:], ((0, 0), (0, 1), (0, 0), (0, 0)))
    left = jnp.pad(g[:, :, :-1], ((0, 0), (0, 0), (1, 0), (0, 0)))
    right = jnp.pad(g[:, :, 1:], ((0, 0), (0, 0), (0, 1), (0, 0)))
    return (0.25 * (up + down + left + right)).reshape(B, T, C)


def wkv_scan(r, w, k, v, a, b, s0):
    def step(S, inp):
        r_t, w_t, k_t, v_t, a_t, b_t = inp
        sa = jnp.einsum('dbhvk,dbhk->dbhv', S, a_t)
        S = S * w_t[..., None, :] + sa[..., :, None] * b_t[..., None, :] + v_t[..., :, None] * k_t[..., None, :]
        return S, jnp.einsum('dbhvk,dbhk->dbhv', S, r_t)
    xs = tuple(jnp.moveaxis(t, 2, 0) for t in (r, w, k, v, a, b))
    s_fin, y = lax.scan(step, s0, xs)
    return jnp.moveaxis(y, 0, 2), s_fin


def parallel_mix(h, p, shift_fn, s0):
    B, T, _ = h.shape
    H, N = N_RWKV_HEADS, HEAD_N
    f32 = jnp.float32
    proj = h @ p['w_in']
    ps, g, f = jnp.split(proj, [SHIFT_W, SHIFT_W + RWKV_W], axis=-1)
    ps = ps + p['shift_mu'] * (shift_fn(ps) - ps)
    r, k, v, wd, ad = jnp.split(ps, [RWKV_W, 2 * RWKV_W, 3 * RWKV_W, 3 * RWKV_W + N_DIRS * LORA_R], axis=-1)
    r, k, v = r.astype(f32), k.astype(f32), v.astype(f32)
    wd = jnp.tanh(wd.astype(f32)).reshape(B, T, N_DIRS, LORA_R)
    ad = ad.astype(f32).reshape(B, T, N_DIRS, LORA_R)
    w_raw = p['w_decay0'].astype(f32)[:, None, None, :] + jnp.einsum('btdr,drc->dbtc', wd, p['w_decay_up'].astype(f32))
    decay = jnp.exp(-jnp.exp(-jax.nn.softplus(-w_raw) - 0.5))
    iclr = jax.nn.sigmoid(p['w_iclr0'].astype(f32)[:, None, None, :] + jnp.einsum('btdr,drc->dbtc', ad, p['w_iclr_up'].astype(f32)))
    kk = (k * p['k_k']).reshape(B, T, H, N)
    kk = (kk * lax.rsqrt(jnp.sum(kk * kk, axis=-1, keepdims=True) + 1e-12)).reshape(B, T, RWKV_W)
    k_dir = k[None] * (1.0 + (iclr - 1.0) * p['k_a'])
    b_dir = kk[None] * iclr
    heads = lambda t: t.reshape(t.shape[:-1] + (H, N))
    both = lambda t: jnp.stack([t, jnp.flip(t, 1)])
    per_dir = lambda t: jnp.stack([t[0], jnp.flip(t[1], 1)])
    y, s_fin = wkv_scan(heads(both(r)), heads(per_dir(decay)), heads(per_dir(k_dir)),
                        heads(both(v)), heads(both(-kk)), heads(per_dir(b_dir)), s0)
    y = y[0] + jnp.flip(y[1], 1)
    mu = jnp.mean(y, axis=-1, keepdims=True)
    var = jnp.mean(jnp.square(y - mu), axis=-1, keepdims=True)
    yn = ((y - mu) * lax.rsqrt(var + GN_EPS)).reshape(B, T, RWKV_W) * p['gn_w'] + p['gn_b']
    k_bonus = 0.5 * (k_dir[0] + k_dir[1])
    bonus = jnp.sum(heads(r * k_bonus) * p['r_k'], axis=-1, keepdims=True) * heads(v)
    y_rwkv = (yn + bonus.reshape(B, T, RWKV_W)) * jax.nn.sigmoid(g.astype(f32))
    fg = f.astype(f32).reshape(B, T, N_FOURIER_GROUPS, FOURIER_GW).transpose(0, 2, 1, 3)
    y_f = jnp.fft.fft2(fg, norm='ortho').real.transpose(0, 2, 1, 3).reshape(B, T, FOURIER_W)
    mixed = jnp.concatenate([y_rwkv, y_f], axis=-1).astype(h.dtype)
    return mixed @ p['w_out'], s_fin


def expert_choice_ffn(h, router_w, w1, w3, w2):
    B, T, D = h.shape
    n = B * T
    cap = n * CAPACITY_FACTOR // N_EXPERTS
    tok = h.reshape(n, D)
    aff = jax.nn.softmax((tok @ router_w).astype(jnp.float32), axis=-1)
    gate, idx = lax.top_k(aff.T, cap)
    xe = tok[idx]
    hid = jax.nn.silu(jnp.einsum('ecd,edf->ecf', xe, w1)) * jnp.einsum('ecd,edf->ecf', xe, w3)
    ye = jnp.einsum('ecf,efd->ecd', hid, w2) * gate[..., None].astype(h.dtype)
    out = jnp.zeros_like(tok).at[idx.reshape(-1)].add(ye.reshape(-1, D))
    return out.reshape(B, T, D)


def block(x, mod, p, shift_fn, s0):
    sh1, sc1, g1, sh2, sc2, g2 = jnp.split(mod[:, None, :].astype(x.dtype), N_MOD, axis=-1)
    h = rms_norm(x, p['norm1_g']) * (1 + sc1) + sh1
    m, s_fin = parallel_mix(h, p, shift_fn, s0)
    x = x + g1 * m
    h = rms_norm(x, p['norm2_g']) * (1 + sc2) + sh2
    x = x + g2 * expert_choice_ffn(h, p['router_w'], p['exp_w1'], p['exp_w3'], p['exp_w2'])
    return x, s_fin


def setup_inputs(seed: int = 0) -> dict:
    key = jax.random.key(seed)
    ks = jax.random.split(key, 32)
    f32 = jnp.float32
    nrm = lambda k, shape, s: jax.random.normal(k, shape, f32) * s
    D, L, E, F = D_MODEL, DEPTH, N_EXPERTS, EXPERT_FF
    return {
        'x_prompt': nrm(ks[0], (BATCH, SEQ, D), 1.0),
        'x_sample': nrm(ks[1], (DEC_BATCH, DEC_SEQ, D), 1.0),
        'state_wkv': nrm(ks[2], (DEC_BATCH, DEPTH, N_DIRS, N_RWKV_HEADS, HEAD_N, HEAD_N), 1.0),
        'c': nrm(ks[3], (DEC_BATCH, D), 1.0),
        'c_ctx': nrm(ks[4], (D,), 1.0),
        'ada_w': nrm(ks[5], (L, D, N_MOD * D), 0.5 * D ** -0.5),
        'ada_b': nrm(ks[6], (L, N_MOD * D), 0.02),
        'norm1_g': 1.0 + nrm(ks[7], (L, D), 0.01),
        'norm2_g': 1.0 + nrm(ks[8], (L, D), 0.01),
        'w_in': nrm(ks[9], (L, D, IN_W), D ** -0.5),
        'shift_mu': jax.random.uniform(ks[10], (L, SHIFT_W), f32),
        'w_decay_up': nrm(ks[11], (L, N_DIRS, LORA_R, RWKV_W), 0.1),
        'w_decay0': jax.random.uniform(ks[12], (L, N_DIRS, RWKV_W), f32, -6.0, 0.0),
        'w_iclr_up': nrm(ks[13], (L, N_DIRS, LORA_R, RWKV_W), 0.1),
        'w_iclr0': nrm(ks[14], (L, N_DIRS, RWKV_W), 0.5),
        'k_k': 0.85 + nrm(ks[15], (L, RWKV_W), 0.05),
        'k_a': 1.0 + nrm(ks[16], (L, RWKV_W), 0.05),
        'r_k': nrm(ks[17], (L, N_RWKV_HEADS, HEAD_N), 0.1),
        'gn_w': 1.0 + nrm(ks[18], (L, RWKV_W), 0.01),
        'gn_b': nrm(ks[19], (L, RWKV_W), 0.01),
        'w_out': nrm(ks[20], (L, MIX_W, D), MIX_W ** -0.5),
        'router_w': nrm(ks[21], (L, D, E), D ** -0.5),
        'exp_w1': nrm(ks[22], (L, E, D, F), D ** -0.5),
        'exp_w3': nrm(ks[23], (L, E, D, F), D ** -0.5),
        'exp_w2': nrm(ks[24], (L, E, F, D), F ** -0.5),
        'final_norm_g': 1.0 + nrm(ks[25], (D,), 0.01),
    }


def reference(x_prompt, x_sample, state_wkv, c, c_ctx, ada_w, ada_b, norm1_g, norm2_g, w_in, shift_mu,
              w_decay_up, w_decay0, w_iclr_up, w_iclr0, k_k, k_a, r_k, gn_w, gn_b, w_out, router_w,
              exp_w1, exp_w3, exp_w2, final_norm_g):
    f32 = jnp.float32
    H, N = N_RWKV_HEADS, HEAD_N
    x_ctx = x_prompt
    x_lat = x_sample
    ctx_states = []
    for l in range(DEPTH):
        p = {'norm1_g': norm1_g[l], 'norm2_g': norm2_g[l], 'w_in': w_in[l], 'shift_mu': shift_mu[l],
             'w_decay_up': w_decay_up[l], 'w_decay0': w_decay0[l], 'w_iclr_up': w_iclr_up[l],
             'w_iclr0': w_iclr0[l], 'k_k': k_k[l], 'k_a': k_a[l], 'r_k': r_k[l], 'gn_w': gn_w[l],
             'gn_b': gn_b[l], 'w_out': w_out[l], 'router_w': router_w[l], 'exp_w1': exp_w1[l],
             'exp_w3': exp_w3[l], 'exp_w2': exp_w2[l]}
        mod_ctx = (jax.nn.silu(c_ctx) @ ada_w[l] + ada_b[l])[None]
        mod_lat = jax.nn.silu(c) @ ada_w[l] + ada_b[l]
        s0_ctx = jnp.zeros((N_DIRS, x_ctx.shape[0], H, N, N), f32)
        x_ctx, s_ctx = block(x_ctx, mod_ctx, p, shift_seq, s0_ctx)
        ctx_states.append(jnp.moveaxis(s_ctx, 0, 1))
        s0_lat = jnp.moveaxis(state_wkv[:, l].astype(f32), 1, 0)
        x_lat, _ = block(x_lat, mod_lat, p, shift_grid, s0_lat)
    y_prompt = rms_norm(x_ctx, final_norm_g)
    y_sample = rms_norm(x_lat, final_norm_g)
    new_state_wkv = jnp.stack(ctx_states, axis=1)
    return (y_prompt, y_sample, new_state_wkv)
```

```python
import functools
import math

import jax
import jax.numpy as jnp
from jax import lax
from jax.experimental import pallas as pl
from jax.experimental.pallas import tpu as pltpu

F32 = jnp.float32
BF16 = jnp.bfloat16
HIGHEST = lax.Precision.HIGHEST

D_MODEL = 2048
RWKV_W = 1024
HEAD_N = 64
N_HEADS = 16
LORA_R = 96
N_EXPERTS = 16
EXPERT_FF = 5504
CAPACITY_FACTOR = 2
N_MOD = 6
NORM_EPS = 1e-6
GN_EPS = 64e-5
FOURIER_GW = 256
N_FOURIER_GROUPS = 4

LANES = 128
HEAD_TILES = RWKV_W // LANES
SHIFT_W = 3 * RWKV_W + 4 * LORA_R
LORA_W = 4 * LORA_R
LORA_OFF = 3 * RWKV_W
G_OFF = LORA_OFF + LORA_W + LANES
F_OFF = G_OFF + RWKV_W
IN_PAD_W = F_OFF + RWKV_W
ROW_CHUNK = 256
VMEM_LIMIT = 56 * 1024 * 1024
DECAY_SCALE = math.exp(-0.5)


def _params(*sem):
    return pltpu.CompilerParams(dimension_semantics=sem, vmem_limit_bytes=VMEM_LIMIT)


def _seg_ones():
    r = lax.broadcasted_iota(jnp.int32, (LANES, LANES), 0) // HEAD_N
    c = lax.broadcasted_iota(jnp.int32, (LANES, LANES), 1) // HEAD_N
    return (r == c).astype(F32)


def _seg_sum(x, seg):
    return jnp.dot(x, seg, precision=HIGHEST, preferred_element_type=F32)


def _ada_kernel(c_ref, w_ref, b_ref, o_ref):
    c = c_ref[...]
    s = c * jax.nn.sigmoid(c)
    o_ref[...] = jnp.dot(s.astype(BF16), w_ref[...].astype(BF16),
                         preferred_element_type=F32) + b_ref[...]


def ada_mod(cvecs, ada_w, ada_b):
    rows, d = cvecs.shape
    n = ada_w.shape[1]
    tn = 1024
    return pl.pallas_call(
        _ada_kernel,
        grid=(n // tn,),
        in_specs=[pl.BlockSpec((rows, d), lambda j: (0, 0)),
                  pl.BlockSpec((d, tn), lambda j: (0, j)),
                  pl.BlockSpec((1, tn), lambda j: (0, j))],
        out_specs=pl.BlockSpec((rows, tn), lambda j: (0, j)),
        out_shape=jax.ShapeDtypeStruct((rows, n), F32),
        compiler_params=_params("arbitrary"),
        name="ada_mod",
    )(cvecs, ada_w, ada_b.reshape(1, n))


def _mod_row_map(tm, rows_per_mod, mod_base):
    return lambda i: mod_base + (i * tm) // rows_per_mod


def _rms_modulate(x, g, scale, shift):
    ms = jnp.mean(x * x, axis=-1, keepdims=True)
    return x * lax.rsqrt(ms + NORM_EPS) * g * (1.0 + scale) + shift


def _inproj_kernel(x_ref, mod_ref, g_ref, w_ref, o_ref, h_ref, *, tm):
    @pl.when(pl.program_id(1) == 0)
    def _():
        def body(c, carry):
            rows = pl.ds(pl.multiple_of(c * ROW_CHUNK, ROW_CHUNK), ROW_CHUNK)
            h = _rms_modulate(x_ref[rows, :], g_ref[...], mod_ref[0, 1:2, :], mod_ref[0, 0:1, :])
            h_ref[rows, :] = h.astype(BF16)
            return carry
        lax.fori_loop(0, tm // ROW_CHUNK, body, 0)

    o_ref[...] = jnp.dot(h_ref[...], w_ref[...], preferred_element_type=F32)


def in_proj(x2d, mod, norm_g, w_p, rows_per_mod, mod_base):
    m, d = x2d.shape
    n = w_p.shape[1]
    tm, tn = min(1024, rows_per_mod), 512
    row_of = _mod_row_map(tm, rows_per_mod, mod_base)
    return pl.pallas_call(
        functools.partial(_inproj_kernel, tm=tm),
        grid=(m // tm, n // tn),
        in_specs=[pl.BlockSpec((tm, d), lambda i, j: (i, 0)),
                  pl.BlockSpec((1, N_MOD, d), lambda i, j: (row_of(i), 0, 0)),
                  pl.BlockSpec((1, d), lambda i, j: (0, 0)),
                  pl.BlockSpec((d, tn), lambda i, j: (0, j))],
        out_specs=pl.BlockSpec((tm, tn), lambda i, j: (i, j)),
        out_shape=jax.ShapeDtypeStruct((m, n), F32),
        scratch_shapes=[pltpu.VMEM((tm, d), BF16)],
        compiler_params=_params("arbitrary", "arbitrary"),
        name="in_proj",
    )(x2d, mod, norm_g.reshape(1, d), w_p)


def _shifted(ref, mu, c, n_chunks, t_len, grid_shift):
    ch = ROW_CHUNK
    base = pl.multiple_of(c * ch, ch)
    cur = ref[0, pl.ds(base, ch), :]
    row = lax.broadcasted_iota(jnp.int32, (ch, 1), 0)
    before = pltpu.roll(cur, 1, 0)
    after = pltpu.roll(cur, ch - 1, 0)
    if not grid_shift:
        prev = jnp.where(row == 0, 0.0, before)
        nxt = jnp.where(row == ch - 1, 0.0, after)
        mixed = 0.5 * (prev + nxt)
    else:
        gw = 64
        left = jnp.where(row % gw == 0, 0.0, before)
        right = jnp.where(row % gw == gw - 1, 0.0, after)
        up_start = pl.multiple_of(jnp.maximum(base - gw, 0), gw)
        dn_start = pl.multiple_of(jnp.minimum(base + ch, t_len - gw), gw)
        up_halo = jnp.where(c > 0, ref[0, pl.ds(up_start, gw), :], 0.0)
        dn_halo = jnp.where(c < n_chunks - 1, ref[0, pl.ds(dn_start, gw), :], 0.0)
        up = jnp.concatenate([up_halo, cur[: ch - gw]], axis=0)
        down = jnp.concatenate([cur[gw:], dn_halo], axis=0)
        mixed = 0.25 * (up + down + left + right)
    return cur + mu * (mixed - cur)


def _prescan_kernel(r_ref, k_ref, v_ref, lora_ref, mur_ref, muk_ref, muv_ref, mul_ref,
                    up_ref, bias_ref, kk_ref, ka_ref, rk_ref,
                    r_o, v_o, a_o, bonus_o, w_o, kd_o, bd_o, *, t_len, grid_shift):
    n_chunks = t_len // ROW_CHUNK
    seg = _seg_ones()
    lane = lax.broadcasted_iota(jnp.int32, (1, LORA_W), 1)

    def body(c, carry):
        sh = functools.partial(_shifted, c=c, n_chunks=n_chunks, t_len=t_len, grid_shift=grid_shift)
        r = sh(r_ref, mur_ref[...])
        k = sh(k_ref, muk_ref[...])
        v = sh(v_ref, muv_ref[...])
        lora = sh(lora_ref, mul_ref[...])
        lora = jnp.where(lane < 2 * LORA_R, jnp.tanh(lora), lora)
        raw = jnp.dot(lora, up_ref[0], precision=HIGHEST, preferred_element_type=F32) + bias_ref[0]
        kk = k * kk_ref[...]
        kk = kk * lax.rsqrt(_seg_sum(kk * kk, seg) + 1e-12)
        rows = pl.ds(pl.multiple_of(c * ROW_CHUNK, ROW_CHUNK), ROW_CHUNK)
        kd_sum = jnp.zeros_like(k)
        for d in range(2):
            decay = jnp.exp(-DECAY_SCALE * jax.nn.sigmoid(raw[:, d * LANES:(d + 1) * LANES]))
            iclr = jax.nn.sigmoid(raw[:, (2 + d) * LANES:(3 + d) * LANES])
            kd = k * (1.0 + (iclr - 1.0) * ka_ref[...])
            w_o[d, 0, rows, :] = decay
            kd_o[d, 0, rows, :] = kd
            bd_o[d, 0, rows, :] = kk * iclr
            kd_sum = kd_sum + kd
        r_o[0, rows, :] = r
        v_o[0, rows, :] = v
        a_o[0, rows, :] = -kk
        bonus_o[0, rows, :] = _seg_sum(r * (0.5 * kd_sum) * rk_ref[...], seg) * v
        return carry

    lax.fori_loop(0, n_chunks, body, 0)


def pre_scan(proj, mu, up_w, up_b, k_k, k_a, r_k, grid_shift):
    b, t, _ = proj.shape
    if not grid_shift:
        assert t == ROW_CHUNK, "sequence shift handles one chunk per sequence"
    assert t % ROW_CHUNK == 0
    col = lambda off: pl.BlockSpec((1, t, LANES), lambda i, j: (i, 0, off + j))
    vec = lambda off: pl.BlockSpec((1, LANES), lambda i, j: (0, off + j))
    out1 = pl.BlockSpec((1, t, LANES), lambda i, j: (i, 0, j))
    out2 = pl.BlockSpec((2, 1, t, LANES), lambda i, j: (0, i, 0, j))
    s1 = jax.ShapeDtypeStruct((b, t, RWKV_W), F32)
    s2 = jax.ShapeDtypeStruct((2, b, t, RWKV_W), F32)
    return pl.pallas_call(
        functools.partial(_prescan_kernel, t_len=t, grid_shift=grid_shift),
        grid=(b, HEAD_TILES),
        in_specs=[col(0), col(HEAD_TILES), col(2 * HEAD_TILES),
                  pl.BlockSpec((1, t, LORA_W), lambda i, j: (i, 0, LORA_OFF // LORA_W)),
                  vec(0), vec(HEAD_TILES), vec(2 * HEAD_TILES),
                  pl.BlockSpec((1, LORA_W), lambda i, j: (0, LORA_OFF // LORA_W)),
                  pl.BlockSpec((1, LORA_W, 4 * LANES), lambda i, j: (j, 0, 0)),
                  pl.BlockSpec((1, 1, 4 * LANES), lambda i, j: (j, 0, 0)),
                  vec(0), vec(0), vec(0)],
        out_specs=[out1, out1, out1, out1, out2, out2, out2],
        out_shape=[s1, s1, s1, s1, s2, s2, s2],
        compiler_params=_params("arbitrary", "arbitrary"),
        name="pre_scan",
    )(proj, proj, proj, proj, mu, mu, mu, mu, up_w, up_b,
      k_k.reshape(1, RWKV_W), k_a.reshape(1, RWKV_W), r_k.reshape(1, RWKV_W))


def _lora_up_tables(w_decay_up, w_decay0, w_iclr_up, w_iclr0):
    mats = [w_decay_up[0], w_decay_up[1], w_iclr_up[0], w_iclr_up[1]]
    bias = [w_decay0[0], w_decay0[1], w_iclr0[0], w_iclr0[1]]
    up = jnp.zeros((HEAD_TILES, LORA_W, 4 * LANES), F32)
    for q, m in enumerate(mats):
        blk = m.astype(F32).reshape(LORA_R, HEAD_TILES, LANES).transpose(1, 0, 2)
        up = up.at[:, q * LORA_R:(q + 1) * LORA_R, q * LANES:(q + 1) * LANES].set(blk)
    b = jnp.stack([v.astype(F32).reshape(HEAD_TILES, LANES) for v in bias], axis=1)
    return up, b.reshape(HEAD_TILES, 1, 4 * LANES)


def _scan_kernel(w_ref, k_ref, b_ref, a_ref, r_ref, v_ref, s0_ref, y_ref, sfin_ref,
                 s_scr, wr_scr, *, tc):
    c = pl.program_id(1)

    @pl.when(c == 0)
    def _():
        s_scr[...] = s0_ref[0]

    def row(ref, t, k):
        return ref[t, k:k + 1, :]

    wr_scr[0] = w_ref[0] * r_ref[0]
    sa = jnp.zeros((HEAD_N, LANES), F32)
    y0 = jnp.zeros((HEAD_N, LANES), F32)
    for k in range(HEAD_N):
        s = s_scr[k]
        sa = sa + s * row(a_ref, 0, k)
        y0 = y0 + s * wr_scr[0, k:k + 1, :]

    def step(t, carry):
        sa, y0 = carry
        r_t = r_ref[t]
        v_t = v_ref[t]
        br = jnp.sum(b_ref[t] * r_t, axis=0, keepdims=True)
        kr = jnp.sum(k_ref[t] * r_t, axis=0, keepdims=True)
        y_ref[t] = y0 + sa * br + v_t * kr
        tn = jnp.minimum(t + 1, tc - 1)
        slot = (t + 1) % 2
        wr_scr[slot] = w_ref[tn] * r_ref[tn]
        sa_n = jnp.zeros((HEAD_N, LANES), F32)
        y0_n = jnp.zeros((HEAD_N, LANES), F32)
        for k in range(HEAD_N):
            s = s_scr[k] * row(w_ref, t, k) + sa * row(b_ref, t, k) + v_t * row(k_ref, t, k)
            s_scr[k] = s
            sa_n = sa_n + s * row(a_ref, tn, k)
            y0_n = y0_n + s * wr_scr[slot, k:k + 1, :]
        return sa_n, y0_n

    lax.fori_loop(0, tc, step, (sa, y0))

    @pl.when(c == pl.num_programs(1) - 1)
    def _():
        sfin_ref[0] = s_scr[...]


def wkv_scan(w, k, b, a, r, v, s0, tc=32):
    t, _, lanes = w.shape
    groups = lanes // LANES
    seq = pl.BlockSpec((tc, HEAD_N, LANES), lambda g, c: (c, 0, g))
    st = pl.BlockSpec((1, HEAD_N, HEAD_N, LANES), lambda g, c: (g, 0, 0, 0))
    return pl.pallas_call(
        functools.partial(_scan_kernel, tc=tc),
        grid=(groups, t // tc),
        in_specs=[seq] * 6 + [st],
        out_specs=[seq, st],
        out_shape=[jax.ShapeDtypeStruct((t, HEAD_N, lanes), F32),
                   jax.ShapeDtypeStruct((groups, HEAD_N, HEAD_N, LANES), F32)],
        scratch_shapes=[pltpu.VMEM((HEAD_N, HEAD_N, LANES), F32),
                        pltpu.VMEM((2, HEAD_N, LANES), F32)],
        compiler_params=_params("arbitrary", "arbitrary"),
        name="wkv_scan",
    )(w, k, b, a, r, v, s0)


def _to_scan_layout(x_fwd, x_bwd):
    b, t, _ = x_fwd.shape
    xs = jnp.stack([x_fwd, jnp.flip(x_bwd, 1)])
    xs = xs.reshape(2, b, t, N_HEADS, HEAD_N).transpose(2, 4, 0, 1, 3)
    return xs.reshape(t, HEAD_N, 2 * b * N_HEADS)


def _from_scan_layout(y, b):
    t = y.shape[0]
    ys = y.reshape(t, HEAD_N, 2, b, N_HEADS).transpose(2, 3, 0, 4, 1).reshape(2, b, t, RWKV_W)
    return ys[0] + jnp.flip(ys[1], 1)


def _postscan_kernel(y_ref, bonus_ref, g_ref, gw_ref, gb_ref, o_ref):
    seg = _seg_ones()
    y = y_ref[0]
    mu = _seg_sum(y, seg) * (1.0 / HEAD_N)
    d = y - mu
    var = _seg_sum(d * d, seg) * (1.0 / HEAD_N)
    yn = d * lax.rsqrt(var + GN_EPS) * gw_ref[...] + gb_ref[...]
    o_ref[0] = ((yn + bonus_ref[0]) * jax.nn.sigmoid(g_ref[0])).astype(BF16)


def post_scan(y, bonus, proj, gn_w, gn_b):
    b, t, _ = y.shape
    ch = ROW_CHUNK
    blk = pl.BlockSpec((1, ch, LANES), lambda i, c, j: (i, c, j))
    vec = pl.BlockSpec((1, LANES), lambda i, c, j: (0, j))
    return pl.pallas_call(
        _postscan_kernel,
        grid=(b, t // ch, HEAD_TILES),
        in_specs=[blk, blk,
                  pl.BlockSpec((1, ch, LANES), lambda i, c, j: (i, c, G_OFF // LANES + j)),
                  vec, vec],
        out_specs=blk,
        out_shape=jax.ShapeDtypeStruct((b, t, RWKV_W), BF16),
        compiler_params=_params("arbitrary", "arbitrary", "arbitrary"),
        name="post_scan",
    )(y, bonus, proj, gn_w.reshape(1, RWKV_W), gn_b.reshape(1, RWKV_W))


def _fnet_kernel(f_ref, cc_ref, ct_ref, o_ref, xcs_ref, *, t_len, scale):
    @pl.when(pl.program_id(2) == 0)
    def _():
        xc = jnp.dot(f_ref[0].astype(BF16), cc_ref[...], preferred_element_type=F32)
        xcs_ref[0:t_len, :] = xc[:, :FOURIER_GW].astype(BF16)
        xcs_ref[t_len:2 * t_len, :] = xc[:, FOURIER_GW:].astype(BF16)

    out = jnp.dot(ct_ref[...], xcs_ref[...], preferred_element_type=F32)
    o_ref[0] = (out * scale).astype(BF16)


def _dft_tables(n):
    j = lax.broadcasted_iota(jnp.int32, (n, n), 0)
    k = lax.broadcasted_iota(jnp.int32, (n, n), 1)
    ang = ((j * k) % n).astype(F32) * (2.0 * math.pi / n)
    return jnp.cos(ang), jnp.sin(ang)


def fnet_mix(proj):
    b, t, _ = proj.shape
    cc, sc = _dft_tables(FOURIER_GW)
    ct, st = _dft_tables(t)
    cc2 = jnp.concatenate([cc, sc], axis=1).astype(BF16)
    ct2 = jnp.concatenate([ct, -st], axis=1).astype(BF16)
    tq = min(512, t)
    return pl.pallas_call(
        functools.partial(_fnet_kernel, t_len=t, scale=1.0 / math.sqrt(t * FOURIER_GW)),
        grid=(b, N_FOURIER_GROUPS, t // tq),
        in_specs=[pl.BlockSpec((1, t, FOURIER_GW), lambda i, g, q: (i, 0, F_OFF // FOURIER_GW + g)),
                  pl.BlockSpec((FOURIER_GW, 2 * FOURIER_GW), lambda i, g, q: (0, 0)),
                  pl.BlockSpec((tq, 2 * t), lambda i, g, q: (q, 0))],
        out_specs=pl.BlockSpec((1, tq, FOURIER_GW), lambda i, g, q: (i, q, g)),
        out_shape=jax.ShapeDtypeStruct((b, t, RWKV_W), BF16),
        scratch_shapes=[pltpu.VMEM((2 * t, FOURIER_GW), BF16)],
        compiler_params=_params("arbitrary", "arbitrary", "arbitrary"),
        name="fnet_mix",
    )(proj, cc2, ct2)


def _outproj_kernel(yr_ref, yf_ref, w_ref, x_ref, mod_ref, o_ref):
    m = jnp.dot(yr_ref[...], w_ref[0:RWKV_W, :], preferred_element_type=F32)
    m = m + jnp.dot(yf_ref[...], w_ref[RWKV_W:2 * RWKV_W, :], preferred_element_type=F32)
    o_ref[...] = x_ref[...] + mod_ref[0, 2:3, :] * m


def out_proj(yr, yf, w_bf, x2d, mod, rows_per_mod, mod_base):
    m, d = x2d.shape
    tm, tn = min(1024, rows_per_mod), 512
    row_of = _mod_row_map(tm, rows_per_mod, mod_base)
    return pl.pallas_call(
        _outproj_kernel,
        grid=(m // tm, d // tn),
        in_specs=[pl.BlockSpec((tm, RWKV_W), lambda i, j: (i, 0)),
                  pl.BlockSpec((tm, RWKV_W), lambda i, j: (i, 0)),
                  pl.BlockSpec((2 * RWKV_W, tn), lambda i, j: (0, j)),
                  pl.BlockSpec((tm, tn), lambda i, j: (i, j)),
                  pl.BlockSpec((1, N_MOD, tn), lambda i, j: (row_of(i), 0, j))],
        out_specs=pl.BlockSpec((tm, tn), lambda i, j: (i, j)),
        out_shape=jax.ShapeDtypeStruct((m, d), F32),
        compiler_params=_params("arbitrary", "arbitrary"),
        name="out_proj",
    )(yr, yf, w_bf, x2d, mod)


def _norm2_router_kernel(x_ref, mod_ref, g_ref, rw_ref, h_ref, aff_ref):
    h = _rms_modulate(x_ref[...], g_ref[...], mod_ref[0, 4:5, :], mod_ref[0, 3:4, :])
    h_ref[...] = h.astype(BF16)
    logits = jnp.dot(h, rw_ref[...], precision=HIGHEST, preferred_element_type=F32)
    z = logits - jnp.max(logits, axis=-1, keepdims=True)
    e = jnp.exp(z)
    aff_ref[...] = e / jnp.sum(e, axis=-1, keepdims=True)


def norm2_router(x2d, mod, norm_g, router_w, rows_per_mod, mod_base):
    m, d = x2d.shape
    tm = ROW_CHUNK
    row_of = _mod_row_map(tm, rows_per_mod, mod_base)
    return pl.pallas_call(
        _norm2_router_kernel,
        grid=(m // tm,),
        in_specs=[pl.BlockSpec((tm, d), lambda i: (i, 0)),
                  pl.BlockSpec((1, N_MOD, d), lambda i: (row_of(i), 0, 0)),
                  pl.BlockSpec((1, d), lambda i: (0, 0)),
                  pl.BlockSpec((d, N_EXPERTS), lambda i: (0, 0))],
        out_specs=[pl.BlockSpec((tm, d), lambda i: (i, 0)),
                   pl.BlockSpec((tm, N_EXPERTS), lambda i: (i, 0))],
        out_shape=[jax.ShapeDtypeStruct((m, d), BF16),
                   jax.ShapeDtypeStruct((m, N_EXPERTS), F32)],
        compiler_params=_params("arbitrary"),
        name="norm2_router",
    )(x2d, mod, norm_g.reshape(1, d), router_w)


def _expert_kernel(xe_ref, w1_ref, w3_ref, w2_ref, gate_ref, o_ref, *, tf, ff):
    f = pl.program_id(1)
    x = xe_ref[0]
    h1 = jnp.dot(x, w1_ref[0].astype(BF16), preferred_element_type=F32)
    h3 = jnp.dot(x, w3_ref[0].astype(BF16), preferred_element_type=F32)
    hid = h1 * jax.nn.sigmoid(h1) * h3
    valid = ff - f * tf
    hid = jnp.where(lax.broadcasted_iota(jnp.int32, (1, tf), 1) < valid, hid, 0.0)
    w2 = jnp.where(lax.broadcasted_iota(jnp.int32, (tf, 1), 0) < valid, w2_ref[0], 0.0)
    part = jnp.dot(hid.astype(BF16), w2.astype(BF16), preferred_element_type=F32)

    @pl.when(f == 0)
    def _():
        o_ref[0] = part

    @pl.when(f > 0)
    def _():
        o_ref[0] += part

    @pl.when(f == pl.num_programs(1) - 1)
    def _():
        o_ref[0] = o_ref[0] * gate_ref[0]


def expert_ffn(xe, w1, w3, w2, gate):
    e, cap, d = xe.shape
    ff = w1.shape[2]
    tf = 256
    nf = pl.cdiv(ff, tf)
    return pl.pallas_call(
        functools.partial(_expert_kernel, tf=tf, ff=ff),
        grid=(e, nf),
        in_specs=[pl.BlockSpec((1, cap, d), lambda i, f: (i, 0, 0)),
                  pl.BlockSpec((1, d, tf), lambda i, f: (i, 0, f)),
                  pl.BlockSpec((1, d, tf), lambda i, f: (i, 0, f)),
                  pl.BlockSpec((1, tf, d), lambda i, f: (i, f, 0)),
                  pl.BlockSpec((1, cap, 1), lambda i, f: (i, 0, 0))],
        out_specs=pl.BlockSpec((1, cap, d), lambda i, f: (i, 0, 0)),
        out_shape=jax.ShapeDtypeStruct((e, cap, d), F32),
        compiler_params=_params("arbitrary", "arbitrary"),
        name="expert_ffn",
    )(xe, w1, w3, w2, gate.reshape(e, cap, 1))


def _final_kernel(x_ref, f_ref, mod_ref, g_ref, o_ref):
    x = x_ref[...] + mod_ref[0, 5:6, :] * f_ref[...]
    ms = jnp.mean(x * x, axis=-1, keepdims=True)
    o_ref[...] = x * lax.rsqrt(ms + NORM_EPS) * g_ref[...]


def final_norm(x2d, ffn, mod, final_g, rows_per_mod, mod_base):
    m, d = x2d.shape
    tm = ROW_CHUNK
    row_of = _mod_row_map(tm, rows_per_mod, mod_base)
    blk = pl.BlockSpec((tm, d), lambda i: (i, 0))
    return pl.pallas_call(
        _final_kernel,
        grid=(m // tm,),
        in_specs=[blk, blk,
                  pl.BlockSpec((1, N_MOD, d), lambda i: (row_of(i), 0, 0)),
                  pl.BlockSpec((1, d), lambda i: (0, 0))],
        out_specs=blk,
        out_shape=jax.ShapeDtypeStruct((m, d), F32),
        compiler_params=_params("arbitrary"),
        name="final_norm",
    )(x2d, ffn, mod, final_g.reshape(1, d))


def _layer(x, mod, rows_per_mod, mod_base, s0, grid_shift, p):
    b, t, d = x.shape
    x2d = x.reshape(b * t, d)
    proj = in_proj(x2d, mod, p["norm1_g"], p["w_in_p"], rows_per_mod, mod_base)
    proj = proj.reshape(b, t, IN_PAD_W)
    r, v, a, bonus, w, kd, bd = pre_scan(proj, p["mu"], p["up_w"], p["up_b"],
                                         p["k_k"], p["k_a"], p["r_k"], grid_shift)
    y, s_fin = wkv_scan(_to_scan_layout(w[0], w[1]), _to_scan_layout(kd[0], kd[1]),
                        _to_scan_layout(bd[0], bd[1]), _to_scan_layout(a, a),
                        _to_scan_layout(r, r), _to_scan_layout(v, v), s0)
    yr = post_scan(_from_scan_layout(y, b), bonus, proj, p["gn_w"], p["gn_b"])
    yf = fnet_mix(proj)
    x1 = out_proj(yr.reshape(b * t, RWKV_W), yf.reshape(b * t, RWKV_W), p["w_out_bf"],
                  x2d, mod, rows_per_mod, mod_base)
    h2, aff = norm2_router(x1, mod, p["norm2_g"], p["router_w"], rows_per_mod, mod_base)
    n = b * t
    cap = n * CAPACITY_FACTOR // N_EXPERTS
    gate, idx = lax.top_k(aff.T, cap)
    ye = expert_ffn(h2[idx], p["exp_w1"], p["exp_w3"], p["exp_w2"], gate)
    ffn = jnp.zeros((n, d), F32).at[idx.reshape(-1)].add(ye.reshape(-1, d))
    out = final_norm(x1, ffn, mod, p["final_norm_g"], rows_per_mod, mod_base)
    return out.reshape(b, t, d), s_fin


def kernel(x_prompt, x_sample, state_wkv, c, c_ctx, ada_w, ada_b, norm1_g, norm2_g, w_in, shift_mu,
           w_decay_up, w_decay0, w_iclr_up, w_iclr0, k_k, k_a, r_k, gn_w, gn_b, w_out, router_w,
           exp_w1, exp_w3, exp_w2, final_norm_g):
    depth = ada_w.shape[0]
    assert depth == 1, "single-layer trunk"
    l = 0
    bc, tc_len, d = x_prompt.shape
    bl, tl, _ = x_sample.shape
    cvecs = jnp.concatenate([c_ctx[None], c], axis=0)
    mod = ada_mod(cvecs, ada_w[l], ada_b[l]).reshape(1 + bl, N_MOD, d)
    zpad = jnp.zeros((d, LANES), F32)
    up_w, up_b = _lora_up_tables(w_decay_up[l], w_decay0[l], w_iclr_up[l], w_iclr0[l])
    p = {
        "norm1_g": norm1_g[l], "norm2_g": norm2_g[l], "final_norm_g": final_norm_g,
        "w_in_p": jnp.concatenate([w_in[l][:, :SHIFT_W], zpad, w_in[l][:, SHIFT_W:]], axis=1).astype(BF16),
        "mu": shift_mu[l].reshape(1, SHIFT_W), "up_w": up_w, "up_b": up_b,
        "k_k": k_k[l], "k_a": k_a[l], "r_k": r_k[l], "gn_w": gn_w[l], "gn_b": gn_b[l],
        "w_out_bf": w_out[l].astype(BF16), "router_w": router_w[l],
        "exp_w1": exp_w1[l], "exp_w3": exp_w3[l], "exp_w2": exp_w2[l],
    }
    groups_c = 2 * bc * N_HEADS // LANES
    s0_ctx = jnp.zeros((groups_c, HEAD_N, HEAD_N, LANES), F32)
    y_prompt, s_ctx = _layer(x_prompt, mod, bc * tc_len, 0, s0_ctx, False, p)
    s_ctx = s_ctx.transpose(1, 2, 0, 3).reshape(HEAD_N, HEAD_N, 2, bc, N_HEADS)
    new_state = s_ctx.transpose(3, 2, 4, 1, 0)[:, None]
    groups_l = 2 * bl * N_HEADS // LANES
    s0_lat = state_wkv[:, l].astype(F32).transpose(4, 3, 1, 0, 2)
    s0_lat = s0_lat.reshape(HEAD_N, HEAD_N, groups_l, LANES).transpose(2, 0, 1, 3)
    y_sample, _ = _layer(x_sample, mod, tl, 1, s0_lat, True, p)
    return (y_prompt, y_sample, new_state)
```

```python
import functools
import math

import jax
import jax.numpy as jnp
from jax import lax
from jax.experimental import pallas as pl
from jax.experimental.pallas import tpu as pltpu

F32 = jnp.float32
BF16 = jnp.bfloat16
HIGHEST = lax.Precision.HIGHEST

D_MODEL = 2048
RWKV_W = 1024
HEAD_N = 64
N_HEADS = 16
LORA_R = 96
N_EXPERTS = 16
EXPERT_FF = 5504
CAPACITY_FACTOR = 2
N_MOD = 6
NORM_EPS = 1e-6
GN_EPS = 64e-5
FOURIER_GW = 256
N_FOURIER_GROUPS = 4

LANES = 128
HEAD_TILES = RWKV_W // LANES
SHIFT_W = 3 * RWKV_W + 4 * LORA_R
LORA_W = 4 * LORA_R
LORA_BLK = LORA_W + LANES
G_OFF = 3 * RWKV_W
F_OFF = G_OFF + RWKV_W
LORA_OFF = F_OFF + RWKV_W
IN_PAD_W = LORA_OFF + LORA_BLK
ROW_CHUNK = 256
VMEM_LIMIT = 56 * 1024 * 1024
DECAY_SCALE = math.exp(-0.5)


def _params(*sem):
    return pltpu.CompilerParams(dimension_semantics=sem, vmem_limit_bytes=VMEM_LIMIT)


def _seg_ones():
    r = lax.broadcasted_iota(jnp.int32, (LANES, LANES), 0) // HEAD_N
    c = lax.broadcasted_iota(jnp.int32, (LANES, LANES), 1) // HEAD_N
    return (r == c).astype(BF16)


def _split2(x):
    hi = x.astype(BF16)
    return hi, (x - hi.astype(F32)).astype(BF16)


def _split3(x):
    hi = x.astype(BF16)
    rem = x - hi.astype(F32)
    mid = rem.astype(BF16)
    return hi, mid, (rem - mid.astype(F32)).astype(BF16)


def _seg_sum(x, seg):
    return sum(jnp.dot(p, seg, preferred_element_type=F32) for p in _split3(x))


def _ada_kernel(c_ref, w_ref, b_ref, o_ref):
    c = c_ref[...]
    s = c * jax.nn.sigmoid(c)
    o_ref[...] = jnp.dot(s.astype(BF16), w_ref[...].astype(BF16),
                         preferred_element_type=F32) + b_ref[...]


def ada_mod(cvecs, ada_w, ada_b):
    rows, d = cvecs.shape
    n = ada_w.shape[1]
    tn = 1024
    return pl.pallas_call(
        _ada_kernel,
        grid=(n // tn,),
        in_specs=[pl.BlockSpec((rows, d), lambda j: (0, 0)),
                  pl.BlockSpec((d, tn), lambda j: (0, j)),
                  pl.BlockSpec((1, tn), lambda j: (0, j))],
        out_specs=pl.BlockSpec((rows, tn), lambda j: (0, j)),
        out_shape=jax.ShapeDtypeStruct((rows, n), F32),
        compiler_params=_params("arbitrary"),
        name="ada_mod",
    )(cvecs, ada_w, ada_b.reshape(1, n))


def _mod_row_map(tm, rows_per_mod, mod_base):
    return lambda i: mod_base + (i * tm) // rows_per_mod


def _rms_modulate(x, g, scale, shift):
    ms = jnp.mean(x * x, axis=-1, keepdims=True)
    return x * lax.rsqrt(ms + NORM_EPS) * g * (1.0 + scale) + shift


def _inproj_kernel(x_ref, mod_ref, g_ref, w_ref, o_ref, h_ref, *, tm):
    @pl.when(pl.program_id(1) == 0)
    def _():
        def body(c, carry):
            rows = pl.ds(pl.multiple_of(c * ROW_CHUNK, ROW_CHUNK), ROW_CHUNK)
            h = _rms_modulate(x_ref[rows, :], g_ref[...], mod_ref[0, 1:2, :], mod_ref[0, 0:1, :])
            h_ref[rows, :] = h.astype(BF16)
            return carry
        lax.fori_loop(0, tm // ROW_CHUNK, body, 0)

    o_ref[...] = jnp.dot(h_ref[...], w_ref[...], preferred_element_type=F32)


def in_proj(x2d, mod, norm_g, w_p, rows_per_mod, mod_base):
    m, d = x2d.shape
    n = w_p.shape[1]
    tm, tn = min(1024, rows_per_mod), 512
    row_of = _mod_row_map(tm, rows_per_mod, mod_base)
    return pl.pallas_call(
        functools.partial(_inproj_kernel, tm=tm),
        grid=(m // tm, n // tn),
        in_specs=[pl.BlockSpec((tm, d), lambda i, j: (i, 0)),
                  pl.BlockSpec((1, N_MOD, d), lambda i, j: (row_of(i), 0, 0)),
                  pl.BlockSpec((1, d), lambda i, j: (0, 0)),
                  pl.BlockSpec((d, tn), lambda i, j: (0, j))],
        out_specs=pl.BlockSpec((tm, tn), lambda i, j: (i, j)),
        out_shape=jax.ShapeDtypeStruct((m, n), F32),
        scratch_shapes=[pltpu.VMEM((tm, d), BF16)],
        compiler_params=_params("arbitrary", "arbitrary"),
        name="in_proj",
    )(x2d, mod, norm_g.reshape(1, d), w_p)


def _shifted(ref, mu, c, n_chunks, t_len, grid_shift):
    ch = ROW_CHUNK
    base = pl.multiple_of(c * ch, ch)
    cur = ref[0, pl.ds(base, ch), :]
    row = lax.broadcasted_iota(jnp.int32, (ch, 1), 0)
    before = pltpu.roll(cur, 1, 0)
    after = pltpu.roll(cur, ch - 1, 0)
    if not grid_shift:
        prev = jnp.where(row == 0, 0.0, before)
        nxt = jnp.where(row == ch - 1, 0.0, after)
        mixed = 0.5 * (prev + nxt)
    else:
        gw = 64
        left = jnp.where(row % gw == 0, 0.0, before)
        right = jnp.where(row % gw == gw - 1, 0.0, after)
        up_start = pl.multiple_of(jnp.maximum(base - gw, 0), gw)
        dn_start = pl.multiple_of(jnp.minimum(base + ch, t_len - gw), gw)
        up_halo = jnp.where(c > 0, ref[0, pl.ds(up_start, gw), :], 0.0)
        dn_halo = jnp.where(c < n_chunks - 1, ref[0, pl.ds(dn_start, gw), :], 0.0)
        up = jnp.concatenate([up_halo, cur[: ch - gw]], axis=0)
        down = jnp.concatenate([cur[gw:], dn_halo], axis=0)
        mixed = 0.25 * (up + down + left + right)
    return cur + mu * (mixed - cur)


def _prescan_kernel(r_ref, k_ref, v_ref, lora_ref, mur_ref, muk_ref, muv_ref, mul_ref,
                    uph_ref, upl_ref, bias_ref, kk_ref, ka_ref, rk_ref,
                    r_o, v_o, a_o, bonus_o, w_o, kd_o, bd_o, *, t_len, grid_shift):
    n_chunks = t_len // ROW_CHUNK
    seg = _seg_ones()
    lane = lax.broadcasted_iota(jnp.int32, (1, LORA_BLK), 1)

    def body(c, carry):
        sh = functools.partial(_shifted, c=c, n_chunks=n_chunks, t_len=t_len, grid_shift=grid_shift)
        r = sh(r_ref, mur_ref[...])
        k = sh(k_ref, muk_ref[...])
        v = sh(v_ref, muv_ref[...])
        lora = sh(lora_ref, mul_ref[...])
        lora = jnp.where(lane < 2 * LORA_R, jnp.tanh(lora), lora)
        l_hi, l_lo = _split2(lora)
        raw = (jnp.dot(l_hi, uph_ref[0], preferred_element_type=F32)
               + jnp.dot(l_hi, upl_ref[0], preferred_element_type=F32)
               + jnp.dot(l_lo, uph_ref[0], preferred_element_type=F32)) + bias_ref[0]
        kk = k * kk_ref[...]
        kk = kk * lax.rsqrt(_seg_sum(kk * kk, seg) + 1e-12)
        rows = pl.ds(pl.multiple_of(c * ROW_CHUNK, ROW_CHUNK), ROW_CHUNK)
        kd_sum = jnp.zeros_like(k)
        for d in range(2):
            decay = jnp.exp(-DECAY_SCALE * jax.nn.sigmoid(raw[:, d * LANES:(d + 1) * LANES]))
            iclr = jax.nn.sigmoid(raw[:, (2 + d) * LANES:(3 + d) * LANES])
            kd = k * (1.0 + (iclr - 1.0) * ka_ref[...])
            w_o[d, 0, rows, :] = decay
            kd_o[d, 0, rows, :] = kd
            bd_o[d, 0, rows, :] = kk * iclr
            kd_sum = kd_sum + kd
        r_o[0, rows, :] = r
        v_o[0, rows, :] = v
        a_o[0, rows, :] = -kk
        bonus_o[0, rows, :] = _seg_sum(r * (0.5 * kd_sum) * rk_ref[...], seg) * v
        return carry

    lax.fori_loop(0, n_chunks, body, 0)


def pre_scan(proj, mu, up_w, up_b, k_k, k_a, r_k, grid_shift):
    b, t, _ = proj.shape
    up_hi, up_lo = _split2(up_w)
    if not grid_shift:
        assert t == ROW_CHUNK, "sequence shift handles one chunk per sequence"
    assert t % ROW_CHUNK == 0
    col = lambda off: pl.BlockSpec((1, t, LANES), lambda i, j: (i, 0, off + j))
    vec = lambda off: pl.BlockSpec((1, LANES), lambda i, j: (0, off + j))
    out1 = pl.BlockSpec((1, t, LANES), lambda i, j: (i, 0, j))
    out2 = pl.BlockSpec((2, 1, t, LANES), lambda i, j: (0, i, 0, j))
    s1 = jax.ShapeDtypeStruct((b, t, RWKV_W), F32)
    s2 = jax.ShapeDtypeStruct((2, b, t, RWKV_W), F32)
    return pl.pallas_call(
        functools.partial(_prescan_kernel, t_len=t, grid_shift=grid_shift),
        grid=(b, HEAD_TILES),
        in_specs=[col(0), col(HEAD_TILES), col(2 * HEAD_TILES),
                  pl.BlockSpec((1, t, LORA_BLK), lambda i, j: (i, 0, LORA_OFF // LORA_BLK)),
                  vec(0), vec(HEAD_TILES), vec(2 * HEAD_TILES),
                  pl.BlockSpec((1, LORA_BLK), lambda i, j: (0, LORA_OFF // LORA_BLK)),
                  pl.BlockSpec((1, LORA_BLK, 4 * LANES), lambda i, j: (j, 0, 0)),
                  pl.BlockSpec((1, LORA_BLK, 4 * LANES), lambda i, j: (j, 0, 0)),
                  pl.BlockSpec((1, 1, 4 * LANES), lambda i, j: (j, 0, 0)),
                  vec(0), vec(0), vec(0)],
        out_specs=[out1, out1, out1, out1, out2, out2, out2],
        out_shape=[s1, s1, s1, s1, s2, s2, s2],
        compiler_params=_params("arbitrary", "arbitrary"),
        name="pre_scan",
    )(proj, proj, proj, proj, mu, mu, mu, mu, up_hi, up_lo, up_b,
      k_k.reshape(1, RWKV_W), k_a.reshape(1, RWKV_W), r_k.reshape(1, RWKV_W))


def _lora_up_tables(w_decay_up, w_decay0, w_iclr_up, w_iclr0):
    mats = [w_decay_up[0], w_decay_up[1], w_iclr_up[0], w_iclr_up[1]]
    bias = [w_decay0[0], w_decay0[1], w_iclr0[0], w_iclr0[1]]
    up = jnp.zeros((HEAD_TILES, LORA_BLK, 4 * LANES), F32)
    for q, m in enumerate(mats):
        blk = m.astype(F32).reshape(LORA_R, HEAD_TILES, LANES).transpose(1, 0, 2)
        up = up.at[:, q * LORA_R:(q + 1) * LORA_R, q * LANES:(q + 1) * LANES].set(blk)
    b = jnp.stack([v.astype(F32).reshape(HEAD_TILES, LANES) for v in bias], axis=1)
    return up, b.reshape(HEAD_TILES, 1, 4 * LANES)


def _scan_kernel(*refs, tc, mixed):
    n_in = 12 if mixed else 6
    ins, rest = refs[:n_in], refs[n_in:]
    if mixed:
        s0_ref, y_ref, yb_ref, sfin_ref, s_scr, wr_scr = rest
        fwd_lane = lax.broadcasted_iota(jnp.int32, (1, LANES), 1) < LANES // 2
        tt = lambda t: t

        def view(i):
            fw, bw = ins[i], ins[n_in // 2 + i]
            full = lambda t: jnp.where(fwd_lane, fw[t], bw[tc - 1 - t])
            row = lambda t, k: jnp.where(fwd_lane, fw[t, k:k + 1, :], bw[tc - 1 - t, k:k + 1, :])
            return full, row
    else:
        s0_ref, y_ref, sfin_ref, s_scr, wr_scr = rest
        backward = pl.program_id(0) >= pl.num_programs(0) // 2
        tt = lambda t: jnp.where(backward, tc - 1 - t, t)

        def view(i):
            ref = ins[i]
            return (lambda t: ref[tt(t)]), (lambda t, k: ref[tt(t), k:k + 1, :])

    (w_full, w_row), (k_full, k_row), (b_full, b_row), (_, a_row), (r_full, _), (v_full, _) = (
        view(i) for i in range(6))
    c = pl.program_id(1)

    @pl.when(c == 0)
    def _():
        s_scr[...] = s0_ref[0]

    wr_scr[0] = w_full(0) * r_full(0)
    sa = jnp.zeros((HEAD_N, LANES), F32)
    y0 = jnp.zeros((HEAD_N, LANES), F32)
    for k in range(HEAD_N):
        s = s_scr[k]
        sa = sa + s * a_row(0, k)
        y0 = y0 + s * wr_scr[0, k:k + 1, :]

    def step(t, carry):
        sa, y0 = carry
        r_t = r_full(t)
        v_t = v_full(t)
        br = jnp.sum(b_full(t) * r_t, axis=0, keepdims=True)
        kr = jnp.sum(k_full(t) * r_t, axis=0, keepdims=True)
        y = y0 + sa * br + v_t * kr
        y_ref[tt(t)] = y
        if mixed:
            yb_ref[tc - 1 - t] = y
        tn = jnp.minimum(t + 1, tc - 1)
        slot = (t + 1) % 2
        wr_scr[slot] = w_full(tn) * r_full(tn)
        sa_n = jnp.zeros((HEAD_N, LANES), F32)
        y0_n = jnp.zeros((HEAD_N, LANES), F32)
        for k in range(HEAD_N):
            s = s_scr[k] * w_row(t, k) + sa * b_row(t, k) + v_t * k_row(t, k)
            s_scr[k] = s
            sa_n = sa_n + s * a_row(tn, k)
            y0_n = y0_n + s * wr_scr[slot, k:k + 1, :]
        return sa_n, y0_n

    lax.fori_loop(0, tc, step, (sa, y0))

    @pl.when(c == pl.num_programs(1) - 1)
    def _():
        sfin_ref[0] = s_scr[...]


def wkv_scan(ops, s0, tc=32):
    t, _, lanes = ops[0].shape
    groups = lanes // LANES
    nc = t // tc
    mixed = groups == 1
    st = pl.BlockSpec((1, HEAD_N, HEAD_N, LANES), lambda g, c: (g, 0, 0, 0))
    y_shape = jax.ShapeDtypeStruct((t, HEAD_N, lanes), F32)
    if mixed:
        fwd = pl.BlockSpec((tc, HEAD_N, LANES), lambda g, c: (c, 0, g))
        bwd = pl.BlockSpec((tc, HEAD_N, LANES), lambda g, c: (nc - 1 - c, 0, g))
        in_specs, operands = [fwd] * 6 + [bwd] * 6 + [st], list(ops) + list(ops) + [s0]
        out_specs, out_shape = [fwd, bwd, st], [y_shape, y_shape]
    else:
        seq = pl.BlockSpec((tc, HEAD_N, LANES),
                           lambda g, c: (jnp.where(g >= groups // 2, nc - 1 - c, c), 0, g))
        in_specs, operands = [seq] * 6 + [st], list(ops) + [s0]
        out_specs, out_shape = [seq, st], [y_shape]
    out_shape.append(jax.ShapeDtypeStruct((groups, HEAD_N, HEAD_N, LANES), F32))
    res = pl.pallas_call(
        functools.partial(_scan_kernel, tc=tc, mixed=mixed),
        grid=(groups, nc),
        in_specs=in_specs,
        out_specs=out_specs,
        out_shape=out_shape,
        scratch_shapes=[pltpu.VMEM((HEAD_N, HEAD_N, LANES), F32),
                        pltpu.VMEM((2, HEAD_N, LANES), F32)],
        compiler_params=_params("arbitrary", "arbitrary"),
        name="wkv_scan",
    )(*operands)
    half = lanes // 2
    if mixed:
        y_f, y_b, s_fin = res
        return y_f[..., :half] + y_b[..., half:], s_fin
    y, s_fin = res
    return y[..., :half] + y[..., half:], s_fin


def _to_scan_layout(x_fwd, x_bwd):
    b, t, _ = x_fwd.shape
    xs = jnp.stack([x_fwd, x_bwd])
    xs = xs.reshape(2, b, t, N_HEADS, HEAD_N).transpose(2, 4, 0, 1, 3)
    return xs.reshape(t, HEAD_N, 2 * b * N_HEADS)


def _from_scan_layout(y, b):
    t = y.shape[0]
    return y.reshape(t, HEAD_N, b, N_HEADS).transpose(2, 0, 3, 1).reshape(b, t, RWKV_W)


def _postscan_kernel(y_ref, bonus_ref, g_ref, gw_ref, gb_ref, o_ref):
    seg = _seg_ones()
    for j in range(HEAD_TILES):
        cols = slice(j * LANES, (j + 1) * LANES)
        y = y_ref[0, :, cols]
        mu = _seg_sum(y, seg) * (1.0 / HEAD_N)
        d = y - mu
        var = _seg_sum(d * d, seg) * (1.0 / HEAD_N)
        yn = d * lax.rsqrt(var + GN_EPS) * gw_ref[:, cols] + gb_ref[:, cols]
        o_ref[0, :, cols] = ((yn + bonus_ref[0, :, cols]) * jax.nn.sigmoid(g_ref[0, :, cols])).astype(BF16)


def post_scan(y, bonus, proj, gn_w, gn_b):
    b, t, _ = y.shape
    ch = ROW_CHUNK
    blk = pl.BlockSpec((1, ch, RWKV_W), lambda i, c: (i, c, 0))
    vec = pl.BlockSpec((1, RWKV_W), lambda i, c: (0, 0))
    return pl.pallas_call(
        _postscan_kernel,
        grid=(b, t // ch),
        in_specs=[blk, blk,
                  pl.BlockSpec((1, ch, RWKV_W), lambda i, c: (i, c, G_OFF // RWKV_W)),
                  vec, vec],
        out_specs=blk,
        out_shape=jax.ShapeDtypeStruct((b, t, RWKV_W), BF16),
        compiler_params=_params("arbitrary", "arbitrary"),
        name="post_scan",
    )(y, bonus, proj, gn_w.reshape(1, RWKV_W), gn_b.reshape(1, RWKV_W))


def _fnet_kernel(f_ref, cc_ref, ct_ref, o_ref, xcs_ref, *, t_len, scale):
    @pl.when(pl.program_id(2) == 0)
    def _():
        xc = jnp.dot(f_ref[0].astype(BF16), cc_ref[...], preferred_element_type=F32)
        xcs_ref[0:t_len, :] = xc[:, :FOURIER_GW].astype(BF16)
        xcs_ref[t_len:2 * t_len, :] = xc[:, FOURIER_GW:].astype(BF16)

    out = jnp.dot(ct_ref[...], xcs_ref[...], preferred_element_type=F32)
    o_ref[0] = (out * scale).astype(BF16)


def _dft_tables(n):
    j = lax.broadcasted_iota(jnp.int32, (n, n), 0)
    k = lax.broadcasted_iota(jnp.int32, (n, n), 1)
    ang = ((j * k) % n).astype(F32) * (2.0 * math.pi / n)
    return jnp.cos(ang), jnp.sin(ang)


def fnet_mix(proj):
    b, t, _ = proj.shape
    cc, sc = _dft_tables(FOURIER_GW)
    ct, st = _dft_tables(t)
    cc2 = jnp.concatenate([cc, sc], axis=1).astype(BF16)
    ct2 = jnp.concatenate([ct, -st], axis=1).astype(BF16)
    tq = min(512, t)
    return pl.pallas_call(
        functools.partial(_fnet_kernel, t_len=t, scale=1.0 / math.sqrt(t * FOURIER_GW)),
        grid=(b, N_FOURIER_GROUPS, t // tq),
        in_specs=[pl.BlockSpec((1, t, FOURIER_GW), lambda i, g, q: (i, 0, F_OFF // FOURIER_GW + g)),
                  pl.BlockSpec((FOURIER_GW, 2 * FOURIER_GW), lambda i, g, q: (0, 0)),
                  pl.BlockSpec((tq, 2 * t), lambda i, g, q: (q, 0))],
        out_specs=pl.BlockSpec((1, tq, FOURIER_GW), lambda i, g, q: (i, q, g)),
        out_shape=jax.ShapeDtypeStruct((b, t, RWKV_W), BF16),
        scratch_shapes=[pltpu.VMEM((2 * t, FOURIER_GW), BF16)],
        compiler_params=_params("arbitrary", "arbitrary", "arbitrary"),
        name="fnet_mix",
    )(proj, cc2, ct2)


def _outproj_kernel(yr_ref, yf_ref, w_ref, x_ref, mod_ref, o_ref):
    m = jnp.dot(yr_ref[...], w_ref[0:RWKV_W, :], preferred_element_type=F32)
    m = m + jnp.dot(yf_ref[...], w_ref[RWKV_W:2 * RWKV_W, :], preferred_element_type=F32)
    o_ref[...] = x_ref[...] + mod_ref[0, 2:3, :] * m


def out_proj(yr, yf, w_bf, x2d, mod, rows_per_mod, mod_base):
    m, d = x2d.shape
    tm, tn = min(1024, rows_per_mod), 512
    row_of = _mod_row_map(tm, rows_per_mod, mod_base)
    return pl.pallas_call(
        _outproj_kernel,
        grid=(m // tm, d // tn),
        in_specs=[pl.BlockSpec((tm, RWKV_W), lambda i, j: (i, 0)),
                  pl.BlockSpec((tm, RWKV_W), lambda i, j: (i, 0)),
                  pl.BlockSpec((2 * RWKV_W, tn), lambda i, j: (0, j)),
                  pl.BlockSpec((tm, tn), lambda i, j: (i, j)),
                  pl.BlockSpec((1, N_MOD, tn), lambda i, j: (row_of(i), 0, j))],
        out_specs=pl.BlockSpec((tm, tn), lambda i, j: (i, j)),
        out_shape=jax.ShapeDtypeStruct((m, d), F32),
        compiler_params=_params("arbitrary", "arbitrary"),
        name="out_proj",
    )(yr, yf, w_bf, x2d, mod)


def _norm2_router_kernel(x_ref, mod_ref, g_ref, rw_ref, h_ref, aff_ref):
    h = _rms_modulate(x_ref[...], g_ref[...], mod_ref[0, 4:5, :], mod_ref[0, 3:4, :])
    h_ref[...] = h.astype(BF16)
    logits = jnp.dot(h, rw_ref[...], precision=HIGHEST, preferred_element_type=F32)
    z = logits - jnp.max(logits, axis=-1, keepdims=True)
    e = jnp.exp(z)
    aff_ref[...] = e / jnp.sum(e, axis=-1, keepdims=True)


def norm2_router(x2d, mod, norm_g, router_w, rows_per_mod, mod_base):
    m, d = x2d.shape
    tm = ROW_CHUNK
    row_of = _mod_row_map(tm, rows_per_mod, mod_base)
    return pl.pallas_call(
        _norm2_router_kernel,
        grid=(m // tm,),
        in_specs=[pl.BlockSpec((tm, d), lambda i: (i, 0)),
                  pl.BlockSpec((1, N_MOD, d), lambda i: (row_of(i), 0, 0)),
                  pl.BlockSpec((1, d), lambda i: (0, 0)),
                  pl.BlockSpec((d, N_EXPERTS), lambda i: (0, 0))],
        out_specs=[pl.BlockSpec((tm, d), lambda i: (i, 0)),
                   pl.BlockSpec((tm, N_EXPERTS), lambda i: (i, 0))],
        out_shape=[jax.ShapeDtypeStruct((m, d), BF16),
                   jax.ShapeDtypeStruct((m, N_EXPERTS), F32)],
        compiler_params=_params("arbitrary"),
        name="norm2_router",
    )(x2d, mod, norm_g.reshape(1, d), router_w)


def _expert_kernel(xe_ref, w1_ref, w3_ref, w2_ref, gate_ref, o_ref, *, tf, ff):
    f = pl.program_id(1)
    x = xe_ref[0]
    h1 = jnp.dot(x, w1_ref[0].astype(BF16), preferred_element_type=F32)
    h3 = jnp.dot(x, w3_ref[0].astype(BF16), preferred_element_type=F32)
    hid = h1 * jax.nn.sigmoid(h1) * h3
    valid = ff - f * tf
    hid = jnp.where(lax.broadcasted_iota(jnp.int32, (1, tf), 1) < valid, hid, 0.0)
    w2 = jnp.where(lax.broadcasted_iota(jnp.int32, (tf, 1), 0) < valid, w2_ref[0], 0.0)
    @pl.when(f == 0)
    def _():
        o_ref[...] = jnp.zeros_like(o_ref)

    o_ref[0] += jnp.dot(hid.astype(BF16), w2.astype(BF16), preferred_element_type=F32)

    @pl.when(f == pl.num_programs(1) - 1)
    def _():
        o_ref[0] = o_ref[0] * gate_ref[0]


def expert_ffn(xe, w1, w3, w2, gate):
    e, cap, d = xe.shape
    ff = w1.shape[2]
    tf = 256
    nf = pl.cdiv(ff, tf)
    return pl.pallas_call(
        functools.partial(_expert_kernel, tf=tf, ff=ff),
        grid=(e, nf),
        in_specs=[pl.BlockSpec((1, cap, d), lambda i, f: (i, 0, 0)),
                  pl.BlockSpec((1, d, tf), lambda i, f: (i, 0, f)),
                  pl.BlockSpec((1, d, tf), lambda i, f: (i, 0, f)),
                  pl.BlockSpec((1, tf, d), lambda i, f: (i, f, 0)),
                  pl.BlockSpec((1, cap, 1), lambda i, f: (i, 0, 0))],
        out_specs=pl.BlockSpec((1, cap, d), lambda i, f: (i, 0, 0)),
        out_shape=jax.ShapeDtypeStruct((e, cap, d), F32),
        compiler_params=_params("arbitrary", "arbitrary"),
        name="expert_ffn",
    )(xe, w1, w3, w2, gate.reshape(e, cap, 1))


def _final_kernel(x_ref, f_ref, mod_ref, g_ref, o_ref):
    x = x_ref[...] + mod_ref[0, 5:6, :] * f_ref[...]
    ms = jnp.mean(x * x, axis=-1, keepdims=True)
    o_ref[...] = x * lax.rsqrt(ms + NORM_EPS) * g_ref[...]


def final_norm(x2d, ffn, mod, final_g, rows_per_mod, mod_base):
    m, d = x2d.shape
    tm = ROW_CHUNK
    row_of = _mod_row_map(tm, rows_per_mod, mod_base)
    blk = pl.BlockSpec((tm, d), lambda i: (i, 0))
    return pl.pallas_call(
        _final_kernel,
        grid=(m // tm,),
        in_specs=[blk, blk,
                  pl.BlockSpec((1, N_MOD, d), lambda i: (row_of(i), 0, 0)),
                  pl.BlockSpec((1, d), lambda i: (0, 0))],
        out_specs=blk,
        out_shape=jax.ShapeDtypeStruct((m, d), F32),
        compiler_params=_params("arbitrary"),
        name="final_norm",
    )(x2d, ffn, mod, final_g.reshape(1, d))


def _prepare_in_proj(w_in, shift_mu):
    d = w_in.shape[0]
    rkv, lora, gf = w_in[:, :G_OFF], w_in[:, G_OFF:SHIFT_W], w_in[:, SHIFT_W:]
    w_p = jnp.concatenate([rkv, gf, lora, jnp.zeros((d, LANES), w_in.dtype)], axis=1).astype(BF16)
    mu = jnp.concatenate([shift_mu[:G_OFF], jnp.zeros((2 * RWKV_W,), F32), shift_mu[G_OFF:],
                          jnp.zeros((LANES,), F32)])
    return w_p, mu.reshape(1, IN_PAD_W)


def _layer(x, mod, rows_per_mod, mod_base, s0, grid_shift, p):
    b, t, d = x.shape
    x2d = x.reshape(b * t, d)
    proj = in_proj(x2d, mod, p["norm1_g"], p["w_in_p"], rows_per_mod, mod_base)
    proj = proj.reshape(b, t, IN_PAD_W)
    r, v, a, bonus, w, kd, bd = pre_scan(proj, p["mu"], p["up_w"], p["up_b"],
                                         p["k_k"], p["k_a"], p["r_k"], grid_shift)
    y, s_fin = wkv_scan([_to_scan_layout(w[0], w[1]), _to_scan_layout(kd[0], kd[1]),
                         _to_scan_layout(bd[0], bd[1]), _to_scan_layout(a, a),
                         _to_scan_layout(r, r), _to_scan_layout(v, v)], s0)
    yr = post_scan(_from_scan_layout(y, b), bonus, proj, p["gn_w"], p["gn_b"])
    yf = fnet_mix(proj)
    x1 = out_proj(yr.reshape(b * t, RWKV_W), yf.reshape(b * t, RWKV_W), p["w_out_bf"],
                  x2d, mod, rows_per_mod, mod_base)
    h2, aff = norm2_router(x1, mod, p["norm2_g"], p["router_w"], rows_per_mod, mod_base)
    n = b * t
    cap = n * CAPACITY_FACTOR // N_EXPERTS
    gate, idx = lax.top_k(aff.T, cap)
    ye = expert_ffn(h2[idx], p["exp_w1"], p["exp_w3"], p["exp_w2"], gate)
    ffn = jnp.zeros((n, d), F32).at[idx.reshape(-1)].add(ye.reshape(-1, d))
    out = final_norm(x1, ffn, mod, p["final_norm_g"], rows_per_mod, mod_base)
    return out.reshape(b, t, d), s_fin


def kernel(x_prompt, x_sample, state_wkv, c, c_ctx, ada_w, ada_b, norm1_g, norm2_g, w_in, shift_mu,
           w_decay_up, w_decay0, w_iclr_up, w_iclr0, k_k, k_a, r_k, gn_w, gn_b, w_out, router_w,
           exp_w1, exp_w3, exp_w2, final_norm_g):
    depth = ada_w.shape[0]
    assert depth == 1, "single-layer trunk"
    l = 0
    bc, tc_len, d = x_prompt.shape
    bl, tl, _ = x_sample.shape
    cvecs = jnp.concatenate([c_ctx[None], c], axis=0)
    mod = ada_mod(cvecs, ada_w[l], ada_b[l]).reshape(1 + bl, N_MOD, d)
    up_w, up_b = _lora_up_tables(w_decay_up[l], w_decay0[l], w_iclr_up[l], w_iclr0[l])
    w_in_p, mu = _prepare_in_proj(w_in[l], shift_mu[l])
    p = {
        "norm1_g": norm1_g[l], "norm2_g": norm2_g[l], "final_norm_g": final_norm_g,
        "w_in_p": w_in_p, "mu": mu, "up_w": up_w, "up_b": up_b,
        "k_k": k_k[l], "k_a": k_a[l], "r_k": r_k[l], "gn_w": gn_w[l], "gn_b": gn_b[l],
        "w_out_bf": w_out[l].astype(BF16), "router_w": router_w[l],
        "exp_w1": exp_w1[l], "exp_w3": exp_w3[l], "exp_w2": exp_w2[l],
    }
    groups_c = 2 * bc * N_HEADS // LANES
    s0_ctx = jnp.zeros((groups_c, HEAD_N, HEAD_N, LANES), F32)
    y_prompt, s_ctx = _layer(x_prompt, mod, bc * tc_len, 0, s0_ctx, False, p)
    s_ctx = s_ctx.transpose(1, 2, 0, 3).reshape(HEAD_N, HEAD_N, 2, bc, N_HEADS)
    new_state = s_ctx.transpose(3, 2, 4, 1, 0)[:, None]
    groups_l = 2 * bl * N_HEADS // LANES
    s0_lat = state_wkv[:, l].astype(F32).transpose(4, 3, 1, 0, 2)
    s0_lat = s0_lat.reshape(HEAD_N, HEAD_N, groups_l, LANES).transpose(2, 0, 1, 3)
    y_sample, _ = _layer(x_sample, mod, tl, 1, s0_lat, True, p)
    return (y_prompt, y_sample, new_state)
```

```python
import functools
import math

import jax
import jax.numpy as jnp
from jax import lax
from jax.experimental import pallas as pl
from jax.experimental.pallas import tpu as pltpu

F32 = jnp.float32
BF16 = jnp.bfloat16
HIGHEST = lax.Precision.HIGHEST

D_MODEL = 2048
RWKV_W = 1024
HEAD_N = 64
N_HEADS = 16
LORA_R = 96
N_EXPERTS = 16
EXPERT_FF = 5504
CAPACITY_FACTOR = 2
N_MOD = 6
NORM_EPS = 1e-6
GN_EPS = 64e-5
FOURIER_GW = 256
N_FOURIER_GROUPS = 4

LANES = 128
HEAD_TILES = RWKV_W // LANES
SHIFT_W = 3 * RWKV_W + 4 * LORA_R
LORA_W = 4 * LORA_R
LORA_BLK = LORA_W + LANES
G_OFF = 3 * RWKV_W
F_OFF = G_OFF + RWKV_W
LORA_OFF = F_OFF + RWKV_W
IN_PAD_W = LORA_OFF + LORA_BLK
ROW_CHUNK = 256
VMEM_LIMIT = 56 * 1024 * 1024
DECAY_SCALE = math.exp(-0.5)


def _params(*sem):
    return pltpu.CompilerParams(dimension_semantics=sem, vmem_limit_bytes=VMEM_LIMIT)


def _seg_ones():
    r = lax.broadcasted_iota(jnp.int32, (LANES, LANES), 0) // HEAD_N
    c = lax.broadcasted_iota(jnp.int32, (LANES, LANES), 1) // HEAD_N
    return (r == c).astype(BF16)


def _split2(x):
    hi = x.astype(BF16)
    return hi, (x - hi.astype(F32)).astype(BF16)


def _split3(x):
    hi = x.astype(BF16)
    rem = x - hi.astype(F32)
    mid = rem.astype(BF16)
    return hi, mid, (rem - mid.astype(F32)).astype(BF16)


def _seg_sum(x, seg):
    return sum(jnp.dot(p, seg, preferred_element_type=F32) for p in _split3(x))


def _ada_kernel(c_ref, w_ref, b_ref, o_ref):
    c = c_ref[...]
    s = c * jax.nn.sigmoid(c)
    o_ref[...] = jnp.dot(s.astype(BF16), w_ref[...].astype(BF16),
                         preferred_element_type=F32) + b_ref[...]


def ada_mod(cvecs, ada_w, ada_b):
    rows, d = cvecs.shape
    n = ada_w.shape[1]
    tn = 1024
    return pl.pallas_call(
        _ada_kernel,
        grid=(n // tn,),
        in_specs=[pl.BlockSpec((rows, d), lambda j: (0, 0)),
                  pl.BlockSpec((d, tn), lambda j: (0, j)),
                  pl.BlockSpec((1, tn), lambda j: (0, j))],
        out_specs=pl.BlockSpec((rows, tn), lambda j: (0, j)),
        out_shape=jax.ShapeDtypeStruct((rows, n), F32),
        compiler_params=_params("arbitrary"),
        name="ada_mod",
    )(cvecs, ada_w, ada_b.reshape(1, n))


def _mod_row_map(tm, rows_per_mod, mod_base):
    return lambda i: mod_base + (i * tm) // rows_per_mod


def _rms_modulate(x, g, scale, shift):
    ms = jnp.mean(x * x, axis=-1, keepdims=True)
    return x * lax.rsqrt(ms + NORM_EPS) * g * (1.0 + scale) + shift


def _inproj_kernel(x_ref, mod_ref, g_ref, w_ref, o_ref, h_ref, *, tm):
    @pl.when(pl.program_id(1) == 0)
    def _():
        def body(c, carry):
            rows = pl.ds(pl.multiple_of(c * ROW_CHUNK, ROW_CHUNK), ROW_CHUNK)
            h = _rms_modulate(x_ref[rows, :], g_ref[...], mod_ref[0, 1:2, :], mod_ref[0, 0:1, :])
            h_ref[rows, :] = h.astype(BF16)
            return carry
        lax.fori_loop(0, tm // ROW_CHUNK, body, 0)

    o_ref[...] = jnp.dot(h_ref[...], w_ref[...], preferred_element_type=F32)


def in_proj(x2d, mod, norm_g, w_p, rows_per_mod, mod_base):
    m, d = x2d.shape
    n = w_p.shape[1]
    tm, tn = min(1024, rows_per_mod), 512
    row_of = _mod_row_map(tm, rows_per_mod, mod_base)
    return pl.pallas_call(
        functools.partial(_inproj_kernel, tm=tm),
        grid=(m // tm, n // tn),
        in_specs=[pl.BlockSpec((tm, d), lambda i, j: (i, 0)),
                  pl.BlockSpec((1, N_MOD, d), lambda i, j: (row_of(i), 0, 0)),
                  pl.BlockSpec((1, d), lambda i, j: (0, 0)),
                  pl.BlockSpec((d, tn), lambda i, j: (0, j))],
        out_specs=pl.BlockSpec((tm, tn), lambda i, j: (i, j)),
        out_shape=jax.ShapeDtypeStruct((m, n), F32),
        scratch_shapes=[pltpu.VMEM((tm, d), BF16)],
        compiler_params=_params("arbitrary", "arbitrary"),
        name="in_proj",
    )(x2d, mod, norm_g.reshape(1, d), w_p)


def _shifted(ref, mu, c, n_chunks, t_len, grid_shift):
    ch = ROW_CHUNK
    base = pl.multiple_of(c * ch, ch)
    cur = ref[0, pl.ds(base, ch), :]
    row = lax.broadcasted_iota(jnp.int32, (ch, 1), 0)
    before = pltpu.roll(cur, 1, 0)
    after = pltpu.roll(cur, ch - 1, 0)
    if not grid_shift:
        prev = jnp.where(row == 0, 0.0, before)
        nxt = jnp.where(row == ch - 1, 0.0, after)
        mixed = 0.5 * (prev + nxt)
    else:
        gw = 64
        left = jnp.where(row % gw == 0, 0.0, before)
        right = jnp.where(row % gw == gw - 1, 0.0, after)
        up_start = pl.multiple_of(jnp.maximum(base - gw, 0), gw)
        dn_start = pl.multiple_of(jnp.minimum(base + ch, t_len - gw), gw)
        up_halo = jnp.where(c > 0, ref[0, pl.ds(up_start, gw), :], 0.0)
        dn_halo = jnp.where(c < n_chunks - 1, ref[0, pl.ds(dn_start, gw), :], 0.0)
        up = jnp.concatenate([up_halo, cur[: ch - gw]], axis=0)
        down = jnp.concatenate([cur[gw:], dn_halo], axis=0)
        mixed = 0.25 * (up + down + left + right)
    return cur + mu * (mixed - cur)


def _prescan_kernel(r_ref, k_ref, v_ref, lora_ref, mur_ref, muk_ref, muv_ref, mul_ref,
                    uph_ref, upl_ref, bias_ref, kk_ref, ka_ref, rk_ref,
                    r_o, v_o, a_o, bonus_o, w_o, kd_o, bd_o, *, t_len, grid_shift):
    n_chunks = t_len // ROW_CHUNK
    seg = _seg_ones()
    lane = lax.broadcasted_iota(jnp.int32, (1, LORA_BLK), 1)

    def body(c, carry):
        sh = functools.partial(_shifted, c=c, n_chunks=n_chunks, t_len=t_len, grid_shift=grid_shift)
        r = sh(r_ref, mur_ref[...])
        k = sh(k_ref, muk_ref[...])
        v = sh(v_ref, muv_ref[...])
        lora = sh(lora_ref, mul_ref[...])
        lora = jnp.where(lane < 2 * LORA_R, jnp.tanh(lora), lora)
        l_hi, l_lo = _split2(lora)
        raw = (jnp.dot(l_hi, uph_ref[0], preferred_element_type=F32)
               + jnp.dot(l_hi, upl_ref[0], preferred_element_type=F32)
               + jnp.dot(l_lo, uph_ref[0], preferred_element_type=F32)) + bias_ref[0]
        kk = k * kk_ref[...]
        kk = kk * lax.rsqrt(_seg_sum(kk * kk, seg) + 1e-12)
        rows = pl.ds(pl.multiple_of(c * ROW_CHUNK, ROW_CHUNK), ROW_CHUNK)
        kd_sum = jnp.zeros_like(k)
        for d in range(2):
            decay = jnp.exp(-DECAY_SCALE * jax.nn.sigmoid(raw[:, d * LANES:(d + 1) * LANES]))
            iclr = jax.nn.sigmoid(raw[:, (2 + d) * LANES:(3 + d) * LANES])
            kd = k * (1.0 + (iclr - 1.0) * ka_ref[...])
            w_o[d, 0, rows, :] = decay
            kd_o[d, 0, rows, :] = kd
            bd_o[d, 0, rows, :] = kk * iclr
            kd_sum = kd_sum + kd
        r_o[0, rows, :] = r
        v_o[0, rows, :] = v
        a_o[0, rows, :] = -kk
        bonus_o[0, rows, :] = _seg_sum(r * (0.5 * kd_sum) * rk_ref[...], seg) * v
        return carry

    lax.fori_loop(0, n_chunks, body, 0)


def pre_scan(proj, mu, up_w, up_b, k_k, k_a, r_k, grid_shift):
    b, t, _ = proj.shape
    up_hi, up_lo = _split2(up_w)
    if not grid_shift:
        assert t == ROW_CHUNK, "sequence shift handles one chunk per sequence"
    assert t % ROW_CHUNK == 0
    col = lambda off: pl.BlockSpec((1, t, LANES), lambda i, j: (i, 0, off + j))
    vec = lambda off: pl.BlockSpec((1, LANES), lambda i, j: (0, off + j))
    out1 = pl.BlockSpec((1, t, LANES), lambda i, j: (i, 0, j))
    out2 = pl.BlockSpec((2, 1, t, LANES), lambda i, j: (0, i, 0, j))
    s1 = jax.ShapeDtypeStruct((b, t, RWKV_W), F32)
    s2 = jax.ShapeDtypeStruct((2, b, t, RWKV_W), F32)
    return pl.pallas_call(
        functools.partial(_prescan_kernel, t_len=t, grid_shift=grid_shift),
        grid=(b, HEAD_TILES),
        in_specs=[col(0), col(HEAD_TILES), col(2 * HEAD_TILES),
                  pl.BlockSpec((1, t, LORA_BLK), lambda i, j: (i, 0, LORA_OFF // LORA_BLK)),
                  vec(0), vec(HEAD_TILES), vec(2 * HEAD_TILES),
                  pl.BlockSpec((1, LORA_BLK), lambda i, j: (0, LORA_OFF // LORA_BLK)),
                  pl.BlockSpec((1, LORA_BLK, 4 * LANES), lambda i, j: (j, 0, 0)),
                  pl.BlockSpec((1, LORA_BLK, 4 * LANES), lambda i, j: (j, 0, 0)),
                  pl.BlockSpec((1, 1, 4 * LANES), lambda i, j: (j, 0, 0)),
                  vec(0), vec(0), vec(0)],
        out_specs=[out1, out1, out1, out1, out2, out2, out2],
        out_shape=[s1, s1, s1, s1, s2, s2, s2],
        compiler_params=_params("arbitrary", "arbitrary"),
        name="pre_scan",
    )(proj, proj, proj, proj, mu, mu, mu, mu, up_hi, up_lo, up_b,
      k_k.reshape(1, RWKV_W), k_a.reshape(1, RWKV_W), r_k.reshape(1, RWKV_W))


def _lora_up_tables(w_decay_up, w_decay0, w_iclr_up, w_iclr0):
    mats = [w_decay_up[0], w_decay_up[1], w_iclr_up[0], w_iclr_up[1]]
    bias = [w_decay0[0], w_decay0[1], w_iclr0[0], w_iclr0[1]]
    up = jnp.zeros((HEAD_TILES, LORA_BLK, 4 * LANES), F32)
    for q, m in enumerate(mats):
        blk = m.astype(F32).reshape(LORA_R, HEAD_TILES, LANES).transpose(1, 0, 2)
        up = up.at[:, q * LORA_R:(q + 1) * LORA_R, q * LANES:(q + 1) * LANES].set(blk)
    b = jnp.stack([v.astype(F32).reshape(HEAD_TILES, LANES) for v in bias], axis=1)
    return up, b.reshape(HEAD_TILES, 1, 4 * LANES)


def _scan_kernel(*refs, tc, mixed):
    n_in = 12 if mixed else 6
    ins, rest = refs[:n_in], refs[n_in:]
    if mixed:
        s0_ref, y_ref, yb_ref, sfin_ref, s_scr, wr_scr = rest
        fwd_lane = lax.broadcasted_iota(jnp.int32, (1, LANES), 1) < LANES // 2
        tt = lambda t: t

        def view(i):
            fw, bw = ins[i], ins[n_in // 2 + i]
            full = lambda t: jnp.where(fwd_lane, fw[t], bw[tc - 1 - t])
            row = lambda t, k: jnp.where(fwd_lane, fw[t, k:k + 1, :], bw[tc - 1 - t, k:k + 1, :])
            return full, row
    else:
        s0_ref, y_ref, sfin_ref, s_scr, wr_scr = rest
        backward = pl.program_id(0) >= pl.num_programs(0) // 2
        tt = lambda t: jnp.where(backward, tc - 1 - t, t)

        def view(i):
            ref = ins[i]
            return (lambda t: ref[tt(t)]), (lambda t, k: ref[tt(t), k:k + 1, :])

    (w_full, w_row), (k_full, k_row), (b_full, b_row), (_, a_row), (r_full, _), (v_full, _) = (
        view(i) for i in range(6))
    c = pl.program_id(1)

    @pl.when(c == 0)
    def _():
        s_scr[...] = s0_ref[0]

    wr_scr[0] = w_full(0) * r_full(0)
    sa = jnp.zeros((HEAD_N, LANES), F32)
    y0 = jnp.zeros((HEAD_N, LANES), F32)
    for k in range(HEAD_N):
        s = s_scr[k]
        sa = sa + s * a_row(0, k)
        y0 = y0 + s * wr_scr[0, k:k + 1, :]

    def step(t, carry):
        sa, y0 = carry
        r_t = r_full(t)
        v_t = v_full(t)
        br = jnp.sum(b_full(t) * r_t, axis=0, keepdims=True)
        kr = jnp.sum(k_full(t) * r_t, axis=0, keepdims=True)
        y = y0 + sa * br + v_t * kr
        y_ref[tt(t)] = y
        if mixed:
            yb_ref[tc - 1 - t] = y
        tn = jnp.minimum(t + 1, tc - 1)
        slot = (t + 1) % 2
        wr_scr[slot] = w_full(tn) * r_full(tn)
        sa_n = jnp.zeros((HEAD_N, LANES), F32)
        y0_n = jnp.zeros((HEAD_N, LANES), F32)
        for k in range(HEAD_N):
            s = s_scr[k] * w_row(t, k) + sa * b_row(t, k) + v_t * k_row(t, k)
            s_scr[k] = s
            sa_n = sa_n + s * a_row(tn, k)
            y0_n = y0_n + s * wr_scr[slot, k:k + 1, :]
        return sa_n, y0_n

    lax.fori_loop(0, tc, step, (sa, y0))

    @pl.when(c == pl.num_programs(1) - 1)
    def _():
        sfin_ref[0] = s_scr[...]


def wkv_scan(ops, s0, tc=32):
    t, _, lanes = ops[0].shape
    groups = lanes // LANES
    nc = t // tc
    mixed = groups == 1
    st = pl.BlockSpec((1, HEAD_N, HEAD_N, LANES), lambda g, c: (g, 0, 0, 0))
    y_shape = jax.ShapeDtypeStruct((t, HEAD_N, lanes), F32)
    if mixed:
        fwd = pl.BlockSpec((tc, HEAD_N, LANES), lambda g, c: (c, 0, g))
        bwd = pl.BlockSpec((tc, HEAD_N, LANES), lambda g, c: (nc - 1 - c, 0, g))
        in_specs, operands = [fwd] * 6 + [bwd] * 6 + [st], list(ops) + list(ops) + [s0]
        out_specs, out_shape = [fwd, bwd, st], [y_shape, y_shape]
    else:
        seq = pl.BlockSpec((tc, HEAD_N, LANES),
                           lambda g, c: (jnp.where(g >= groups // 2, nc - 1 - c, c), 0, g))
        in_specs, operands = [seq] * 6 + [st], list(ops) + [s0]
        out_specs, out_shape = [seq, st], [y_shape]
    out_shape.append(jax.ShapeDtypeStruct((groups, HEAD_N, HEAD_N, LANES), F32))
    res = pl.pallas_call(
        functools.partial(_scan_kernel, tc=tc, mixed=mixed),
        grid=(groups, nc),
        in_specs=in_specs,
        out_specs=out_specs,
        out_shape=out_shape,
        scratch_shapes=[pltpu.VMEM((HEAD_N, HEAD_N, LANES), F32),
                        pltpu.VMEM((2, HEAD_N, LANES), F32)],
        compiler_params=_params("arbitrary", "arbitrary"),
        name="wkv_scan",
    )(*operands)
    half = lanes // 2
    if mixed:
        y_f, y_b, s_fin = res
        return y_f[..., :half] + y_b[..., half:], s_fin
    y, s_fin = res
    return y[..., :half] + y[..., half:], s_fin


def _to_scan_layout(x_fwd, x_bwd):
    b, t, _ = x_fwd.shape
    xs = jnp.stack([x_fwd, x_bwd])
    xs = xs.reshape(2, b, t, N_HEADS, HEAD_N).transpose(2, 4, 0, 1, 3)
    return xs.reshape(t, HEAD_N, 2 * b * N_HEADS)


def _from_scan_layout(y, b):
    t = y.shape[0]
    return y.reshape(t, HEAD_N, b, N_HEADS).transpose(2, 0, 3, 1).reshape(b, t, RWKV_W)


def _postscan_kernel(y_ref, bonus_ref, g_ref, gw_ref, gb_ref, o_ref):
    seg = _seg_ones()
    for j in range(HEAD_TILES):
        cols = slice(j * LANES, (j + 1) * LANES)
        y = y_ref[0, :, cols]
        mu = _seg_sum(y, seg) * (1.0 / HEAD_N)
        d = y - mu
        var = _seg_sum(d * d, seg) * (1.0 / HEAD_N)
        yn = d * lax.rsqrt(var + GN_EPS) * gw_ref[:, cols] + gb_ref[:, cols]
        o_ref[0, :, cols] = ((yn + bonus_ref[0, :, cols]) * jax.nn.sigmoid(g_ref[0, :, cols])).astype(BF16)


def post_scan(y, bonus, proj, gn_w, gn_b):
    b, t, _ = y.shape
    ch = ROW_CHUNK
    blk = pl.BlockSpec((1, ch, RWKV_W), lambda i, c: (i, c, 0))
    vec = pl.BlockSpec((1, RWKV_W), lambda i, c: (0, 0))
    return pl.pallas_call(
        _postscan_kernel,
        grid=(b, t // ch),
        in_specs=[blk, blk,
                  pl.BlockSpec((1, ch, RWKV_W), lambda i, c: (i, c, G_OFF // RWKV_W)),
                  vec, vec],
        out_specs=blk,
        out_shape=jax.ShapeDtypeStruct((b, t, RWKV_W), BF16),
        compiler_params=_params("arbitrary", "arbitrary"),
        name="post_scan",
    )(y, bonus, proj, gn_w.reshape(1, RWKV_W), gn_b.reshape(1, RWKV_W))


def _fnet_kernel(f_ref, cc_ref, ct_ref, o_ref, xcs_ref, *, t_len, scale):
    @pl.when(pl.program_id(2) == 0)
    def _():
        xc = jnp.dot(f_ref[0].astype(BF16), cc_ref[...], preferred_element_type=F32)
        xcs_ref[0:t_len, :] = xc[:, :FOURIER_GW].astype(BF16)
        xcs_ref[t_len:2 * t_len, :] = xc[:, FOURIER_GW:].astype(BF16)

    out = jnp.dot(ct_ref[...], xcs_ref[...], preferred_element_type=F32)
    o_ref[0] = (out * scale).astype(BF16)


def _dft_tables(n):
    j = lax.broadcasted_iota(jnp.int32, (n, n), 0)
    k = lax.broadcasted_iota(jnp.int32, (n, n), 1)
    ang = ((j * k) % n).astype(F32) * (2.0 * math.pi / n)
    return jnp.cos(ang), jnp.sin(ang)


def fnet_mix(proj):
    b, t, _ = proj.shape
    cc, sc = _dft_tables(FOURIER_GW)
    ct, st = _dft_tables(t)
    cc2 = jnp.concatenate([cc, sc], axis=1).astype(BF16)
    ct2 = jnp.concatenate([ct, -st], axis=1).astype(BF16)
    tq = min(512, t)
    return pl.pallas_call(
        functools.partial(_fnet_kernel, t_len=t, scale=1.0 / math.sqrt(t * FOURIER_GW)),
        grid=(b, N_FOURIER_GROUPS, t // tq),
        in_specs=[pl.BlockSpec((1, t, FOURIER_GW), lambda i, g, q: (i, 0, F_OFF // FOURIER_GW + g)),
                  pl.BlockSpec((FOURIER_GW, 2 * FOURIER_GW), lambda i, g, q: (0, 0)),
                  pl.BlockSpec((tq, 2 * t), lambda i, g, q: (q, 0))],
        out_specs=pl.BlockSpec((1, tq, FOURIER_GW), lambda i, g, q: (i, q, g)),
        out_shape=jax.ShapeDtypeStruct((b, t, RWKV_W), BF16),
        scratch_shapes=[pltpu.VMEM((2 * t, FOURIER_GW), BF16)],
        compiler_params=_params("arbitrary", "arbitrary", "arbitrary"),
        name="fnet_mix",
    )(proj, cc2, ct2)


def _outproj_kernel(yr_ref, yf_ref, w_ref, x_ref, mod_ref, o_ref):
    m = jnp.dot(yr_ref[...], w_ref[0:RWKV_W, :], preferred_element_type=F32)
    m = m + jnp.dot(yf_ref[...], w_ref[RWKV_W:2 * RWKV_W, :], preferred_element_type=F32)
    o_ref[...] = x_ref[...] + mod_ref[0, 2:3, :] * m


def out_proj(yr, yf, w_bf, x2d, mod, rows_per_mod, mod_base):
    m, d = x2d.shape
    tm, tn = min(1024, rows_per_mod), 512
    row_of = _mod_row_map(tm, rows_per_mod, mod_base)
    return pl.pallas_call(
        _outproj_kernel,
        grid=(m // tm, d // tn),
        in_specs=[pl.BlockSpec((tm, RWKV_W), lambda i, j: (i, 0)),
                  pl.BlockSpec((tm, RWKV_W), lambda i, j: (i, 0)),
                  pl.BlockSpec((2 * RWKV_W, tn), lambda i, j: (0, j)),
                  pl.BlockSpec((tm, tn), lambda i, j: (i, j)),
                  pl.BlockSpec((1, N_MOD, tn), lambda i, j: (row_of(i), 0, j))],
        out_specs=pl.BlockSpec((tm, tn), lambda i, j: (i, j)),
        out_shape=jax.ShapeDtypeStruct((m, d), F32),
        compiler_params=_params("arbitrary", "arbitrary"),
        name="out_proj",
    )(yr, yf, w_bf, x2d, mod)


def _norm2_router_kernel(x_ref, mod_ref, g_ref, rw_ref, h_ref, aff_ref):
    h = _rms_modulate(x_ref[...], g_ref[...], mod_ref[0, 4:5, :], mod_ref[0, 3:4, :])
    h_ref[...] = h
    logits = jnp.dot(h, rw_ref[...], precision=HIGHEST, preferred_element_type=F32)
    z = logits - jnp.max(logits, axis=-1, keepdims=True)
    e = jnp.exp(z)
    aff_ref[...] = e / jnp.sum(e, axis=-1, keepdims=True)


def norm2_router(x2d, mod, norm_g, router_w, rows_per_mod, mod_base):
    m, d = x2d.shape
    tm = ROW_CHUNK
    row_of = _mod_row_map(tm, rows_per_mod, mod_base)
    return pl.pallas_call(
        _norm2_router_kernel,
        grid=(m // tm,),
        in_specs=[pl.BlockSpec((tm, d), lambda i: (i, 0)),
                  pl.BlockSpec((1, N_MOD, d), lambda i: (row_of(i), 0, 0)),
                  pl.BlockSpec((1, d), lambda i: (0, 0)),
                  pl.BlockSpec((d, N_EXPERTS), lambda i: (0, 0))],
        out_specs=[pl.BlockSpec((tm, d), lambda i: (i, 0)),
                   pl.BlockSpec((tm, N_EXPERTS), lambda i: (i, 0))],
        out_shape=[jax.ShapeDtypeStruct((m, d), F32),
                   jax.ShapeDtypeStruct((m, N_EXPERTS), F32)],
        compiler_params=_params("arbitrary"),
        name="norm2_router",
    )(x2d, mod, norm_g.reshape(1, d), router_w)


GATHER_ROWS = 256
PAIR_CHUNK = LANES


def _issue_row_gather(idx_ref, base, n_rows, src_hbm, dst, sem):
    def body(r, carry):
        row = idx_ref[base + r]
        pltpu.make_async_copy(src_hbm.at[pl.ds(row, 1), :], dst.at[pl.ds(r, 1), :], sem).start()
        return carry
    lax.fori_loop(0, n_rows, body, 0, unroll=8)


def _wait_row_gather(n_rows, src_hbm, dst, sem):
    pltpu.make_async_copy(src_hbm.at[pl.ds(0, n_rows), :], dst, sem).wait()


def _gather_kernel(idx_ref, h_hbm, o_ref, buf, sem):
    s = pl.program_id(0)
    slot = s % 2

    @pl.when(s == 0)
    def _():
        _issue_row_gather(idx_ref, 0, GATHER_ROWS, h_hbm, buf.at[0], sem.at[0])

    @pl.when(s + 1 < pl.num_programs(0))
    def _():
        _issue_row_gather(idx_ref, (s + 1) * GATHER_ROWS, GATHER_ROWS, h_hbm,
                          buf.at[1 - slot], sem.at[1 - slot])

    _wait_row_gather(GATHER_ROWS, h_hbm, buf.at[slot], sem.at[slot])
    o_ref[...] = buf[slot].astype(BF16)


def gather_tokens(h, idx_flat):
    n, d = h.shape
    rows = idx_flat.shape[0]
    return pl.pallas_call(
        _gather_kernel,
        grid_spec=pltpu.PrefetchScalarGridSpec(
            num_scalar_prefetch=1,
            grid=(rows // GATHER_ROWS,),
            in_specs=[pl.BlockSpec(memory_space=pl.ANY)],
            out_specs=pl.BlockSpec((GATHER_ROWS, d), lambda s, idx: (s, 0)),
            scratch_shapes=[pltpu.VMEM((2, GATHER_ROWS, d), F32), pltpu.SemaphoreType.DMA((2,))]),
        out_shape=jax.ShapeDtypeStruct((rows, d), BF16),
        compiler_params=_params("arbitrary"),
        name="gather_tokens",
    )(idx_flat, h)


def _expert_kernel(xe_ref, w1_ref, w3_ref, w2_ref, gate_ref, o_ref, *, tf, ff):
    f = pl.program_id(1)
    x = xe_ref[0]
    h1 = jnp.dot(x, w1_ref[0].astype(BF16), preferred_element_type=F32)
    h3 = jnp.dot(x, w3_ref[0].astype(BF16), preferred_element_type=F32)
    hid = h1 * jax.nn.sigmoid(h1) * h3
    valid = ff - f * tf
    hid = jnp.where(lax.broadcasted_iota(jnp.int32, (1, tf), 1) < valid, hid, 0.0)
    w2 = jnp.where(lax.broadcasted_iota(jnp.int32, (tf, 1), 0) < valid, w2_ref[0], 0.0)
    @pl.when(f == 0)
    def _():
        o_ref[...] = jnp.zeros_like(o_ref)

    o_ref[0] += jnp.dot(hid.astype(BF16), w2.astype(BF16), preferred_element_type=F32)

    @pl.when(f == pl.num_programs(1) - 1)
    def _():
        o_ref[0] = o_ref[0] * gate_ref[0]


def expert_ffn(xe, w1, w3, w2, gate):
    e, cap, d = xe.shape
    ff = w1.shape[2]
    tf = 512
    nf = pl.cdiv(ff, tf)
    return pl.pallas_call(
        functools.partial(_expert_kernel, tf=tf, ff=ff),
        grid=(e, nf),
        in_specs=[pl.BlockSpec((1, cap, d), lambda i, f: (i, 0, 0), pipeline_mode=pl.Buffered(1)),
                  pl.BlockSpec((1, d, tf), lambda i, f: (i, 0, f)),
                  pl.BlockSpec((1, d, tf), lambda i, f: (i, 0, f)),
                  pl.BlockSpec((1, tf, d), lambda i, f: (i, f, 0)),
                  pl.BlockSpec((1, cap, 1), lambda i, f: (i, 0, 0))],
        out_specs=pl.BlockSpec((1, cap, d), lambda i, f: (i, 0, 0)),
        out_shape=jax.ShapeDtypeStruct((e, cap, d), F32),
        compiler_params=_params("arbitrary", "arbitrary"),
        name="expert_ffn",
    )(xe, w1, w3, w2, gate.reshape(e, cap, 1))


def _combine_final_kernel(rows_ref, ptr_ref, tok_ref, ye_hbm, x_ref, mod_ref, g_ref, o_ref,
                          buf, sem, acc_ref, *, tm):
    i = pl.program_id(0)
    p0 = ptr_ref[i]
    p1 = ptr_ref[i + 1]
    c0 = p0 // PAIR_CHUNK
    c1 = jnp.where(p1 > p0, (p1 - 1) // PAIR_CHUNK + 1, c0)
    acc_ref[...] = jnp.zeros_like(acc_ref)

    @pl.when(c1 > c0)
    def _():
        _issue_row_gather(rows_ref, c0 * PAIR_CHUNK, PAIR_CHUNK, ye_hbm, buf.at[0], sem.at[0])

    def chunk(c, carry):
        slot = (c - c0) % 2

        @pl.when(c + 1 < c1)
        def _():
            _issue_row_gather(rows_ref, (c + 1) * PAIR_CHUNK, PAIR_CHUNK, ye_hbm,
                              buf.at[1 - slot], sem.at[1 - slot])

        _wait_row_gather(PAIR_CHUNK, ye_hbm, buf.at[slot], sem.at[slot])
        local = tok_ref[pl.ds(c, 1), :] - i * tm
        onehot = (lax.broadcasted_iota(jnp.int32, (tm, PAIR_CHUNK), 0) == local).astype(BF16)
        hi, lo = _split2(buf[slot])
        acc_ref[...] += (jnp.dot(onehot, hi, preferred_element_type=F32)
                         + jnp.dot(onehot, lo, preferred_element_type=F32))
        return carry

    lax.fori_loop(c0, c1, chunk, 0)
    x = x_ref[...] + mod_ref[0, 5:6, :] * acc_ref[...]
    ms = jnp.mean(x * x, axis=-1, keepdims=True)
    o_ref[...] = x * lax.rsqrt(ms + NORM_EPS) * g_ref[...]


def combine_final(x2d, ye_flat, pair_rows, pair_toks, tile_ptr, mod, final_g, rows_per_mod, mod_base):
    m, d = x2d.shape
    tm = ROW_CHUNK
    n_pairs = pair_rows.shape[0]
    row_of = _mod_row_map(tm, rows_per_mod, mod_base)
    return pl.pallas_call(
        functools.partial(_combine_final_kernel, tm=tm),
        grid_spec=pltpu.PrefetchScalarGridSpec(
            num_scalar_prefetch=2,
            grid=(m // tm,),
            in_specs=[pl.BlockSpec((n_pairs // PAIR_CHUNK, PAIR_CHUNK), lambda i, r, p: (0, 0)),
                      pl.BlockSpec(memory_space=pl.ANY),
                      pl.BlockSpec((tm, d), lambda i, r, p: (i, 0)),
                      pl.BlockSpec((1, N_MOD, d), lambda i, r, p: (row_of(i), 0, 0)),
                      pl.BlockSpec((1, d), lambda i, r, p: (0, 0))],
            out_specs=pl.BlockSpec((tm, d), lambda i, r, p: (i, 0)),
            scratch_shapes=[pltpu.VMEM((2, PAIR_CHUNK, d), F32), pltpu.SemaphoreType.DMA((2,)),
                            pltpu.VMEM((tm, d), F32)]),
        out_shape=jax.ShapeDtypeStruct((m, d), F32),
        compiler_params=_params("arbitrary"),
        name="combine_final",
    )(pair_rows, tile_ptr, pair_toks.reshape(n_pairs // PAIR_CHUNK, PAIR_CHUNK), ye_flat,
      x2d, mod, final_g.reshape(1, d))


def _prepare_in_proj(w_in, shift_mu):
    d = w_in.shape[0]
    rkv, lora, gf = w_in[:, :G_OFF], w_in[:, G_OFF:SHIFT_W], w_in[:, SHIFT_W:]
    w_p = jnp.concatenate([rkv, gf, lora, jnp.zeros((d, LANES), w_in.dtype)], axis=1).astype(BF16)
    mu = jnp.concatenate([shift_mu[:G_OFF], jnp.zeros((2 * RWKV_W,), F32), shift_mu[G_OFF:],
                          jnp.zeros((LANES,), F32)])
    return w_p, mu.reshape(1, IN_PAD_W)


def _layer(x, mod, rows_per_mod, mod_base, s0, grid_shift, p):
    b, t, d = x.shape
    x2d = x.reshape(b * t, d)
    proj = in_proj(x2d, mod, p["norm1_g"], p["w_in_p"], rows_per_mod, mod_base)
    proj = proj.reshape(b, t, IN_PAD_W)
    r, v, a, bonus, w, kd, bd = pre_scan(proj, p["mu"], p["up_w"], p["up_b"],
                                         p["k_k"], p["k_a"], p["r_k"], grid_shift)
    y, s_fin = wkv_scan([_to_scan_layout(w[0], w[1]), _to_scan_layout(kd[0], kd[1]),
                         _to_scan_layout(bd[0], bd[1]), _to_scan_layout(a, a),
                         _to_scan_layout(r, r), _to_scan_layout(v, v)], s0)
    yr = post_scan(_from_scan_layout(y, b), bonus, proj, p["gn_w"], p["gn_b"])
    yf = fnet_mix(proj)
    x1 = out_proj(yr.reshape(b * t, RWKV_W), yf.reshape(b * t, RWKV_W), p["w_out_bf"],
                  x2d, mod, rows_per_mod, mod_base)
    h2, aff = norm2_router(x1, mod, p["norm2_g"], p["router_w"], rows_per_mod, mod_base)
    n = b * t
    cap = n * CAPACITY_FACTOR // N_EXPERTS
    gate, idx = lax.top_k(aff.T, cap)
    idx_flat = idx.reshape(-1).astype(jnp.int32)
    xe = gather_tokens(h2, idx_flat).reshape(N_EXPERTS, cap, d)
    ye = expert_ffn(xe, p["exp_w1"], p["exp_w3"], p["exp_w2"], gate)
    pair_toks, pair_rows = lax.sort_key_val(idx_flat, jnp.arange(idx_flat.shape[0], dtype=jnp.int32))
    tile_ptr = jnp.searchsorted(pair_toks, jnp.arange(0, n + 1, ROW_CHUNK, dtype=jnp.int32),
                                side="left").astype(jnp.int32)
    out = combine_final(x1, ye.reshape(-1, d), pair_rows, pair_toks, tile_ptr, mod,
                        p["final_norm_g"], rows_per_mod, mod_base)
    return out.reshape(b, t, d), s_fin


def kernel(x_prompt, x_sample, state_wkv, c, c_ctx, ada_w, ada_b, norm1_g, norm2_g, w_in, shift_mu,
           w_decay_up, w_decay0, w_iclr_up, w_iclr0, k_k, k_a, r_k, gn_w, gn_b, w_out, router_w,
           exp_w1, exp_w3, exp_w2, final_norm_g):
    depth = ada_w.shape[0]
    assert depth == 1, "single-layer trunk"
    l = 0
    bc, tc_len, d = x_prompt.shape
    bl, tl, _ = x_sample.shape
    cvecs = jnp.concatenate([c_ctx[None], c], axis=0)
    mod = ada_mod(cvecs, ada_w[l], ada_b[l]).reshape(1 + bl, N_MOD, d)
    up_w, up_b = _lora_up_tables(w_decay_up[l], w_decay0[l], w_iclr_up[l], w_iclr0[l])
    w_in_p, mu = _prepare_in_proj(w_in[l], shift_mu[l])
    p = {
        "norm1_g": norm1_g[l], "norm2_g": norm2_g[l], "final_norm_g": final_norm_g,
        "w_in_p": w_in_p, "mu": mu, "up_w": up_w, "up_b": up_b,
        "k_k": k_k[l], "k_a": k_a[l], "r_k": r_k[l], "gn_w": gn_w[l], "gn_b": gn_b[l],
        "w_out_bf": w_out[l].astype(BF16), "router_w": router_w[l],
        "exp_w1": exp_w1[l], "exp_w3": exp_w3[l], "exp_w2": exp_w2[l],
    }
    groups_c = 2 * bc * N_HEADS // LANES
    s0_ctx = jnp.zeros((groups_c, HEAD_N, HEAD_N, LANES), F32)
    y_prompt, s_ctx = _layer(x_prompt, mod, bc * tc_len, 0, s0_ctx, False, p)
    s_ctx = s_ctx.transpose(1, 2, 0, 3).reshape(HEAD_N, HEAD_N, 2, bc, N_HEADS)
    new_state = s_ctx.transpose(3, 2, 4, 1, 0)[:, None]
    groups_l = 2 * bl * N_HEADS // LANES
    s0_lat = state_wkv[:, l].astype(F32).transpose(4, 3, 1, 0, 2)
    s0_lat = s0_lat.reshape(HEAD_N, HEAD_N, groups_l, LANES).transpose(2, 0, 1, 3)
    y_sample, _ = _layer(x_sample, mod, tl, 1, s0_lat, True, p)
    return (y_prompt, y_sample, new_state)
```

```python
import functools
import math

import jax
import jax.numpy as jnp
from jax import lax
from jax.experimental import pallas as pl
from jax.experimental.pallas import tpu as pltpu

F32 = jnp.float32
BF16 = jnp.bfloat16
HIGHEST = lax.Precision.HIGHEST

D_MODEL = 2048
RWKV_W = 1024
HEAD_N = 64
N_HEADS = 16
LORA_R = 96
N_EXPERTS = 16
EXPERT_FF = 5504
CAPACITY_FACTOR = 2
N_MOD = 6
NORM_EPS = 1e-6
GN_EPS = 64e-5
FOURIER_GW = 256
N_FOURIER_GROUPS = 4

LANES = 128
HEAD_TILES = RWKV_W // LANES
SHIFT_W = 3 * RWKV_W + 4 * LORA_R
LORA_W = 4 * LORA_R
LORA_BLK = LORA_W + LANES
G_OFF = 3 * RWKV_W
F_OFF = G_OFF + RWKV_W
LORA_OFF = F_OFF + RWKV_W
IN_PAD_W = LORA_OFF + LORA_BLK
ROW_CHUNK = 256
VMEM_LIMIT = 56 * 1024 * 1024
DECAY_SCALE = math.exp(-0.5)


def _params(*sem):
    return pltpu.CompilerParams(dimension_semantics=sem, vmem_limit_bytes=VMEM_LIMIT)


def _seg_ones():
    r = lax.broadcasted_iota(jnp.int32, (LANES, LANES), 0) // HEAD_N
    c = lax.broadcasted_iota(jnp.int32, (LANES, LANES), 1) // HEAD_N
    return (r == c).astype(BF16)


def _split2(x):
    hi = x.astype(BF16)
    return hi, (x - hi.astype(F32)).astype(BF16)


def _split3(x):
    hi = x.astype(BF16)
    rem = x - hi.astype(F32)
    mid = rem.astype(BF16)
    return hi, mid, (rem - mid.astype(F32)).astype(BF16)


def _seg_sum(x, seg):
    return sum(jnp.dot(p, seg, preferred_element_type=F32) for p in _split3(x))


def _ada_kernel(c_ref, w_ref, b_ref, o_ref):
    c = c_ref[...]
    s = c * jax.nn.sigmoid(c)
    o_ref[...] = jnp.dot(s.astype(BF16), w_ref[...].astype(BF16),
                         preferred_element_type=F32) + b_ref[...]


def ada_mod(cvecs, ada_w, ada_b):
    rows, d = cvecs.shape
    n = ada_w.shape[1]
    tn = 1024
    return pl.pallas_call(
        _ada_kernel,
        grid=(n // tn,),
        in_specs=[pl.BlockSpec((rows, d), lambda j: (0, 0)),
                  pl.BlockSpec((d, tn), lambda j: (0, j)),
                  pl.BlockSpec((1, tn), lambda j: (0, j))],
        out_specs=pl.BlockSpec((rows, tn), lambda j: (0, j)),
        out_shape=jax.ShapeDtypeStruct((rows, n), F32),
        compiler_params=_params("arbitrary"),
        name="ada_mod",
    )(cvecs, ada_w, ada_b.reshape(1, n))


def _mod_row_map(tm, rows_per_mod, mod_base):
    return lambda i: mod_base + (i * tm) // rows_per_mod


def _rms_modulate(x, g, scale, shift):
    ms = jnp.mean(x * x, axis=-1, keepdims=True)
    return x * lax.rsqrt(ms + NORM_EPS) * g * (1.0 + scale) + shift


def _inproj_kernel(x_ref, mod_ref, g_ref, w_ref, o_ref, h_ref, *, tm):
    @pl.when(pl.program_id(1) == 0)
    def _():
        def body(c, carry):
            rows = pl.ds(pl.multiple_of(c * ROW_CHUNK, ROW_CHUNK), ROW_CHUNK)
            h = _rms_modulate(x_ref[rows, :], g_ref[...], mod_ref[0, 1:2, :], mod_ref[0, 0:1, :])
            h_ref[rows, :] = h.astype(BF16)
            return carry
        lax.fori_loop(0, tm // ROW_CHUNK, body, 0)

    o_ref[...] = jnp.dot(h_ref[...], w_ref[...], preferred_element_type=F32)


def in_proj(x2d, mod, norm_g, w_p, rows_per_mod, mod_base):
    m, d = x2d.shape
    n = w_p.shape[1]
    tm, tn = min(1024, rows_per_mod), 512
    row_of = _mod_row_map(tm, rows_per_mod, mod_base)
    return pl.pallas_call(
        functools.partial(_inproj_kernel, tm=tm),
        grid=(m // tm, n // tn),
        in_specs=[pl.BlockSpec((tm, d), lambda i, j: (i, 0)),
                  pl.BlockSpec((1, N_MOD, d), lambda i, j: (row_of(i), 0, 0)),
                  pl.BlockSpec((1, d), lambda i, j: (0, 0)),
                  pl.BlockSpec((d, tn), lambda i, j: (0, j))],
        out_specs=pl.BlockSpec((tm, tn), lambda i, j: (i, j)),
        out_shape=jax.ShapeDtypeStruct((m, n), F32),
        scratch_shapes=[pltpu.VMEM((tm, d), BF16)],
        compiler_params=_params("arbitrary", "arbitrary"),
        name="in_proj",
    )(x2d, mod, norm_g.reshape(1, d), w_p)


def _shifted(ref, mu, c, n_chunks, t_len, grid_shift):
    ch = ROW_CHUNK
    base = pl.multiple_of(c * ch, ch)
    cur = ref[0, pl.ds(base, ch), :]
    row = lax.broadcasted_iota(jnp.int32, (ch, 1), 0)
    before = pltpu.roll(cur, 1, 0)
    after = pltpu.roll(cur, ch - 1, 0)
    if not grid_shift:
        prev = jnp.where(row == 0, 0.0, before)
        nxt = jnp.where(row == ch - 1, 0.0, after)
        mixed = 0.5 * (prev + nxt)
    else:
        gw = 64
        left = jnp.where(row % gw == 0, 0.0, before)
        right = jnp.where(row % gw == gw - 1, 0.0, after)
        up_start = pl.multiple_of(jnp.maximum(base - gw, 0), gw)
        dn_start = pl.multiple_of(jnp.minimum(base + ch, t_len - gw), gw)
        up_halo = jnp.where(c > 0, ref[0, pl.ds(up_start, gw), :], 0.0)
        dn_halo = jnp.where(c < n_chunks - 1, ref[0, pl.ds(dn_start, gw), :], 0.0)
        up = jnp.concatenate([up_halo, cur[: ch - gw]], axis=0)
        down = jnp.concatenate([cur[gw:], dn_halo], axis=0)
        mixed = 0.25 * (up + down + left + right)
    return cur + mu * (mixed - cur)


def _prescan_kernel(r_ref, k_ref, v_ref, lora_ref, mur_ref, muk_ref, muv_ref, mul_ref,
                    uph_ref, upl_ref, bias_ref, kk_ref, ka_ref, rk_ref,
                    r_o, v_o, a_o, bonus_o, w_o, kd_o, bd_o, *, t_len, grid_shift):
    n_chunks = t_len // ROW_CHUNK
    seg = _seg_ones()
    lane = lax.broadcasted_iota(jnp.int32, (1, LORA_BLK), 1)

    def body(c, carry):
        sh = functools.partial(_shifted, c=c, n_chunks=n_chunks, t_len=t_len, grid_shift=grid_shift)
        r = sh(r_ref, mur_ref[...])
        k = sh(k_ref, muk_ref[...])
        v = sh(v_ref, muv_ref[...])
        lora = sh(lora_ref, mul_ref[...])
        lora = jnp.where(lane < 2 * LORA_R, jnp.tanh(lora), lora)
        l_hi, l_lo = _split2(lora)
        raw = (jnp.dot(l_hi, uph_ref[0], preferred_element_type=F32)
               + jnp.dot(l_hi, upl_ref[0], preferred_element_type=F32)
               + jnp.dot(l_lo, uph_ref[0], preferred_element_type=F32)) + bias_ref[0]
        kk = k * kk_ref[...]
        kk = kk * lax.rsqrt(_seg_sum(kk * kk, seg) + 1e-12)
        rows = pl.ds(pl.multiple_of(c * ROW_CHUNK, ROW_CHUNK), ROW_CHUNK)
        kd_sum = jnp.zeros_like(k)
        for d in range(2):
            decay = jnp.exp(-DECAY_SCALE * jax.nn.sigmoid(raw[:, d * LANES:(d + 1) * LANES]))
            iclr = jax.nn.sigmoid(raw[:, (2 + d) * LANES:(3 + d) * LANES])
            kd = k * (1.0 + (iclr - 1.0) * ka_ref[...])
            w_o[d, 0, rows, :] = decay
            kd_o[d, 0, rows, :] = kd
            bd_o[d, 0, rows, :] = kk * iclr
            kd_sum = kd_sum + kd
        r_o[0, rows, :] = r
        v_o[0, rows, :] = v
        a_o[0, rows, :] = -kk
        bonus_o[0, rows, :] = _seg_sum(r * (0.5 * kd_sum) * rk_ref[...], seg) * v
        return carry

    lax.fori_loop(0, n_chunks, body, 0)


def pre_scan(proj, mu, up_w, up_b, k_k, k_a, r_k, grid_shift):
    b, t, _ = proj.shape
    up_hi, up_lo = _split2(up_w)
    if not grid_shift:
        assert t == ROW_CHUNK, "sequence shift handles one chunk per sequence"
    assert t % ROW_CHUNK == 0
    col = lambda off: pl.BlockSpec((1, t, LANES), lambda i, j: (i, 0, off + j))
    vec = lambda off: pl.BlockSpec((1, LANES), lambda i, j: (0, off + j))
    out1 = pl.BlockSpec((1, t, LANES), lambda i, j: (i, 0, j))
    out2 = pl.BlockSpec((2, 1, t, LANES), lambda i, j: (0, i, 0, j))
    s1 = jax.ShapeDtypeStruct((b, t, RWKV_W), F32)
    s2 = jax.ShapeDtypeStruct((2, b, t, RWKV_W), F32)
    return pl.pallas_call(
        functools.partial(_prescan_kernel, t_len=t, grid_shift=grid_shift),
        grid=(b, HEAD_TILES),
        in_specs=[col(0), col(HEAD_TILES), col(2 * HEAD_TILES),
                  pl.BlockSpec((1, t, LORA_BLK), lambda i, j: (i, 0, LORA_OFF // LORA_BLK)),
                  vec(0), vec(HEAD_TILES), vec(2 * HEAD_TILES),
                  pl.BlockSpec((1, LORA_BLK), lambda i, j: (0, LORA_OFF // LORA_BLK)),
                  pl.BlockSpec((1, LORA_BLK, 4 * LANES), lambda i, j: (j, 0, 0)),
                  pl.BlockSpec((1, LORA_BLK, 4 * LANES), lambda i, j: (j, 0, 0)),
                  pl.BlockSpec((1, 1, 4 * LANES), lambda i, j: (j, 0, 0)),
                  vec(0), vec(0), vec(0)],
        out_specs=[out1, out1, out1, out1, out2, out2, out2],
        out_shape=[s1, s1, s1, s1, s2, s2, s2],
        compiler_params=_params("arbitrary", "arbitrary"),
        name="pre_scan",
    )(proj, proj, proj, proj, mu, mu, mu, mu, up_hi, up_lo, up_b,
      k_k.reshape(1, RWKV_W), k_a.reshape(1, RWKV_W), r_k.reshape(1, RWKV_W))


def _lora_up_tables(w_decay_up, w_decay0, w_iclr_up, w_iclr0):
    mats = [w_decay_up[0], w_decay_up[1], w_iclr_up[0], w_iclr_up[1]]
    bias = [w_decay0[0], w_decay0[1], w_iclr0[0], w_iclr0[1]]
    up = jnp.zeros((HEAD_TILES, LORA_BLK, 4 * LANES), F32)
    for q, m in enumerate(mats):
        blk = m.astype(F32).reshape(LORA_R, HEAD_TILES, LANES).transpose(1, 0, 2)
        up = up.at[:, q * LORA_R:(q + 1) * LORA_R, q * LANES:(q + 1) * LANES].set(blk)
    b = jnp.stack([v.astype(F32).reshape(HEAD_TILES, LANES) for v in bias], axis=1)
    return up, b.reshape(HEAD_TILES, 1, 4 * LANES)


def _scan_kernel(*refs, tc, mixed):
    n_in = 12 if mixed else 6
    ins, rest = refs[:n_in], refs[n_in:]
    if mixed:
        s0_ref, y_ref, yb_ref, sfin_ref, s_scr, wr_scr = rest
        fwd_lane = lax.broadcasted_iota(jnp.int32, (1, LANES), 1) < LANES // 2
        tt = lambda t: t

        def view(i):
            fw, bw = ins[i], ins[n_in // 2 + i]
            full = lambda t: jnp.where(fwd_lane, fw[t], bw[tc - 1 - t])
            row = lambda t, k: jnp.where(fwd_lane, fw[t, k:k + 1, :], bw[tc - 1 - t, k:k + 1, :])
            return full, row
    else:
        s0_ref, y_ref, sfin_ref, s_scr, wr_scr = rest
        backward = pl.program_id(0) >= pl.num_programs(0) // 2
        tt = lambda t: jnp.where(backward, tc - 1 - t, t)

        def view(i):
            ref = ins[i]
            return (lambda t: ref[tt(t)]), (lambda t, k: ref[tt(t), k:k + 1, :])

    (w_full, w_row), (k_full, k_row), (b_full, b_row), (_, a_row), (r_full, _), (v_full, _) = (
        view(i) for i in range(6))
    c = pl.program_id(1)

    @pl.when(c == 0)
    def _():
        s_scr[...] = s0_ref[0]

    wr_scr[0] = w_full(0) * r_full(0)
    sa = jnp.zeros((HEAD_N, LANES), F32)
    y0 = jnp.zeros((HEAD_N, LANES), F32)
    for k in range(HEAD_N):
        s = s_scr[k]
        sa = sa + s * a_row(0, k)
        y0 = y0 + s * wr_scr[0, k:k + 1, :]

    def step(t, carry):
        sa, y0 = carry
        r_t = r_full(t)
        v_t = v_full(t)
        br = jnp.sum(b_full(t) * r_t, axis=0, keepdims=True)
        kr = jnp.sum(k_full(t) * r_t, axis=0, keepdims=True)
        y = y0 + sa * br + v_t * kr
        y_ref[tt(t)] = y
        if mixed:
            yb_ref[tc - 1 - t] = y
        tn = jnp.minimum(t + 1, tc - 1)
        slot = (t + 1) % 2
        wr_scr[slot] = w_full(tn) * r_full(tn)
        sa_n = jnp.zeros((HEAD_N, LANES), F32)
        y0_n = jnp.zeros((HEAD_N, LANES), F32)
        for k in range(HEAD_N):
            s = s_scr[k] * w_row(t, k) + sa * b_row(t, k) + v_t * k_row(t, k)
            s_scr[k] = s
            sa_n = sa_n + s * a_row(tn, k)
            y0_n = y0_n + s * wr_scr[slot, k:k + 1, :]
        return sa_n, y0_n

    lax.fori_loop(0, tc, step, (sa, y0))

    @pl.when(c == pl.num_programs(1) - 1)
    def _():
        sfin_ref[0] = s_scr[...]


def wkv_scan(ops, s0, tc=32):
    t, _, lanes = ops[0].shape
    groups = lanes // LANES
    nc = t // tc
    mixed = groups == 1
    st = pl.BlockSpec((1, HEAD_N, HEAD_N, LANES), lambda g, c: (g, 0, 0, 0))
    y_shape = jax.ShapeDtypeStruct((t, HEAD_N, lanes), F32)
    if mixed:
        fwd = pl.BlockSpec((tc, HEAD_N, LANES), lambda g, c: (c, 0, g))
        bwd = pl.BlockSpec((tc, HEAD_N, LANES), lambda g, c: (nc - 1 - c, 0, g))
        in_specs, operands = [fwd] * 6 + [bwd] * 6 + [st], list(ops) + list(ops) + [s0]
        out_specs, out_shape = [fwd, bwd, st], [y_shape, y_shape]
    else:
        half_g = groups // 2
        tmap = lambda g, c: jnp.where(g >= half_g, nc - 1 - c, c)
        seq = pl.BlockSpec((tc, HEAD_N, LANES), lambda g, c: (tmap(g, c), 0, g))
        shared = pl.BlockSpec((tc, HEAD_N, LANES), lambda g, c: (tmap(g, c), 0, g % half_g))
        in_specs = [seq if o.shape[2] == lanes else shared for o in ops] + [st]
        operands = list(ops) + [s0]
        out_specs, out_shape = [seq, st], [y_shape]
    out_shape.append(jax.ShapeDtypeStruct((groups, HEAD_N, HEAD_N, LANES), F32))
    res = pl.pallas_call(
        functools.partial(_scan_kernel, tc=tc, mixed=mixed),
        grid=(groups, nc),
        in_specs=in_specs,
        out_specs=out_specs,
        out_shape=out_shape,
        scratch_shapes=[pltpu.VMEM((HEAD_N, HEAD_N, LANES), F32),
                        pltpu.VMEM((2, HEAD_N, LANES), F32)],
        compiler_params=_params("arbitrary", "arbitrary"),
        name="wkv_scan",
    )(*operands)
    half = lanes // 2
    if mixed:
        y_f, y_b, s_fin = res
        return y_f[..., :half] + y_b[..., half:], s_fin
    y, s_fin = res
    return y[..., :half] + y[..., half:], s_fin


K_BLOCK = 8


def _relayout_kernel(*refs, n_ops, rb, dup):
    ins, outs, scr = refs[:n_ops], refs[n_ops:2 * n_ops], refs[2 * n_ops]
    for x_ref, o_ref in zip(ins, outs):
        x2 = x_ref.reshape(rb * K_BLOCK, LANES)
        for r0 in range(0, rb, LANES):
            n_rows = min(LANES, rb - r0)
            for kk in range(K_BLOCK):
                m = x2[pl.ds(r0 * K_BLOCK + kk, n_rows, stride=K_BLOCK), :]
                if dup:
                    m = jnp.concatenate([m, m], axis=0)
                scr[pl.ds(kk, LANES, stride=K_BLOCK), :] = m.T
            o_ref[:, :, r0:r0 + LANES] = scr[...].reshape(LANES, K_BLOCK, LANES)


def relayout_to_scan(xs, dup=False):
    rows, _, t = xs[0].shape
    if dup:
        assert 2 * rows == LANES
    else:
        assert rows % LANES == 0
    rb = min(rows, 512)
    lanes_blk = 2 * rb if dup else rb
    n = len(xs)
    return pl.pallas_call(
        functools.partial(_relayout_kernel, n_ops=n, rb=rb, dup=dup),
        grid=(rows // rb, HEAD_N // K_BLOCK, t // LANES),
        in_specs=[pl.BlockSpec((rb, K_BLOCK, LANES), lambda r, k, c: (r, k, c))] * n,
        out_specs=[pl.BlockSpec((LANES, K_BLOCK, lanes_blk), lambda r, k, c: (c, k, r))] * n,
        out_shape=[jax.ShapeDtypeStruct((t, HEAD_N, lanes_blk * (rows // rb)), F32)] * n,
        scratch_shapes=[pltpu.VMEM((LANES * K_BLOCK, LANES), F32)],
        compiler_params=_params("arbitrary", "arbitrary", "arbitrary"),
        name="relayout_to_scan",
    )(*xs)


def _scan_rows(x):
    t = x.shape[-2]
    return jnp.swapaxes(x, -1, -2).reshape(-1, HEAD_N, t)


def _from_scan_layout(y, b):
    t = y.shape[0]
    return y.reshape(t, HEAD_N, b, N_HEADS).transpose(2, 0, 3, 1).reshape(b, t, RWKV_W)


def _postscan_kernel(y_ref, bonus_ref, g_ref, gw_ref, gb_ref, o_ref):
    seg = _seg_ones()
    for j in range(HEAD_TILES):
        cols = slice(j * LANES, (j + 1) * LANES)
        y = y_ref[0, :, cols]
        mu = _seg_sum(y, seg) * (1.0 / HEAD_N)
        d = y - mu
        var = _seg_sum(d * d, seg) * (1.0 / HEAD_N)
        yn = d * lax.rsqrt(var + GN_EPS) * gw_ref[:, cols] + gb_ref[:, cols]
        o_ref[0, :, cols] = ((yn + bonus_ref[0, :, cols]) * jax.nn.sigmoid(g_ref[0, :, cols])).astype(BF16)


def post_scan(y, bonus, proj, gn_w, gn_b):
    b, t, _ = y.shape
    ch = ROW_CHUNK
    blk = pl.BlockSpec((1, ch, RWKV_W), lambda i, c: (i, c, 0))
    vec = pl.BlockSpec((1, RWKV_W), lambda i, c: (0, 0))
    return pl.pallas_call(
        _postscan_kernel,
        grid=(b, t // ch),
        in_specs=[blk, blk,
                  pl.BlockSpec((1, ch, RWKV_W), lambda i, c: (i, c, G_OFF // RWKV_W)),
                  vec, vec],
        out_specs=blk,
        out_shape=jax.ShapeDtypeStruct((b, t, RWKV_W), BF16),
        compiler_params=_params("arbitrary", "arbitrary"),
        name="post_scan",
    )(y, bonus, proj, gn_w.reshape(1, RWKV_W), gn_b.reshape(1, RWKV_W))


def _fnet_kernel(f_ref, cc_ref, ct_ref, o_ref, xcs_ref, *, t_len, scale):
    @pl.when(pl.program_id(2) == 0)
    def _():
        xc = jnp.dot(f_ref[0].astype(BF16), cc_ref[...], preferred_element_type=F32)
        xcs_ref[0:t_len, :] = xc[:, :FOURIER_GW].astype(BF16)
        xcs_ref[t_len:2 * t_len, :] = xc[:, FOURIER_GW:].astype(BF16)

    out = jnp.dot(ct_ref[...], xcs_ref[...], preferred_element_type=F32)
    o_ref[0] = (out * scale).astype(BF16)


def _dft_tables(n):
    j = lax.broadcasted_iota(jnp.int32, (n, n), 0)
    k = lax.broadcasted_iota(jnp.int32, (n, n), 1)
    ang = ((j * k) % n).astype(F32) * (2.0 * math.pi / n)
    return jnp.cos(ang), jnp.sin(ang)


def fnet_mix(proj):
    b, t, _ = proj.shape
    cc, sc = _dft_tables(FOURIER_GW)
    ct, st = _dft_tables(t)
    cc2 = jnp.concatenate([cc, sc], axis=1).astype(BF16)
    ct2 = jnp.concatenate([ct, -st], axis=1).astype(BF16)
    tq = min(512, t)
    return pl.pallas_call(
        functools.partial(_fnet_kernel, t_len=t, scale=1.0 / math.sqrt(t * FOURIER_GW)),
        grid=(b, N_FOURIER_GROUPS, t // tq),
        in_specs=[pl.BlockSpec((1, t, FOURIER_GW), lambda i, g, q: (i, 0, F_OFF // FOURIER_GW + g)),
                  pl.BlockSpec((FOURIER_GW, 2 * FOURIER_GW), lambda i, g, q: (0, 0)),
                  pl.BlockSpec((tq, 2 * t), lambda i, g, q: (q, 0))],
        out_specs=pl.BlockSpec((1, tq, FOURIER_GW), lambda i, g, q: (i, q, g)),
        out_shape=jax.ShapeDtypeStruct((b, t, RWKV_W), BF16),
        scratch_shapes=[pltpu.VMEM((2 * t, FOURIER_GW), BF16)],
        compiler_params=_params("arbitrary", "arbitrary", "arbitrary"),
        name="fnet_mix",
    )(proj, cc2, ct2)


def _outproj_kernel(yr_ref, yf_ref, w_ref, x_ref, mod_ref, o_ref):
    m = jnp.dot(yr_ref[...], w_ref[0:RWKV_W, :], preferred_element_type=F32)
    m = m + jnp.dot(yf_ref[...], w_ref[RWKV_W:2 * RWKV_W, :], preferred_element_type=F32)
    o_ref[...] = x_ref[...] + mod_ref[0, 2:3, :] * m


def out_proj(yr, yf, w_bf, x2d, mod, rows_per_mod, mod_base):
    m, d = x2d.shape
    tm, tn = min(1024, rows_per_mod), 512
    row_of = _mod_row_map(tm, rows_per_mod, mod_base)
    return pl.pallas_call(
        _outproj_kernel,
        grid=(m // tm, d // tn),
        in_specs=[pl.BlockSpec((tm, RWKV_W), lambda i, j: (i, 0)),
                  pl.BlockSpec((tm, RWKV_W), lambda i, j: (i, 0)),
                  pl.BlockSpec((2 * RWKV_W, tn), lambda i, j: (0, j)),
                  pl.BlockSpec((tm, tn), lambda i, j: (i, j)),
                  pl.BlockSpec((1, N_MOD, tn), lambda i, j: (row_of(i), 0, j))],
        out_specs=pl.BlockSpec((tm, tn), lambda i, j: (i, j)),
        out_shape=jax.ShapeDtypeStruct((m, d), F32),
        compiler_params=_params("arbitrary", "arbitrary"),
        name="out_proj",
    )(yr, yf, w_bf, x2d, mod)


def _norm2_router_kernel(x_ref, mod_ref, g_ref, rw_ref, h_ref, aff_ref):
    h = _rms_modulate(x_ref[...], g_ref[...], mod_ref[0, 4:5, :], mod_ref[0, 3:4, :])
    h_ref[...] = h
    logits = jnp.dot(h, rw_ref[...], precision=HIGHEST, preferred_element_type=F32)
    z = logits - jnp.max(logits, axis=-1, keepdims=True)
    e = jnp.exp(z)
    aff_ref[...] = e / jnp.sum(e, axis=-1, keepdims=True)


def norm2_router(x2d, mod, norm_g, router_w, rows_per_mod, mod_base):
    m, d = x2d.shape
    tm = ROW_CHUNK
    row_of = _mod_row_map(tm, rows_per_mod, mod_base)
    return pl.pallas_call(
        _norm2_router_kernel,
        grid=(m // tm,),
        in_specs=[pl.BlockSpec((tm, d), lambda i: (i, 0)),
                  pl.BlockSpec((1, N_MOD, d), lambda i: (row_of(i), 0, 0)),
                  pl.BlockSpec((1, d), lambda i: (0, 0)),
                  pl.BlockSpec((d, N_EXPERTS), lambda i: (0, 0))],
        out_specs=[pl.BlockSpec((tm, d), lambda i: (i, 0)),
                   pl.BlockSpec((tm, N_EXPERTS), lambda i: (i, 0))],
        out_shape=[jax.ShapeDtypeStruct((m, d), F32),
                   jax.ShapeDtypeStruct((m, N_EXPERTS), F32)],
        compiler_params=_params("arbitrary"),
        name="norm2_router",
    )(x2d, mod, norm_g.reshape(1, d), router_w)


GATHER_ROWS = 256
PAIR_CHUNK = LANES


def _issue_row_gather(idx_ref, base, n_rows, src_hbm, dst, sem):
    def body(r, carry):
        row = idx_ref[base + r]
        pltpu.make_async_copy(src_hbm.at[pl.ds(row, 1), :], dst.at[pl.ds(r, 1), :], sem).start()
        return carry
    lax.fori_loop(0, n_rows, body, 0, unroll=8)


def _wait_row_gather(n_rows, src_hbm, dst, sem):
    pltpu.make_async_copy(src_hbm.at[pl.ds(0, n_rows), :], dst, sem).wait()


def _gather_kernel(idx_ref, h_hbm, o_ref, buf, sem):
    s = pl.program_id(0)
    slot = s % 2

    @pl.when(s == 0)
    def _():
        _issue_row_gather(idx_ref, 0, GATHER_ROWS, h_hbm, buf.at[0], sem.at[0])

    @pl.when(s + 1 < pl.num_programs(0))
    def _():
        _issue_row_gather(idx_ref, (s + 1) * GATHER_ROWS, GATHER_ROWS, h_hbm,
                          buf.at[1 - slot], sem.at[1 - slot])

    _wait_row_gather(GATHER_ROWS, h_hbm, buf.at[slot], sem.at[slot])
    o_ref[...] = buf[slot].astype(BF16)


def gather_tokens(h, idx_flat):
    n, d = h.shape
    rows = idx_flat.shape[0]
    return pl.pallas_call(
        _gather_kernel,
        grid_spec=pltpu.PrefetchScalarGridSpec(
            num_scalar_prefetch=1,
            grid=(rows // GATHER_ROWS,),
            in_specs=[pl.BlockSpec(memory_space=pl.ANY)],
            out_specs=pl.BlockSpec((GATHER_ROWS, d), lambda s, idx: (s, 0)),
            scratch_shapes=[pltpu.VMEM((2, GATHER_ROWS, d), F32), pltpu.SemaphoreType.DMA((2,))]),
        out_shape=jax.ShapeDtypeStruct((rows, d), BF16),
        compiler_params=_params("arbitrary"),
        name="gather_tokens",
    )(idx_flat, h)


def _expert_kernel(xe_ref, w1_ref, w3_ref, w2_ref, gate_ref, o_ref, *, tf, ff):
    f = pl.program_id(1)
    x = xe_ref[0]
    h1 = jnp.dot(x, w1_ref[0].astype(BF16), preferred_element_type=F32)
    h3 = jnp.dot(x, w3_ref[0].astype(BF16), preferred_element_type=F32)
    hid = h1 * jax.nn.sigmoid(h1) * h3
    valid = ff - f * tf
    hid = jnp.where(lax.broadcasted_iota(jnp.int32, (1, tf), 1) < valid, hid, 0.0)
    w2 = jnp.where(lax.broadcasted_iota(jnp.int32, (tf, 1), 0) < valid, w2_ref[0], 0.0)
    @pl.when(f == 0)
    def _():
        o_ref[...] = jnp.zeros_like(o_ref)

    o_ref[0] += jnp.dot(hid.astype(BF16), w2.astype(BF16), preferred_element_type=F32)

    @pl.when(f == pl.num_programs(1) - 1)
    def _():
        o_ref[0] = o_ref[0] * gate_ref[0]


def expert_ffn(xe, w1, w3, w2, gate):
    e, cap, d = xe.shape
    ff = w1.shape[2]
    tf = 512
    nf = pl.cdiv(ff, tf)
    return pl.pallas_call(
        functools.partial(_expert_kernel, tf=tf, ff=ff),
        grid=(e, nf),
        in_specs=[pl.BlockSpec((1, cap, d), lambda i, f: (i, 0, 0), pipeline_mode=pl.Buffered(1)),
                  pl.BlockSpec((1, d, tf), lambda i, f: (i, 0, f)),
                  pl.BlockSpec((1, d, tf), lambda i, f: (i, 0, f)),
                  pl.BlockSpec((1, tf, d), lambda i, f: (i, f, 0)),
                  pl.BlockSpec((1, cap, 1), lambda i, f: (i, 0, 0))],
        out_specs=pl.BlockSpec((1, cap, d), lambda i, f: (i, 0, 0)),
        out_shape=jax.ShapeDtypeStruct((e, cap, d), F32),
        compiler_params=_params("arbitrary", "arbitrary"),
        name="expert_ffn",
    )(xe, w1, w3, w2, gate.reshape(e, cap, 1))


def _combine_final_kernel(rows_ref, ptr_ref, tok_ref, ye_hbm, x_ref, mod_ref, g_ref, o_ref,
                          buf, sem, acc_ref, *, tm):
    i = pl.program_id(0)
    p0 = ptr_ref[i]
    p1 = ptr_ref[i + 1]
    c0 = p0 // PAIR_CHUNK
    c1 = jnp.where(p1 > p0, (p1 - 1) // PAIR_CHUNK + 1, c0)
    acc_ref[...] = jnp.zeros_like(acc_ref)

    @pl.when(c1 > c0)
    def _():
        _issue_row_gather(rows_ref, c0 * PAIR_CHUNK, PAIR_CHUNK, ye_hbm, buf.at[0], sem.at[0])

    def chunk(c, carry):
        slot = (c - c0) % 2

        @pl.when(c + 1 < c1)
        def _():
            _issue_row_gather(rows_ref, (c + 1) * PAIR_CHUNK, PAIR_CHUNK, ye_hbm,
                              buf.at[1 - slot], sem.at[1 - slot])

        _wait_row_gather(PAIR_CHUNK, ye_hbm, buf.at[slot], sem.at[slot])
        local = tok_ref[pl.ds(c, 1), :] - i * tm
        onehot = (lax.broadcasted_iota(jnp.int32, (tm, PAIR_CHUNK), 0) == local).astype(BF16)
        hi, lo = _split2(buf[slot])
        acc_ref[...] += (jnp.dot(onehot, hi, preferred_element_type=F32)
                         + jnp.dot(onehot, lo, preferred_element_type=F32))
        return carry

    lax.fori_loop(c0, c1, chunk, 0)
    x = x_ref[...] + mod_ref[0, 5:6, :] * acc_ref[...]
    ms = jnp.mean(x * x, axis=-1, keepdims=True)
    o_ref[...] = x * lax.rsqrt(ms + NORM_EPS) * g_ref[...]


def combine_final(x2d, ye_flat, pair_rows, pair_toks, tile_ptr, mod, final_g, rows_per_mod, mod_base):
    m, d = x2d.shape
    tm = ROW_CHUNK
    n_pairs = pair_rows.shape[0]
    row_of = _mod_row_map(tm, rows_per_mod, mod_base)
    return pl.pallas_call(
        functools.partial(_combine_final_kernel, tm=tm),
        grid_spec=pltpu.PrefetchScalarGridSpec(
            num_scalar_prefetch=2,
            grid=(m // tm,),
            in_specs=[pl.BlockSpec((n_pairs // PAIR_CHUNK, PAIR_CHUNK), lambda i, r, p: (0, 0)),
                      pl.BlockSpec(memory_space=pl.ANY),
                      pl.BlockSpec((tm, d), lambda i, r, p: (i, 0)),
                      pl.BlockSpec((1, N_MOD, d), lambda i, r, p: (row_of(i), 0, 0)),
                      pl.BlockSpec((1, d), lambda i, r, p: (0, 0))],
            out_specs=pl.BlockSpec((tm, d), lambda i, r, p: (i, 0)),
            scratch_shapes=[pltpu.VMEM((2, PAIR_CHUNK, d), F32), pltpu.SemaphoreType.DMA((2,)),
                            pltpu.VMEM((tm, d), F32)]),
        out_shape=jax.ShapeDtypeStruct((m, d), F32),
        compiler_params=_params("arbitrary"),
        name="combine_final",
    )(pair_rows, tile_ptr, pair_toks.reshape(n_pairs // PAIR_CHUNK, PAIR_CHUNK), ye_flat,
      x2d, mod, final_g.reshape(1, d))


def _prepare_in_proj(w_in, shift_mu):
    d = w_in.shape[0]
    rkv, lora, gf = w_in[:, :G_OFF], w_in[:, G_OFF:SHIFT_W], w_in[:, SHIFT_W:]
    w_p = jnp.concatenate([rkv, gf, lora, jnp.zeros((d, LANES), w_in.dtype)], axis=1).astype(BF16)
    mu = jnp.concatenate([shift_mu[:G_OFF], jnp.zeros((2 * RWKV_W,), F32), shift_mu[G_OFF:],
                          jnp.zeros((LANES,), F32)])
    return w_p, mu.reshape(1, IN_PAD_W)


def _layer(x, mod, rows_per_mod, mod_base, s0, grid_shift, p):
    b, t, d = x.shape
    x2d = x.reshape(b * t, d)
    proj = in_proj(x2d, mod, p["norm1_g"], p["w_in_p"], rows_per_mod, mod_base)
    proj = proj.reshape(b, t, IN_PAD_W)
    r, v, a, bonus, w, kd, bd = pre_scan(proj, p["mu"], p["up_w"], p["up_b"],
                                         p["k_k"], p["k_a"], p["r_k"], grid_shift)
    per_dir = relayout_to_scan([_scan_rows(w), _scan_rows(kd), _scan_rows(bd)])
    shared = relayout_to_scan([_scan_rows(a), _scan_rows(r), _scan_rows(v)],
                              dup=2 * b * N_HEADS == LANES)
    y, s_fin = wkv_scan(list(per_dir) + list(shared), s0)
    yr = post_scan(_from_scan_layout(y, b), bonus, proj, p["gn_w"], p["gn_b"])
    yf = fnet_mix(proj)
    x1 = out_proj(yr.reshape(b * t, RWKV_W), yf.reshape(b * t, RWKV_W), p["w_out_bf"],
                  x2d, mod, rows_per_mod, mod_base)
    h2, aff = norm2_router(x1, mod, p["norm2_g"], p["router_w"], rows_per_mod, mod_base)
    n = b * t
    cap = n * CAPACITY_FACTOR // N_EXPERTS
    gate, idx = lax.top_k(aff.T, cap)
    idx_flat = idx.reshape(-1).astype(jnp.int32)
    xe = gather_tokens(h2, idx_flat).reshape(N_EXPERTS, cap, d)
    ye = expert_ffn(xe, p["exp_w1"], p["exp_w3"], p["exp_w2"], gate)
    pair_toks, pair_rows = lax.sort_key_val(idx_flat, jnp.arange(idx_flat.shape[0], dtype=jnp.int32))
    tile_ptr = jnp.searchsorted(pair_toks, jnp.arange(0, n + 1, ROW_CHUNK, dtype=jnp.int32),
                                side="left").astype(jnp.int32)
    out = combine_final(x1, ye.reshape(-1, d), pair_rows, pair_toks, tile_ptr, mod,
                        p["final_norm_g"], rows_per_mod, mod_base)
    return out.reshape(b, t, d), s_fin


def kernel(x_prompt, x_sample, state_wkv, c, c_ctx, ada_w, ada_b, norm1_g, norm2_g, w_in, shift_mu,
           w_decay_up, w_decay0, w_iclr_up, w_iclr0, k_k, k_a, r_k, gn_w, gn_b, w_out, router_w,
           exp_w1, exp_w3, exp_w2, final_norm_g):
    depth = ada_w.shape[0]
    assert depth == 1, "single-layer trunk"
    l = 0
    bc, tc_len, d = x_prompt.shape
    bl, tl, _ = x_sample.shape
    cvecs = jnp.concatenate([c_ctx[None], c], axis=0)
    mod = ada_mod(cvecs, ada_w[l], ada_b[l]).reshape(1 + bl, N_MOD, d)
    up_w, up_b = _lora_up_tables(w_decay_up[l], w_decay0[l], w_iclr_up[l], w_iclr0[l])
    w_in_p, mu = _prepare_in_proj(w_in[l], shift_mu[l])
    p = {
        "norm1_g": norm1_g[l], "norm2_g": norm2_g[l], "final_norm_g": final_norm_g,
        "w_in_p": w_in_p, "mu": mu, "up_w": up_w, "up_b": up_b,
        "k_k": k_k[l], "k_a": k_a[l], "r_k": r_k[l], "gn_w": gn_w[l], "gn_b": gn_b[l],
        "w_out_bf": w_out[l].astype(BF16), "router_w": router_w[l],
        "exp_w1": exp_w1[l], "exp_w3": exp_w3[l], "exp_w2": exp_w2[l],
    }
    groups_c = 2 * bc * N_HEADS // LANES
    s0_ctx = jnp.zeros((groups_c, HEAD_N, HEAD_N, LANES), F32)
    y_prompt, s_ctx = _layer(x_prompt, mod, bc * tc_len, 0, s0_ctx, False, p)
    s_ctx = s_ctx.transpose(1, 2, 0, 3).reshape(HEAD_N, HEAD_N, 2, bc, N_HEADS)
    new_state = s_ctx.transpose(3, 2, 4, 1, 0)[:, None]
    groups_l = 2 * bl * N_HEADS // LANES
    s0_lat = state_wkv[:, l].astype(F32).transpose(4, 3, 1, 0, 2)
    s0_lat = s0_lat.reshape(HEAD_N, HEAD_N, groups_l, LANES).transpose(2, 0, 1, 3)
    y_sample, _ = _layer(x_sample, mod, tl, 1, s0_lat, True, p)
    return (y_prompt, y_sample, new_state)
```

```python
import functools
import math

import jax
import jax.numpy as jnp
from jax import lax
from jax.experimental import pallas as pl
from jax.experimental.pallas import tpu as pltpu

F32 = jnp.float32
BF16 = jnp.bfloat16
HIGHEST = lax.Precision.HIGHEST

D_MODEL = 2048
RWKV_W = 1024
HEAD_N = 64
N_HEADS = 16
LORA_R = 96
N_EXPERTS = 16
EXPERT_FF = 5504
CAPACITY_FACTOR = 2
N_MOD = 6
NORM_EPS = 1e-6
GN_EPS = 64e-5
FOURIER_GW = 256
N_FOURIER_GROUPS = 4

LANES = 128
HEAD_TILES = RWKV_W // LANES
SHIFT_W = 3 * RWKV_W + 4 * LORA_R
LORA_W = 4 * LORA_R
LORA_BLK = LORA_W + LANES
G_OFF = 3 * RWKV_W
F_OFF = G_OFF + RWKV_W
LORA_OFF = F_OFF + RWKV_W
IN_PAD_W = LORA_OFF + LORA_BLK
ROW_CHUNK = 256
VMEM_LIMIT = 56 * 1024 * 1024
DECAY_SCALE = math.exp(-0.5)


def _params(*sem):
    return pltpu.CompilerParams(dimension_semantics=sem, vmem_limit_bytes=VMEM_LIMIT)


def _seg_ones():
    r = lax.broadcasted_iota(jnp.int32, (LANES, LANES), 0) // HEAD_N
    c = lax.broadcasted_iota(jnp.int32, (LANES, LANES), 1) // HEAD_N
    return (r == c).astype(BF16)


def _split2(x):
    hi = x.astype(BF16)
    return hi, (x - hi.astype(F32)).astype(BF16)


def _split3(x):
    hi = x.astype(BF16)
    rem = x - hi.astype(F32)
    mid = rem.astype(BF16)
    return hi, mid, (rem - mid.astype(F32)).astype(BF16)


def _seg_sum(x, seg):
    return sum(jnp.dot(p, seg, preferred_element_type=F32) for p in _split3(x))


def _ada_kernel(c_ref, w_ref, b_ref, o_ref):
    c = c_ref[...]
    s = c * jax.nn.sigmoid(c)
    o_ref[...] = jnp.dot(s.astype(BF16), w_ref[...].astype(BF16),
                         preferred_element_type=F32) + b_ref[...]


def ada_mod(cvecs, ada_w, ada_b):
    rows, d = cvecs.shape
    n = ada_w.shape[1]
    tn = 1024
    return pl.pallas_call(
        _ada_kernel,
        grid=(n // tn,),
        in_specs=[pl.BlockSpec((rows, d), lambda j: (0, 0)),
                  pl.BlockSpec((d, tn), lambda j: (0, j)),
                  pl.BlockSpec((1, tn), lambda j: (0, j))],
        out_specs=pl.BlockSpec((rows, tn), lambda j: (0, j)),
        out_shape=jax.ShapeDtypeStruct((rows, n), F32),
        compiler_params=_params("arbitrary"),
        name="ada_mod",
    )(cvecs, ada_w, ada_b.reshape(1, n))


def _mod_row_map(tm, rows_per_mod, mod_base):
    return lambda i: mod_base + (i * tm) // rows_per_mod


def _rms_modulate(x, g, scale, shift):
    ms = jnp.mean(x * x, axis=-1, keepdims=True)
    return x * lax.rsqrt(ms + NORM_EPS) * g * (1.0 + scale) + shift


def _inproj_kernel(x_ref, mod_ref, g_ref, w_ref, o_ref, h_ref, *, tm):
    @pl.when(pl.program_id(1) == 0)
    def _():
        def body(c, carry):
            rows = pl.ds(pl.multiple_of(c * ROW_CHUNK, ROW_CHUNK), ROW_CHUNK)
            h = _rms_modulate(x_ref[rows, :], g_ref[...], mod_ref[0, 1:2, :], mod_ref[0, 0:1, :])
            h_ref[rows, :] = h.astype(BF16)
            return carry
        lax.fori_loop(0, tm // ROW_CHUNK, body, 0)

    o_ref[...] = jnp.dot(h_ref[...], w_ref[...], preferred_element_type=F32)


def in_proj(x2d, mod, norm_g, w_p, rows_per_mod, mod_base):
    m, d = x2d.shape
    n = w_p.shape[1]
    tm, tn = min(1024, rows_per_mod), 512
    row_of = _mod_row_map(tm, rows_per_mod, mod_base)
    return pl.pallas_call(
        functools.partial(_inproj_kernel, tm=tm),
        grid=(m // tm, n // tn),
        in_specs=[pl.BlockSpec((tm, d), lambda i, j: (i, 0)),
                  pl.BlockSpec((1, N_MOD, d), lambda i, j: (row_of(i), 0, 0)),
                  pl.BlockSpec((1, d), lambda i, j: (0, 0)),
                  pl.BlockSpec((d, tn), lambda i, j: (0, j))],
        out_specs=pl.BlockSpec((tm, tn), lambda i, j: (i, j)),
        out_shape=jax.ShapeDtypeStruct((m, n), F32),
        scratch_shapes=[pltpu.VMEM((tm, d), BF16)],
        compiler_params=_params("arbitrary", "arbitrary"),
        name="in_proj",
    )(x2d, mod, norm_g.reshape(1, d), w_p)


def _shifted(ref, mu, c, n_chunks, t_len, grid_shift):
    ch = ROW_CHUNK
    base = pl.multiple_of(c * ch, ch)
    cur = ref[0, pl.ds(base, ch), :]
    row = lax.broadcasted_iota(jnp.int32, (ch, 1), 0)
    before = pltpu.roll(cur, 1, 0)
    after = pltpu.roll(cur, ch - 1, 0)
    if not grid_shift:
        prev = jnp.where(row == 0, 0.0, before)
        nxt = jnp.where(row == ch - 1, 0.0, after)
        mixed = 0.5 * (prev + nxt)
    else:
        gw = 64
        left = jnp.where(row % gw == 0, 0.0, before)
        right = jnp.where(row % gw == gw - 1, 0.0, after)
        up_start = pl.multiple_of(jnp.maximum(base - gw, 0), gw)
        dn_start = pl.multiple_of(jnp.minimum(base + ch, t_len - gw), gw)
        up_halo = jnp.where(c > 0, ref[0, pl.ds(up_start, gw), :], 0.0)
        dn_halo = jnp.where(c < n_chunks - 1, ref[0, pl.ds(dn_start, gw), :], 0.0)
        up = jnp.concatenate([up_halo, cur[: ch - gw]], axis=0)
        down = jnp.concatenate([cur[gw:], dn_halo], axis=0)
        mixed = 0.25 * (up + down + left + right)
    return cur + mu * (mixed - cur)


def _prescan_kernel(r_ref, k_ref, v_ref, lora_ref, mur_ref, muk_ref, muv_ref, mul_ref,
                    uph_ref, upl_ref, bias_ref, kk_ref, ka_ref, rk_ref,
                    r_o, v_o, a_o, bonus_o, w_o, kd_o, bd_o, *, t_len, grid_shift):
    n_chunks = t_len // ROW_CHUNK
    seg = _seg_ones()
    lane = lax.broadcasted_iota(jnp.int32, (1, LORA_BLK), 1)
    sh = functools.partial(_shifted, c=pl.program_id(2), n_chunks=n_chunks, t_len=t_len,
                           grid_shift=grid_shift)
    r = sh(r_ref, mur_ref[...])
    k = sh(k_ref, muk_ref[...])
    v = sh(v_ref, muv_ref[...])
    lora = sh(lora_ref, mul_ref[...])
    lora = jnp.where(lane < 2 * LORA_R, jnp.tanh(lora), lora)
    l_hi, l_lo = _split2(lora)
    raw = (jnp.dot(l_hi, uph_ref[0], preferred_element_type=F32)
           + jnp.dot(l_hi, upl_ref[0], preferred_element_type=F32)
           + jnp.dot(l_lo, uph_ref[0], preferred_element_type=F32)) + bias_ref[0]
    kk = k * kk_ref[...]
    kk = kk * lax.rsqrt(_seg_sum(kk * kk, seg) + 1e-12)
    kd_sum = jnp.zeros_like(k)
    for d in range(2):
        decay = jnp.exp(-DECAY_SCALE * jax.nn.sigmoid(raw[:, d * LANES:(d + 1) * LANES]))
        iclr = jax.nn.sigmoid(raw[:, (2 + d) * LANES:(3 + d) * LANES])
        kd = k * (1.0 + (iclr - 1.0) * ka_ref[...])
        w_o[d, 0] = decay.T
        kd_o[d, 0] = kd.T
        bd_o[d, 0] = (kk * iclr).T
        kd_sum = kd_sum + kd
    r_o[0] = r.T
    v_o[0] = v.T
    a_o[0] = (-kk).T
    bonus_o[0] = _seg_sum(r * (0.5 * kd_sum) * rk_ref[...], seg) * v


def pre_scan(proj, mu, up_w, up_b, k_k, k_a, r_k, grid_shift):
    b, t, _ = proj.shape
    up_hi, up_lo = _split2(up_w)
    if not grid_shift:
        assert t == ROW_CHUNK, "sequence shift handles one chunk per sequence"
    assert t % ROW_CHUNK == 0
    col = lambda off: pl.BlockSpec((1, t, LANES), lambda i, j, c: (i, 0, off + j))
    vec = lambda off: pl.BlockSpec((1, LANES), lambda i, j, c: (0, off + j))
    out_t = pl.BlockSpec((1, LANES, ROW_CHUNK), lambda i, j, c: (i, j, c))
    out_t2 = pl.BlockSpec((2, 1, LANES, ROW_CHUNK), lambda i, j, c: (0, i, j, c))
    s_t = jax.ShapeDtypeStruct((b, RWKV_W, t), F32)
    s_t2 = jax.ShapeDtypeStruct((2, b, RWKV_W, t), F32)
    return pl.pallas_call(
        functools.partial(_prescan_kernel, t_len=t, grid_shift=grid_shift),
        grid=(b, HEAD_TILES, t // ROW_CHUNK),
        in_specs=[col(0), col(HEAD_TILES), col(2 * HEAD_TILES),
                  pl.BlockSpec((1, t, LORA_BLK), lambda i, j, c: (i, 0, LORA_OFF // LORA_BLK)),
                  vec(0), vec(HEAD_TILES), vec(2 * HEAD_TILES),
                  pl.BlockSpec((1, LORA_BLK), lambda i, j, c: (0, LORA_OFF // LORA_BLK)),
                  pl.BlockSpec((1, LORA_BLK, 4 * LANES), lambda i, j, c: (j, 0, 0)),
                  pl.BlockSpec((1, LORA_BLK, 4 * LANES), lambda i, j, c: (j, 0, 0)),
                  pl.BlockSpec((1, 1, 4 * LANES), lambda i, j, c: (j, 0, 0)),
                  vec(0), vec(0), vec(0)],
        out_specs=[out_t, out_t, out_t,
                   pl.BlockSpec((1, ROW_CHUNK, LANES), lambda i, j, c: (i, c, j)),
                   out_t2, out_t2, out_t2],
        out_shape=[s_t, s_t, s_t, jax.ShapeDtypeStruct((b, t, RWKV_W), F32), s_t2, s_t2, s_t2],
        compiler_params=_params("arbitrary", "arbitrary", "arbitrary"),
        name="pre_scan",
    )(proj, proj, proj, proj, mu, mu, mu, mu, up_hi, up_lo, up_b,
      k_k.reshape(1, RWKV_W), k_a.reshape(1, RWKV_W), r_k.reshape(1, RWKV_W))


def _lora_up_tables(w_decay_up, w_decay0, w_iclr_up, w_iclr0):
    mats = [w_decay_up[0], w_decay_up[1], w_iclr_up[0], w_iclr_up[1]]
    bias = [w_decay0[0], w_decay0[1], w_iclr0[0], w_iclr0[1]]
    up = jnp.zeros((HEAD_TILES, LORA_BLK, 4 * LANES), F32)
    for q, m in enumerate(mats):
        blk = m.astype(F32).reshape(LORA_R, HEAD_TILES, LANES).transpose(1, 0, 2)
        up = up.at[:, q * LORA_R:(q + 1) * LORA_R, q * LANES:(q + 1) * LANES].set(blk)
    b = jnp.stack([v.astype(F32).reshape(HEAD_TILES, LANES) for v in bias], axis=1)
    return up, b.reshape(HEAD_TILES, 1, 4 * LANES)


def _scan_chunk(ins, y_ref, yb_ref, s_scr, wr_scr, *, tc, backward=None):
    mixed = yb_ref is not None
    if mixed:
        fwd_lane = lax.broadcasted_iota(jnp.int32, (1, LANES), 1) < LANES // 2
        tt = lambda t: t

        def view(i):
            fw, bw = ins[i], ins[6 + i]
            full = lambda t: jnp.where(fwd_lane, fw[t], bw[tc - 1 - t])
            row = lambda t, k: jnp.where(fwd_lane, fw[t, k:k + 1, :], bw[tc - 1 - t, k:k + 1, :])
            return full, row
    else:
        tt = lambda t: jnp.where(backward, tc - 1 - t, t)

        def view(i):
            ref = ins[i]
            return (lambda t: ref[tt(t)]), (lambda t, k: ref[tt(t), k:k + 1, :])

    (w_full, w_row), (k_full, k_row), (b_full, b_row), (_, a_row), (r_full, _), (v_full, _) = (
        view(i) for i in range(6))

    wr_scr[0] = w_full(0) * r_full(0)
    sa = jnp.zeros((HEAD_N, LANES), F32)
    y0 = jnp.zeros((HEAD_N, LANES), F32)
    for k in range(HEAD_N):
        s = s_scr[k]
        sa = sa + s * a_row(0, k)
        y0 = y0 + s * wr_scr[0, k:k + 1, :]

    def step(t, carry):
        sa, y0 = carry
        r_t = r_full(t)
        v_t = v_full(t)
        br = jnp.sum(b_full(t) * r_t, axis=0, keepdims=True)
        kr = jnp.sum(k_full(t) * r_t, axis=0, keepdims=True)
        y = y0 + sa * br + v_t * kr
        y_ref[tt(t)] = y
        if mixed:
            yb_ref[tc - 1 - t] = y
        tn = jnp.minimum(t + 1, tc - 1)
        slot = (t + 1) % 2
        wr_scr[slot] = w_full(tn) * r_full(tn)
        sa_n = jnp.zeros((HEAD_N, LANES), F32)
        y0_n = jnp.zeros((HEAD_N, LANES), F32)
        for k in range(HEAD_N):
            s = s_scr[k] * w_row(t, k) + sa * b_row(t, k) + v_t * k_row(t, k)
            s_scr[k] = s
            sa_n = sa_n + s * a_row(tn, k)
            y0_n = y0_n + s * wr_scr[slot, k:k + 1, :]
        return sa_n, y0_n

    lax.fori_loop(0, tc, step, (sa, y0))


def _scan_kernel(*refs, tc, mixed):
    n_in = 12 if mixed else 6
    ins, rest = refs[:n_in], refs[n_in:]
    if mixed:
        s0_ref, y_ref, yb_ref, sfin_ref, s_scr, wr_scr = rest
        backward = None
    else:
        s0_ref, y_ref, sfin_ref, s_scr, wr_scr = rest
        yb_ref = None
        backward = pl.program_id(0) >= pl.num_programs(0) // 2
    c = pl.program_id(1)

    @pl.when(c == 0)
    def _():
        s_scr[...] = s0_ref[0]

    _scan_chunk(ins, y_ref, yb_ref, s_scr, wr_scr, tc=tc, backward=backward)

    @pl.when(c == pl.num_programs(1) - 1)
    def _():
        sfin_ref[0] = s_scr[...]


def wkv_scan(ops, s0, tc=32):
    t, _, lanes = ops[0].shape
    groups = lanes // LANES
    nc = t // tc
    mixed = groups == 1
    st = pl.BlockSpec((1, HEAD_N, HEAD_N, LANES), lambda g, c: (g, 0, 0, 0))
    y_shape = jax.ShapeDtypeStruct((t, HEAD_N, lanes), F32)
    if mixed:
        fwd = pl.BlockSpec((tc, HEAD_N, LANES), lambda g, c: (c, 0, g))
        bwd = pl.BlockSpec((tc, HEAD_N, LANES), lambda g, c: (nc - 1 - c, 0, g))
        in_specs, operands = [fwd] * 6 + [bwd] * 6 + [st], list(ops) + list(ops) + [s0]
        out_specs, out_shape = [fwd, bwd, st], [y_shape, y_shape]
    else:
        half_g = groups // 2
        tmap = lambda g, c: jnp.where(g >= half_g, nc - 1 - c, c)
        seq = pl.BlockSpec((tc, HEAD_N, LANES), lambda g, c: (tmap(g, c), 0, g))
        shared = pl.BlockSpec((tc, HEAD_N, LANES), lambda g, c: (tmap(g, c), 0, g % half_g))
        in_specs = [seq if o.shape[2] == lanes else shared for o in ops] + [st]
        operands = list(ops) + [s0]
        out_specs, out_shape = [seq, st], [y_shape]
    out_shape.append(jax.ShapeDtypeStruct((groups, HEAD_N, HEAD_N, LANES), F32))
    res = pl.pallas_call(
        functools.partial(_scan_kernel, tc=tc, mixed=mixed),
        grid=(groups, nc),
        in_specs=in_specs,
        out_specs=out_specs,
        out_shape=out_shape,
        scratch_shapes=[pltpu.VMEM((HEAD_N, HEAD_N, LANES), F32),
                        pltpu.VMEM((2, HEAD_N, LANES), F32)],
        compiler_params=_params("arbitrary", "arbitrary"),
        name="wkv_scan",
    )(*operands)
    half = lanes // 2
    if mixed:
        y_f, y_b, s_fin = res
        return y_f[..., :half] + y_b[..., half:], s_fin
    y, s_fin = res
    return y[..., :half] + y[..., half:], s_fin


K_BLOCK = 8


def _relayout_kernel(*refs, n_ops, rb, dup):
    ins, outs, scr = refs[:n_ops], refs[n_ops:2 * n_ops], refs[2 * n_ops]
    for x_ref, o_ref in zip(ins, outs):
        x2 = x_ref.reshape(rb * K_BLOCK, LANES)
        for r0 in range(0, rb, LANES):
            n_rows = min(LANES, rb - r0)
            for kk in range(K_BLOCK):
                m = x2[pl.ds(r0 * K_BLOCK + kk, n_rows, stride=K_BLOCK), :]
                if dup:
                    m = jnp.concatenate([m, m], axis=0)
                scr[pl.ds(kk, LANES, stride=K_BLOCK), :] = m.T
            o_ref[:, :, r0:r0 + LANES] = scr[...].reshape(LANES, K_BLOCK, LANES)


def relayout_to_scan(xs, dup=False):
    rows, _, t = xs[0].shape
    if dup:
        assert 2 * rows == LANES
    else:
        assert rows % LANES == 0
    rb = min(rows, 512)
    lanes_blk = 2 * rb if dup else rb
    n = len(xs)
    return pl.pallas_call(
        functools.partial(_relayout_kernel, n_ops=n, rb=rb, dup=dup),
        grid=(rows // rb, HEAD_N // K_BLOCK, t // LANES),
        in_specs=[pl.BlockSpec((rb, K_BLOCK, LANES), lambda r, k, c: (r, k, c))] * n,
        out_specs=[pl.BlockSpec((LANES, K_BLOCK, lanes_blk), lambda r, k, c: (c, k, r))] * n,
        out_shape=[jax.ShapeDtypeStruct((t, HEAD_N, lanes_blk * (rows // rb)), F32)] * n,
        scratch_shapes=[pltpu.VMEM((LANES * K_BLOCK, LANES), F32)],
        compiler_params=_params("arbitrary", "arbitrary", "arbitrary"),
        name="relayout_to_scan",
    )(*xs)


def _scan_rows(x):
    return x.reshape(-1, HEAD_N, x.shape[-1])


def _from_scan_layout(y, b):
    t = y.shape[0]
    return y.reshape(t, HEAD_N, b, N_HEADS).transpose(2, 0, 3, 1).reshape(b, t, RWKV_W)


def _postscan_kernel(y_ref, bonus_ref, g_ref, gw_ref, gb_ref, o_ref):
    seg = _seg_ones()
    for j in range(HEAD_TILES):
        cols = slice(j * LANES, (j + 1) * LANES)
        y = y_ref[0, :, cols]
        mu = _seg_sum(y, seg) * (1.0 / HEAD_N)
        d = y - mu
        var = _seg_sum(d * d, seg) * (1.0 / HEAD_N)
        yn = d * lax.rsqrt(var + GN_EPS) * gw_ref[:, cols] + gb_ref[:, cols]
        o_ref[0, :, cols] = ((yn + bonus_ref[0, :, cols]) * jax.nn.sigmoid(g_ref[0, :, cols])).astype(BF16)


def post_scan(y, bonus, proj, gn_w, gn_b):
    b, t, _ = y.shape
    ch = ROW_CHUNK
    blk = pl.BlockSpec((1, ch, RWKV_W), lambda i, c: (i, c, 0))
    vec = pl.BlockSpec((1, RWKV_W), lambda i, c: (0, 0))
    return pl.pallas_call(
        _postscan_kernel,
        grid=(b, t // ch),
        in_specs=[blk, blk,
                  pl.BlockSpec((1, ch, RWKV_W), lambda i, c: (i, c, G_OFF // RWKV_W)),
                  vec, vec],
        out_specs=blk,
        out_shape=jax.ShapeDtypeStruct((b, t, RWKV_W), BF16),
        compiler_params=_params("arbitrary", "arbitrary"),
        name="post_scan",
    )(y, bonus, proj, gn_w.reshape(1, RWKV_W), gn_b.reshape(1, RWKV_W))


def _fnet_kernel(f_ref, cc_ref, ct_ref, o_ref, xcs_ref, *, t_len, scale):
    @pl.when(pl.program_id(2) == 0)
    def _():
        xc = jnp.dot(f_ref[0].astype(BF16), cc_ref[...], preferred_element_type=F32)
        xcs_ref[0:t_len, :] = xc[:, :FOURIER_GW].astype(BF16)
        xcs_ref[t_len:2 * t_len, :] = xc[:, FOURIER_GW:].astype(BF16)

    out = jnp.dot(ct_ref[...], xcs_ref[...], preferred_element_type=F32)
    o_ref[0] = (out * scale).astype(BF16)


def _fnet_short_kernel(f_ref, cc_ref, ct_ref, o_ref, *, t_len, scale):
    gw = FOURIER_GW
    for g in range(N_FOURIER_GROUPS):
        cols = slice(g * gw, (g + 1) * gw)
        xc = jnp.dot(f_ref[0, :, cols].astype(BF16), cc_ref[...], preferred_element_type=F32)
        out = (jnp.dot(ct_ref[:, :t_len], xc[:, :gw].astype(BF16), preferred_element_type=F32)
               + jnp.dot(ct_ref[:, t_len:], xc[:, gw:].astype(BF16), preferred_element_type=F32))
        o_ref[0, :, cols] = (out * scale).astype(BF16)


def _dft_tables(n):
    j = lax.broadcasted_iota(jnp.int32, (n, n), 0)
    k = lax.broadcasted_iota(jnp.int32, (n, n), 1)
    ang = ((j * k) % n).astype(F32) * (2.0 * math.pi / n)
    return jnp.cos(ang), jnp.sin(ang)


def fnet_mix(proj):
    b, t, _ = proj.shape
    cc, sc = _dft_tables(FOURIER_GW)
    ct, st = _dft_tables(t)
    cc2 = jnp.concatenate([cc, sc], axis=1).astype(BF16)
    ct2 = jnp.concatenate([ct, -st], axis=1).astype(BF16)
    scale = 1.0 / math.sqrt(t * FOURIER_GW)
    if t <= 512:
        blk = pl.BlockSpec((1, t, RWKV_W), lambda i: (i, 0, 0))
        return pl.pallas_call(
            functools.partial(_fnet_short_kernel, t_len=t, scale=scale),
            grid=(b,),
            in_specs=[pl.BlockSpec((1, t, RWKV_W), lambda i: (i, 0, F_OFF // RWKV_W)),
                      pl.BlockSpec((FOURIER_GW, 2 * FOURIER_GW), lambda i: (0, 0)),
                      pl.BlockSpec((t, 2 * t), lambda i: (0, 0))],
            out_specs=blk,
            out_shape=jax.ShapeDtypeStruct((b, t, RWKV_W), BF16),
            compiler_params=_params("arbitrary"),
            name="fnet_mix",
        )(proj, cc2, ct2)
    tq = min(512, t)
    return pl.pallas_call(
        functools.partial(_fnet_kernel, t_len=t, scale=scale),
        grid=(b, N_FOURIER_GROUPS, t // tq),
        in_specs=[pl.BlockSpec((1, t, FOURIER_GW), lambda i, g, q: (i, 0, F_OFF // FOURIER_GW + g)),
                  pl.BlockSpec((FOURIER_GW, 2 * FOURIER_GW), lambda i, g, q: (0, 0)),
                  pl.BlockSpec((tq, 2 * t), lambda i, g, q: (q, 0))],
        out_specs=pl.BlockSpec((1, tq, FOURIER_GW), lambda i, g, q: (i, q, g)),
        out_shape=jax.ShapeDtypeStruct((b, t, RWKV_W), BF16),
        scratch_shapes=[pltpu.VMEM((2 * t, FOURIER_GW), BF16)],
        compiler_params=_params("arbitrary", "arbitrary", "arbitrary"),
        name="fnet_mix",
    )(proj, cc2, ct2)


def _outproj_kernel(yr_ref, yf_ref, w_ref, x_ref, mod_ref, o_ref):
    m = jnp.dot(yr_ref[...], w_ref[0:RWKV_W, :], preferred_element_type=F32)
    m = m + jnp.dot(yf_ref[...], w_ref[RWKV_W:2 * RWKV_W, :], preferred_element_type=F32)
    o_ref[...] = x_ref[...] + mod_ref[0, 2:3, :] * m


def out_proj(yr, yf, w_bf, x2d, mod, rows_per_mod, mod_base):
    m, d = x2d.shape
    tm, tn = min(1024, rows_per_mod), 512
    row_of = _mod_row_map(tm, rows_per_mod, mod_base)
    return pl.pallas_call(
        _outproj_kernel,
        grid=(m // tm, d // tn),
        in_specs=[pl.BlockSpec((tm, RWKV_W), lambda i, j: (i, 0)),
                  pl.BlockSpec((tm, RWKV_W), lambda i, j: (i, 0)),
                  pl.BlockSpec((2 * RWKV_W, tn), lambda i, j: (0, j)),
                  pl.BlockSpec((tm, tn), lambda i, j: (i, j)),
                  pl.BlockSpec((1, N_MOD, tn), lambda i, j: (row_of(i), 0, j))],
        out_specs=pl.BlockSpec((tm, tn), lambda i, j: (i, j)),
        out_shape=jax.ShapeDtypeStruct((m, d), F32),
        compiler_params=_params("arbitrary", "arbitrary"),
        name="out_proj",
    )(yr, yf, w_bf, x2d, mod)


def _norm2_router_kernel(x_ref, mod_ref, g_ref, rw_ref, h_ref, aff_ref):
    h = _rms_modulate(x_ref[...], g_ref[...], mod_ref[0, 4:5, :], mod_ref[0, 3:4, :])
    h_ref[...] = h
    logits = jnp.dot(h, rw_ref[...], precision=HIGHEST, preferred_element_type=F32)
    z = logits - jnp.max(logits, axis=-1, keepdims=True)
    e = jnp.exp(z)
    aff_ref[...] = e / jnp.sum(e, axis=-1, keepdims=True)


def norm2_router(x2d, mod, norm_g, router_w, rows_per_mod, mod_base):
    m, d = x2d.shape
    tm = ROW_CHUNK
    row_of = _mod_row_map(tm, rows_per_mod, mod_base)
    return pl.pallas_call(
        _norm2_router_kernel,
        grid=(m // tm,),
        in_specs=[pl.BlockSpec((tm, d), lambda i: (i, 0)),
                  pl.BlockSpec((1, N_MOD, d), lambda i: (row_of(i), 0, 0)),
                  pl.BlockSpec((1, d), lambda i: (0, 0)),
                  pl.BlockSpec((d, N_EXPERTS), lambda i: (0, 0))],
        out_specs=[pl.BlockSpec((tm, d), lambda i: (i, 0)),
                   pl.BlockSpec((tm, N_EXPERTS), lambda i: (i, 0))],
        out_shape=[jax.ShapeDtypeStruct((m, d), F32),
                   jax.ShapeDtypeStruct((m, N_EXPERTS), F32)],
        compiler_params=_params("arbitrary"),
        name="norm2_router",
    )(x2d, mod, norm_g.reshape(1, d), router_w)


GATHER_ROWS = 256
PAIR_CHUNK = LANES
COMBINE_ROWS = 512


def _issue_row_gather(idx_ref, base, n_rows, src_hbm, dst, sem):
    def body(r, carry):
        row = idx_ref[base + r]
        pltpu.make_async_copy(src_hbm.at[pl.ds(row, 1), :], dst.at[pl.ds(r, 1), :], sem).start()
        return carry
    lax.fori_loop(0, n_rows, body, 0, unroll=8)


def _wait_row_gather(n_rows, src_hbm, dst, sem):
    pltpu.make_async_copy(src_hbm.at[pl.ds(0, n_rows), :], dst, sem).wait()


def _gather_kernel(idx_ref, h_hbm, o_ref, buf, sem):
    s = pl.program_id(0)
    slot = s % 2

    @pl.when(s == 0)
    def _():
        _issue_row_gather(idx_ref, 0, GATHER_ROWS, h_hbm, buf.at[0], sem.at[0])

    @pl.when(s + 1 < pl.num_programs(0))
    def _():
        _issue_row_gather(idx_ref, (s + 1) * GATHER_ROWS, GATHER_ROWS, h_hbm,
                          buf.at[1 - slot], sem.at[1 - slot])

    _wait_row_gather(GATHER_ROWS, h_hbm, buf.at[slot], sem.at[slot])
    o_ref[...] = buf[slot].astype(BF16)


def gather_tokens(h, idx_flat):
    n, d = h.shape
    rows = idx_flat.shape[0]
    return pl.pallas_call(
        _gather_kernel,
        grid_spec=pltpu.PrefetchScalarGridSpec(
            num_scalar_prefetch=1,
            grid=(rows // GATHER_ROWS,),
            in_specs=[pl.BlockSpec(memory_space=pl.ANY)],
            out_specs=pl.BlockSpec((GATHER_ROWS, d), lambda s, idx: (s, 0)),
            scratch_shapes=[pltpu.VMEM((2, GATHER_ROWS, d), F32), pltpu.SemaphoreType.DMA((2,))]),
        out_shape=jax.ShapeDtypeStruct((rows, d), BF16),
        compiler_params=_params("arbitrary"),
        name="gather_tokens",
    )(idx_flat, h)


def _ffn_accumulate(xe_ref, w1_ref, w3_ref, w2_ref, o_ref, *, tf, ff):
    f = pl.program_id(1)
    x = xe_ref[0]
    h1 = jnp.dot(x, w1_ref[0].astype(BF16), preferred_element_type=F32)
    h3 = jnp.dot(x, w3_ref[0].astype(BF16), preferred_element_type=F32)
    hid = h1 * jax.nn.sigmoid(h1) * h3
    valid = ff - f * tf
    hid = jnp.where(lax.broadcasted_iota(jnp.int32, (1, tf), 1) < valid, hid, 0.0)
    w2 = jnp.where(lax.broadcasted_iota(jnp.int32, (tf, 1), 0) < valid, w2_ref[0], 0.0)
    o_ref[0] += jnp.dot(hid.astype(BF16), w2.astype(BF16), preferred_element_type=F32)


def _expert_kernel(xe_ref, w1_ref, w3_ref, w2_ref, gate_ref, o_ref, *, tf, ff):
    f = pl.program_id(1)

    @pl.when(f == 0)
    def _():
        o_ref[...] = jnp.zeros_like(o_ref)

    _ffn_accumulate(xe_ref, w1_ref, w3_ref, w2_ref, o_ref, tf=tf, ff=ff)

    @pl.when(f == pl.num_programs(1) - 1)
    def _():
        o_ref[0] = o_ref[0] * gate_ref[0]


def expert_ffn(xe, w1, w3, w2, gate):
    e, cap, d = xe.shape
    ff = w1.shape[2]
    tf = 512
    nf = pl.cdiv(ff, tf)
    return pl.pallas_call(
        functools.partial(_expert_kernel, tf=tf, ff=ff),
        grid=(e, nf),
        in_specs=[pl.BlockSpec((1, cap, d), lambda i, f: (i, 0, 0), pipeline_mode=pl.Buffered(1)),
                  pl.BlockSpec((1, d, tf), lambda i, f: (i, 0, f)),
                  pl.BlockSpec((1, d, tf), lambda i, f: (i, 0, f)),
                  pl.BlockSpec((1, tf, d), lambda i, f: (i, f, 0)),
                  pl.BlockSpec((1, cap, 1), lambda i, f: (i, 0, 0))],
        out_specs=pl.BlockSpec((1, cap, d), lambda i, f: (i, 0, 0)),
        out_shape=jax.ShapeDtypeStruct((e, cap, d), F32),
        compiler_params=_params("arbitrary", "arbitrary"),
        name="expert_ffn",
    )(xe, w1, w3, w2, gate.reshape(e, cap, 1))


def _combine_final_kernel(rows_ref, ptr_ref, tok_ref, ye_hbm, x_ref, mod_ref, g_ref, o_ref,
                          buf, sem, acc_ref, *, tm):
    i = pl.program_id(0)
    p0 = ptr_ref[i]
    p1 = ptr_ref[i + 1]
    c0 = p0 // PAIR_CHUNK
    c1 = jnp.where(p1 > p0, (p1 - 1) // PAIR_CHUNK + 1, c0)
    acc_ref[...] = jnp.zeros_like(acc_ref)

    @pl.when(c1 > c0)
    def _():
        _issue_row_gather(rows_ref, c0 * PAIR_CHUNK, PAIR_CHUNK, ye_hbm, buf.at[0], sem.at[0])

    def chunk(c, carry):
        slot = (c - c0) % 2

        @pl.when(c + 1 < c1)
        def _():
            _issue_row_gather(rows_ref, (c + 1) * PAIR_CHUNK, PAIR_CHUNK, ye_hbm,
                              buf.at[1 - slot], sem.at[1 - slot])

        _wait_row_gather(PAIR_CHUNK, ye_hbm, buf.at[slot], sem.at[slot])
        local = tok_ref[pl.ds(c, 1), :] - i * tm
        onehot = (lax.broadcasted_iota(jnp.int32, (tm, PAIR_CHUNK), 0) == local).astype(BF16)
        hi, lo = _split2(buf[slot])
        acc_ref[...] += (jnp.dot(onehot, hi, preferred_element_type=F32)
                         + jnp.dot(onehot, lo, preferred_element_type=F32))
        return carry

    lax.fori_loop(c0, c1, chunk, 0)
    x = x_ref[...] + mod_ref[0, 5:6, :] * acc_ref[...]
    ms = jnp.mean(x * x, axis=-1, keepdims=True)
    o_ref[...] = x * lax.rsqrt(ms + NORM_EPS) * g_ref[...]


def combine_final(x2d, ye_flat, pair_rows, pair_toks, tile_ptr, mod, final_g, rows_per_mod, mod_base):
    m, d = x2d.shape
    tm = COMBINE_ROWS
    n_pairs = pair_rows.shape[0]
    row_of = _mod_row_map(tm, rows_per_mod, mod_base)
    return pl.pallas_call(
        functools.partial(_combine_final_kernel, tm=tm),
        grid_spec=pltpu.PrefetchScalarGridSpec(
            num_scalar_prefetch=2,
            grid=(m // tm,),
            in_specs=[pl.BlockSpec((n_pairs // PAIR_CHUNK, PAIR_CHUNK), lambda i, r, p: (0, 0)),
                      pl.BlockSpec(memory_space=pl.ANY),
                      pl.BlockSpec((tm, d), lambda i, r, p: (i, 0)),
                      pl.BlockSpec((1, N_MOD, d), lambda i, r, p: (row_of(i), 0, 0)),
                      pl.BlockSpec((1, d), lambda i, r, p: (0, 0))],
            out_specs=pl.BlockSpec((tm, d), lambda i, r, p: (i, 0)),
            scratch_shapes=[pltpu.VMEM((2, PAIR_CHUNK, d), F32), pltpu.SemaphoreType.DMA((2,)),
                            pltpu.VMEM((tm, d), F32)]),
        out_shape=jax.ShapeDtypeStruct((m, d), F32),
        compiler_params=_params("arbitrary"),
        name="combine_final",
    )(pair_rows, tile_ptr, pair_toks.reshape(n_pairs // PAIR_CHUNK, PAIR_CHUNK), ye_flat,
      x2d, mod, final_g.reshape(1, d))


def _prepare_in_proj(w_in, shift_mu):
    d = w_in.shape[0]
    rkv, lora, gf = w_in[:, :G_OFF], w_in[:, G_OFF:SHIFT_W], w_in[:, SHIFT_W:]
    w_p = jnp.concatenate([rkv, gf, lora, jnp.zeros((d, LANES), w_in.dtype)], axis=1).astype(BF16)
    mu = jnp.concatenate([shift_mu[:G_OFF], jnp.zeros((2 * RWKV_W,), F32), shift_mu[G_OFF:],
                          jnp.zeros((LANES,), F32)])
    return w_p, mu.reshape(1, IN_PAD_W)


def _mix_inputs(x, mod, rows_per_mod, mod_base, grid_shift, p):
    b, t, d = x.shape
    x2d = x.reshape(b * t, d)
    proj = in_proj(x2d, mod, p["norm1_g"], p["w_in_p"], rows_per_mod, mod_base)
    proj = proj.reshape(b, t, IN_PAD_W)
    r, v, a, bonus, w, kd, bd = pre_scan(proj, p["mu"], p["up_w"], p["up_b"],
                                         p["k_k"], p["k_a"], p["r_k"], grid_shift)
    per_dir = relayout_to_scan([_scan_rows(w), _scan_rows(kd), _scan_rows(bd)])
    shared = relayout_to_scan([_scan_rows(a), _scan_rows(r), _scan_rows(v)],
                              dup=2 * b * N_HEADS == LANES)
    return {"x2d": x2d, "proj": proj, "bonus": bonus, "ops": list(per_dir) + list(shared),
            "mod": (mod, rows_per_mod, mod_base), "shape": (b, t, d)}


def _route(st, y, p):
    b, t, d = st["shape"]
    mod_args = st["mod"]
    yr = post_scan(_from_scan_layout(y, b), st["bonus"], st["proj"], p["gn_w"], p["gn_b"])
    yf = fnet_mix(st["proj"])
    x1 = out_proj(yr.reshape(b * t, RWKV_W), yf.reshape(b * t, RWKV_W), p["w_out_bf"],
                  st["x2d"], *mod_args)
    h2, aff = norm2_router(x1, mod_args[0], p["norm2_g"], p["router_w"], *mod_args[1:])
    n = b * t
    cap = n * CAPACITY_FACTOR // N_EXPERTS
    gate, idx = lax.top_k(aff.T, cap)
    idx_flat = idx.reshape(-1).astype(jnp.int32)
    xe = gather_tokens(h2, idx_flat).reshape(N_EXPERTS, cap, d)
    return {"x1": x1, "xe": xe, "gate": gate, "idx_flat": idx_flat, "mod": mod_args,
            "shape": (b, t, d)}


def _combine(rt, ye, p):
    b, t, d = rt["shape"]
    n = b * t
    idx_flat = rt["idx_flat"]
    pair_toks, pair_rows = lax.sort_key_val(idx_flat, jnp.arange(idx_flat.shape[0], dtype=jnp.int32))
    tile_ptr = jnp.searchsorted(pair_toks, jnp.arange(0, n + 1, COMBINE_ROWS, dtype=jnp.int32),
                                side="left").astype(jnp.int32)
    mod, rows_per_mod, mod_base = rt["mod"]
    out = combine_final(rt["x1"], ye.reshape(-1, d), pair_rows, pair_toks, tile_ptr, mod,
                        p["final_norm_g"], rows_per_mod, mod_base)
    return out.reshape(b, t, d)


def _experts(p):
    return p["exp_w1"], p["exp_w3"], p["exp_w2"]


def _layer(x, mod, rows_per_mod, mod_base, s0, grid_shift, p):
    st = _mix_inputs(x, mod, rows_per_mod, mod_base, grid_shift, p)
    y, s_fin = wkv_scan(st["ops"], s0)
    rt = _route(st, y, p)
    ye = expert_ffn(rt["xe"], *_experts(p), rt["gate"])
    return _combine(rt, ye, p), s_fin


def kernel(x_prompt, x_sample, state_wkv, c, c_ctx, ada_w, ada_b, norm1_g, norm2_g, w_in, shift_mu,
           w_decay_up, w_decay0, w_iclr_up, w_iclr0, k_k, k_a, r_k, gn_w, gn_b, w_out, router_w,
           exp_w1, exp_w3, exp_w2, final_norm_g):
    depth = ada_w.shape[0]
    assert depth == 1, "single-layer trunk"
    l = 0
    bc, tc_len, d = x_prompt.shape
    bl, tl, _ = x_sample.shape
    cvecs = jnp.concatenate([c_ctx[None], c], axis=0)
    mod = ada_mod(cvecs, ada_w[l], ada_b[l]).reshape(1 + bl, N_MOD, d)
    up_w, up_b = _lora_up_tables(w_decay_up[l], w_decay0[l], w_iclr_up[l], w_iclr0[l])
    w_in_p, mu = _prepare_in_proj(w_in[l], shift_mu[l])
    p = {
        "norm1_g": norm1_g[l], "norm2_g": norm2_g[l], "final_norm_g": final_norm_g,
        "w_in_p": w_in_p, "mu": mu, "up_w": up_w, "up_b": up_b,
        "k_k": k_k[l], "k_a": k_a[l], "r_k": r_k[l], "gn_w": gn_w[l], "gn_b": gn_b[l],
        "w_out_bf": w_out[l].astype(BF16), "router_w": router_w[l],
        "exp_w1": exp_w1[l], "exp_w3": exp_w3[l], "exp_w2": exp_w2[l],
    }
    groups_c = 2 * bc * N_HEADS // LANES
    s0_ctx = jnp.zeros((groups_c, HEAD_N, HEAD_N, LANES), F32)
    ctx = _mix_inputs(x_prompt, mod, bc * tc_len, 0, False, p)
    y_ctx, s_ctx = wkv_scan(ctx["ops"], s0_ctx)
    ctx_rt = _route(ctx, y_ctx, p)
    s_ctx = s_ctx.transpose(1, 2, 0, 3).reshape(HEAD_N, HEAD_N, 2, bc, N_HEADS)
    new_state = s_ctx.transpose(3, 2, 4, 1, 0)[:, None]
    groups_l = 2 * bl * N_HEADS // LANES
    s0_lat = state_wkv[:, l].astype(F32).transpose(4, 3, 1, 0, 2)
    s0_lat = s0_lat.reshape(HEAD_N, HEAD_N, groups_l, LANES).transpose(2, 0, 1, 3)
    lat = _mix_inputs(x_sample, mod, tl, 1, True, p)
    ye_ctx = expert_ffn(ctx_rt["xe"], *_experts(p), ctx_rt["gate"])
    y_lat, _ = wkv_scan(lat["ops"], s0_lat)
    lat_rt = _route(lat, y_lat, p)
    ye_lat = expert_ffn(lat_rt["xe"], *_experts(p), lat_rt["gate"])
    return (_combine(ctx_rt, ye_ctx, p), _combine(lat_rt, ye_lat, p), new_state)
```

```python
import functools
import math

import jax
import jax.numpy as jnp
from jax import lax
from jax.experimental import pallas as pl
from jax.experimental.pallas import tpu as pltpu

F32 = jnp.float32
BF16 = jnp.bfloat16
HIGHEST = lax.Precision.HIGHEST

D_MODEL = 2048
RWKV_W = 1024
HEAD_N = 64
N_HEADS = 16
LORA_R = 96
N_EXPERTS = 16
EXPERT_FF = 5504
CAPACITY_FACTOR = 2
N_MOD = 6
NORM_EPS = 1e-6
GN_EPS = 64e-5
FOURIER_GW = 256
N_FOURIER_GROUPS = 4

LANES = 128
HEAD_TILES = RWKV_W // LANES
SHIFT_W = 3 * RWKV_W + 4 * LORA_R
LORA_W = 4 * LORA_R
LORA_BLK = LORA_W + LANES
G_OFF = 3 * RWKV_W
F_OFF = G_OFF + RWKV_W
LORA_OFF = F_OFF + RWKV_W
IN_PAD_W = LORA_OFF + LORA_BLK
ROW_CHUNK = 256
VMEM_LIMIT = 56 * 1024 * 1024
DECAY_SCALE = math.exp(-0.5)


def _params(*sem):
    return pltpu.CompilerParams(dimension_semantics=sem, vmem_limit_bytes=VMEM_LIMIT)


def _seg_ones():
    r = lax.broadcasted_iota(jnp.int32, (LANES, LANES), 0) // HEAD_N
    c = lax.broadcasted_iota(jnp.int32, (LANES, LANES), 1) // HEAD_N
    return (r == c).astype(BF16)


def _split2(x):
    hi = x.astype(BF16)
    return hi, (x - hi.astype(F32)).astype(BF16)


def _split3(x):
    hi = x.astype(BF16)
    rem = x - hi.astype(F32)
    mid = rem.astype(BF16)
    return hi, mid, (rem - mid.astype(F32)).astype(BF16)


def _seg_sum(x, seg):
    return sum(jnp.dot(p, seg, preferred_element_type=F32) for p in _split3(x))


def _ada_kernel(c_ref, w_ref, b_ref, o_ref):
    c = c_ref[...]
    s = c * jax.nn.sigmoid(c)
    o_ref[...] = jnp.dot(s.astype(BF16), w_ref[...].astype(BF16),
                         preferred_element_type=F32) + b_ref[...]


def ada_mod(cvecs, ada_w, ada_b):
    rows, d = cvecs.shape
    n = ada_w.shape[1]
    tn = 1024
    return pl.pallas_call(
        _ada_kernel,
        grid=(n // tn,),
        in_specs=[pl.BlockSpec((rows, d), lambda j: (0, 0)),
                  pl.BlockSpec((d, tn), lambda j: (0, j)),
                  pl.BlockSpec((1, tn), lambda j: (0, j))],
        out_specs=pl.BlockSpec((rows, tn), lambda j: (0, j)),
        out_shape=jax.ShapeDtypeStruct((rows, n), F32),
        compiler_params=_params("arbitrary"),
        name="ada_mod",
    )(cvecs, ada_w, ada_b.reshape(1, n))


def _mod_row_map(tm, rows_per_mod, mod_base):
    return lambda i: mod_base + (i * tm) // rows_per_mod


def _rms_modulate(x, g, scale, shift):
    ms = jnp.mean(x * x, axis=-1, keepdims=True)
    return x * lax.rsqrt(ms + NORM_EPS) * g * (1.0 + scale) + shift


def _inproj_kernel(x_ref, mod_ref, g_ref, w_ref, o_ref, h_ref, *, tm):
    @pl.when(pl.program_id(1) == 0)
    def _():
        def body(c, carry):
            rows = pl.ds(pl.multiple_of(c * ROW_CHUNK, ROW_CHUNK), ROW_CHUNK)
            h = _rms_modulate(x_ref[rows, :], g_ref[...], mod_ref[0, 1:2, :], mod_ref[0, 0:1, :])
            h_ref[rows, :] = h.astype(BF16)
            return carry
        lax.fori_loop(0, tm // ROW_CHUNK, body, 0)

    o_ref[...] = jnp.dot(h_ref[...], w_ref[...], preferred_element_type=F32)


def in_proj(x2d, mod, norm_g, w_p, rows_per_mod, mod_base):
    m, d = x2d.shape
    n = w_p.shape[1]
    tm, tn = min(1024, rows_per_mod), 512
    row_of = _mod_row_map(tm, rows_per_mod, mod_base)
    return pl.pallas_call(
        functools.partial(_inproj_kernel, tm=tm),
        grid=(m // tm, n // tn),
        in_specs=[pl.BlockSpec((tm, d), lambda i, j: (i, 0)),
                  pl.BlockSpec((1, N_MOD, d), lambda i, j: (row_of(i), 0, 0)),
                  pl.BlockSpec((1, d), lambda i, j: (0, 0)),
                  pl.BlockSpec((d, tn), lambda i, j: (0, j))],
        out_specs=pl.BlockSpec((tm, tn), lambda i, j: (i, j)),
        out_shape=jax.ShapeDtypeStruct((m, n), F32),
        scratch_shapes=[pltpu.VMEM((tm, d), BF16)],
        compiler_params=_params("arbitrary", "arbitrary"),
        name="in_proj",
    )(x2d, mod, norm_g.reshape(1, d), w_p)


def _shifted(ref, mu, c, n_chunks, t_len, grid_shift):
    ch = ROW_CHUNK
    base = pl.multiple_of(c * ch, ch)
    cur = ref[0, pl.ds(base, ch), :]
    row = lax.broadcasted_iota(jnp.int32, (ch, 1), 0)
    before = pltpu.roll(cur, 1, 0)
    after = pltpu.roll(cur, ch - 1, 0)
    if not grid_shift:
        prev = jnp.where(row == 0, 0.0, before)
        nxt = jnp.where(row == ch - 1, 0.0, after)
        mixed = 0.5 * (prev + nxt)
    else:
        gw = 64
        left = jnp.where(row % gw == 0, 0.0, before)
        right = jnp.where(row % gw == gw - 1, 0.0, after)
        up_start = pl.multiple_of(jnp.maximum(base - gw, 0), gw)
        dn_start = pl.multiple_of(jnp.minimum(base + ch, t_len - gw), gw)
        up_halo = jnp.where(c > 0, ref[0, pl.ds(up_start, gw), :], 0.0)
        dn_halo = jnp.where(c < n_chunks - 1, ref[0, pl.ds(dn_start, gw), :], 0.0)
        up = jnp.concatenate([up_halo, cur[: ch - gw]], axis=0)
        down = jnp.concatenate([cur[gw:], dn_halo], axis=0)
        mixed = 0.25 * (up + down + left + right)
    return cur + mu * (mixed - cur)


def _prescan_kernel(r_ref, k_ref, v_ref, lora_ref, mur_ref, muk_ref, muv_ref, mul_ref,
                    uph_ref, upl_ref, bias_ref, kk_ref, ka_ref, rk_ref,
                    r_o, v_o, a_o, bonus_o, w_o, kd_o, bd_o, lh_scr, ll_scr, *, t_len, grid_shift):
    n_chunks = t_len // ROW_CHUNK
    seg = _seg_ones()
    c = pl.program_id(2)
    rows = pl.ds(pl.multiple_of(c * ROW_CHUNK, ROW_CHUNK), ROW_CHUNK)
    sh = functools.partial(_shifted, c=c, n_chunks=n_chunks, t_len=t_len, grid_shift=grid_shift)

    @pl.when(pl.program_id(1) == 0)
    def _():
        lane = lax.broadcasted_iota(jnp.int32, (1, LORA_BLK), 1)
        lora = sh(lora_ref, mul_ref[...])
        lora = jnp.where(lane < 2 * LORA_R, jnp.tanh(lora), lora)
        hi, lo = _split2(lora)
        lh_scr[rows, :] = hi
        ll_scr[rows, :] = lo

    r = sh(r_ref, mur_ref[...])
    k = sh(k_ref, muk_ref[...])
    v = sh(v_ref, muv_ref[...])
    l_hi = lh_scr[rows, :]
    l_lo = ll_scr[rows, :]
    raw = (jnp.dot(l_hi, uph_ref[0], preferred_element_type=F32)
           + jnp.dot(l_hi, upl_ref[0], preferred_element_type=F32)
           + jnp.dot(l_lo, uph_ref[0], preferred_element_type=F32)) + bias_ref[0]
    kk = k * kk_ref[...]
    kk = kk * lax.rsqrt(_seg_sum(kk * kk, seg) + 1e-12)
    kd_sum = jnp.zeros_like(k)
    for d in range(2):
        decay = jnp.exp(-DECAY_SCALE * jax.nn.sigmoid(raw[:, d * LANES:(d + 1) * LANES]))
        iclr = jax.nn.sigmoid(raw[:, (2 + d) * LANES:(3 + d) * LANES])
        kd = k * (1.0 + (iclr - 1.0) * ka_ref[...])
        w_o[d, 0] = decay.T
        kd_o[d, 0] = kd.T
        bd_o[d, 0] = (kk * iclr).T
        kd_sum = kd_sum + kd
    r_o[0] = r.T
    v_o[0] = v.T
    a_o[0] = (-kk).T
    bonus_o[0] = _seg_sum(r * (0.5 * kd_sum) * rk_ref[...], seg) * v


def pre_scan(proj, mu, up_w, up_b, k_k, k_a, r_k, grid_shift):
    b, t, _ = proj.shape
    up_hi, up_lo = _split2(up_w)
    if not grid_shift:
        assert t == ROW_CHUNK, "sequence shift handles one chunk per sequence"
    assert t % ROW_CHUNK == 0
    col = lambda off: pl.BlockSpec((1, t, LANES), lambda i, j, c: (i, 0, off + j))
    vec = lambda off: pl.BlockSpec((1, LANES), lambda i, j, c: (0, off + j))
    out_t = pl.BlockSpec((1, LANES, ROW_CHUNK), lambda i, j, c: (i, j, c))
    out_t2 = pl.BlockSpec((2, 1, LANES, ROW_CHUNK), lambda i, j, c: (0, i, j, c))
    s_t = jax.ShapeDtypeStruct((b, RWKV_W, t), F32)
    s_t2 = jax.ShapeDtypeStruct((2, b, RWKV_W, t), F32)
    return pl.pallas_call(
        functools.partial(_prescan_kernel, t_len=t, grid_shift=grid_shift),
        grid=(b, HEAD_TILES, t // ROW_CHUNK),
        in_specs=[col(0), col(HEAD_TILES), col(2 * HEAD_TILES),
                  pl.BlockSpec((1, t, LORA_BLK), lambda i, j, c: (i, 0, LORA_OFF // LORA_BLK)),
                  vec(0), vec(HEAD_TILES), vec(2 * HEAD_TILES),
                  pl.BlockSpec((1, LORA_BLK), lambda i, j, c: (0, LORA_OFF // LORA_BLK)),
                  pl.BlockSpec((1, LORA_BLK, 4 * LANES), lambda i, j, c: (j, 0, 0)),
                  pl.BlockSpec((1, LORA_BLK, 4 * LANES), lambda i, j, c: (j, 0, 0)),
                  pl.BlockSpec((1, 1, 4 * LANES), lambda i, j, c: (j, 0, 0)),
                  vec(0), vec(0), vec(0)],
        out_specs=[out_t, out_t, out_t,
                   pl.BlockSpec((1, ROW_CHUNK, LANES), lambda i, j, c: (i, c, j)),
                   out_t2, out_t2, out_t2],
        out_shape=[s_t, s_t, s_t, jax.ShapeDtypeStruct((b, t, RWKV_W), F32), s_t2, s_t2, s_t2],
        scratch_shapes=[pltpu.VMEM((t, LORA_BLK), BF16), pltpu.VMEM((t, LORA_BLK), BF16)],
        compiler_params=_params("arbitrary", "arbitrary", "arbitrary"),
        name="pre_scan",
    )(proj, proj, proj, proj, mu, mu, mu, mu, up_hi, up_lo, up_b,
      k_k.reshape(1, RWKV_W), k_a.reshape(1, RWKV_W), r_k.reshape(1, RWKV_W))


def _lora_up_tables(w_decay_up, w_decay0, w_iclr_up, w_iclr0):
    mats = [w_decay_up[0], w_decay_up[1], w_iclr_up[0], w_iclr_up[1]]
    bias = [w_decay0[0], w_decay0[1], w_iclr0[0], w_iclr0[1]]
    up = jnp.zeros((HEAD_TILES, LORA_BLK, 4 * LANES), F32)
    for q, m in enumerate(mats):
        blk = m.astype(F32).reshape(LORA_R, HEAD_TILES, LANES).transpose(1, 0, 2)
        up = up.at[:, q * LORA_R:(q + 1) * LORA_R, q * LANES:(q + 1) * LANES].set(blk)
    b = jnp.stack([v.astype(F32).reshape(HEAD_TILES, LANES) for v in bias], axis=1)
    return up, b.reshape(HEAD_TILES, 1, 4 * LANES)


def _scan_chunk(ins, y_ref, yb_ref, s_scr, wr_scr, *, tc, backward=None):
    mixed = yb_ref is not None
    if mixed:
        fwd_lane = lax.broadcasted_iota(jnp.int32, (1, LANES), 1) < LANES // 2
        tt = lambda t: t

        def view(i):
            fw, bw = ins[i], ins[6 + i]
            full = lambda t: jnp.where(fwd_lane, fw[t], bw[tc - 1 - t])
            row = lambda t, k: jnp.where(fwd_lane, fw[t, k:k + 1, :], bw[tc - 1 - t, k:k + 1, :])
            return full, row
    else:
        tt = lambda t: jnp.where(backward, tc - 1 - t, t)

        def view(i):
            ref = ins[i]
            return (lambda t: ref[tt(t)]), (lambda t, k: ref[tt(t), k:k + 1, :])

    (w_full, w_row), (k_full, k_row), (b_full, b_row), (_, a_row), (r_full, _), (v_full, _) = (
        view(i) for i in range(6))

    wr_scr[0] = w_full(0) * r_full(0)
    sa = jnp.zeros((HEAD_N, LANES), F32)
    y0 = jnp.zeros((HEAD_N, LANES), F32)
    for k in range(HEAD_N):
        s = s_scr[k]
        sa = sa + s * a_row(0, k)
        y0 = y0 + s * wr_scr[0, k:k + 1, :]

    def step(t, carry):
        sa, y0 = carry
        r_t = r_full(t)
        v_t = v_full(t)
        br = jnp.sum(b_full(t) * r_t, axis=0, keepdims=True)
        kr = jnp.sum(k_full(t) * r_t, axis=0, keepdims=True)
        y = y0 + sa * br + v_t * kr
        y_ref[tt(t)] = y
        if mixed:
            yb_ref[tc - 1 - t] = y
        tn = jnp.minimum(t + 1, tc - 1)
        slot = (t + 1) % 2
        wr_scr[slot] = w_full(tn) * r_full(tn)
        sa_n = jnp.zeros((HEAD_N, LANES), F32)
        y0_n = jnp.zeros((HEAD_N, LANES), F32)
        for k in range(HEAD_N):
            s = s_scr[k] * w_row(t, k) + sa * b_row(t, k) + v_t * k_row(t, k)
            s_scr[k] = s
            sa_n = sa_n + s * a_row(tn, k)
            y0_n = y0_n + s * wr_scr[slot, k:k + 1, :]
        return sa_n, y0_n

    lax.fori_loop(0, tc, step, (sa, y0))


def _scan_kernel(*refs, tc, mixed):
    n_in = 12 if mixed else 6
    ins, rest = refs[:n_in], refs[n_in:]
    if mixed:
        s0_ref, y_ref, yb_ref, sfin_ref, s_scr, wr_scr = rest
        backward = None
    else:
        s0_ref, y_ref, sfin_ref, s_scr, wr_scr = rest
        yb_ref = None
        backward = pl.program_id(0) >= pl.num_programs(0) // 2
    c = pl.program_id(1)

    @pl.when(c == 0)
    def _():
        s_scr[...] = s0_ref[0]

    _scan_chunk(ins, y_ref, yb_ref, s_scr, wr_scr, tc=tc, backward=backward)

    @pl.when(c == pl.num_programs(1) - 1)
    def _():
        sfin_ref[0] = s_scr[...]


def wkv_scan(ops, s0, tc=32):
    t, _, lanes = ops[0].shape
    groups = lanes // LANES
    nc = t // tc
    mixed = groups == 1
    st = pl.BlockSpec((1, HEAD_N, HEAD_N, LANES), lambda g, c: (g, 0, 0, 0))
    y_shape = jax.ShapeDtypeStruct((t, HEAD_N, lanes), F32)
    if mixed:
        fwd = pl.BlockSpec((tc, HEAD_N, LANES), lambda g, c: (c, 0, g))
        bwd = pl.BlockSpec((tc, HEAD_N, LANES), lambda g, c: (nc - 1 - c, 0, g))
        in_specs, operands = [fwd] * 6 + [bwd] * 6 + [st], list(ops) + list(ops) + [s0]
        out_specs, out_shape = [fwd, bwd, st], [y_shape, y_shape]
    else:
        half_g = groups // 2
        tmap = lambda g, c: jnp.where(g >= half_g, nc - 1 - c, c)
        seq = pl.BlockSpec((tc, HEAD_N, LANES), lambda g, c: (tmap(g, c), 0, g))
        shared = pl.BlockSpec((tc, HEAD_N, LANES), lambda g, c: (tmap(g, c), 0, g % half_g))
        in_specs = [seq if o.shape[2] == lanes else shared for o in ops] + [st]
        operands = list(ops) + [s0]
        out_specs, out_shape = [seq, st], [y_shape]
    out_shape.append(jax.ShapeDtypeStruct((groups, HEAD_N, HEAD_N, LANES), F32))
    res = pl.pallas_call(
        functools.partial(_scan_kernel, tc=tc, mixed=mixed),
        grid=(groups, nc),
        in_specs=in_specs,
        out_specs=out_specs,
        out_shape=out_shape,
        scratch_shapes=[pltpu.VMEM((HEAD_N, HEAD_N, LANES), F32),
                        pltpu.VMEM((2, HEAD_N, LANES), F32)],
        compiler_params=_params("arbitrary", "arbitrary"),
        name="wkv_scan",
    )(*operands)
    half = lanes // 2
    if mixed:
        y_f, y_b, s_fin = res
        return y_f[..., :half] + y_b[..., half:], s_fin
    y, s_fin = res
    return y[..., :half] + y[..., half:], s_fin


K_BLOCK = 8


def _relayout_kernel(*refs, cfg):
    n_ops = len(cfg)
    ins, outs, scr = refs[:n_ops], refs[n_ops:2 * n_ops], refs[2 * n_ops]
    for x_ref, o_ref, (rb, dup) in zip(ins, outs, cfg):
        x2 = x_ref.reshape(rb * K_BLOCK, LANES)
        for r0 in range(0, rb, LANES):
            n_rows = min(LANES, rb - r0)
            for kk in range(K_BLOCK):
                m = x2[pl.ds(r0 * K_BLOCK + kk, n_rows, stride=K_BLOCK), :]
                if dup:
                    m = jnp.concatenate([m, m], axis=0)
                scr[pl.ds(kk, LANES, stride=K_BLOCK), :] = m.T
            o_ref[:, :, r0:r0 + LANES] = scr[...].reshape(LANES, K_BLOCK, LANES)


def relayout_to_scan(xs, dups):
    t = xs[0].shape[2]
    cfg, in_specs, out_specs, out_shape = [], [], [], []
    for x, dup in zip(xs, dups):
        rows = x.shape[0]
        if dup:
            assert 2 * rows == LANES
        else:
            assert rows % LANES == 0
        rb = min(rows, 512)
        lanes_blk = 2 * rb if dup else rb
        cfg.append((rb, dup))
        in_specs.append(pl.BlockSpec((rb, K_BLOCK, LANES), lambda r, k, c: (r, k, c)))
        out_specs.append(pl.BlockSpec((LANES, K_BLOCK, lanes_blk), lambda r, k, c: (c, k, r)))
        out_shape.append(jax.ShapeDtypeStruct((t, HEAD_N, lanes_blk * (rows // rb)), F32))
    n_row_blocks = xs[0].shape[0] // cfg[0][0]
    assert all(x.shape[0] // rb == n_row_blocks for x, (rb, _) in zip(xs, cfg))
    return pl.pallas_call(
        functools.partial(_relayout_kernel, cfg=tuple(cfg)),
        grid=(n_row_blocks, HEAD_N // K_BLOCK, t // LANES),
        in_specs=in_specs,
        out_specs=out_specs,
        out_shape=out_shape,
        scratch_shapes=[pltpu.VMEM((LANES * K_BLOCK, LANES), F32)],
        compiler_params=_params("arbitrary", "arbitrary", "arbitrary"),
        name="relayout_to_scan",
    )(*xs)


def _scan_rows(x):
    return x.reshape(-1, HEAD_N, x.shape[-1])


def _from_scan_layout(y, b):
    t = y.shape[0]
    return y.reshape(t, HEAD_N, b, N_HEADS).transpose(2, 0, 3, 1).reshape(b, t, RWKV_W)


def _postscan_kernel(y_ref, bonus_ref, g_ref, gw_ref, gb_ref, o_ref):
    seg = _seg_ones()
    for j in range(HEAD_TILES):
        cols = slice(j * LANES, (j + 1) * LANES)
        y = y_ref[0, :, cols]
        mu = _seg_sum(y, seg) * (1.0 / HEAD_N)
        d = y - mu
        var = _seg_sum(d * d, seg) * (1.0 / HEAD_N)
        yn = d * lax.rsqrt(var + GN_EPS) * gw_ref[:, cols] + gb_ref[:, cols]
        o_ref[0, :, cols] = ((yn + bonus_ref[0, :, cols]) * jax.nn.sigmoid(g_ref[0, :, cols])).astype(BF16)


def post_scan(y, bonus, proj, gn_w, gn_b):
    b, t, _ = y.shape
    ch = ROW_CHUNK
    blk = pl.BlockSpec((1, ch, RWKV_W), lambda i, c: (i, c, 0))
    vec = pl.BlockSpec((1, RWKV_W), lambda i, c: (0, 0))
    return pl.pallas_call(
        _postscan_kernel,
        grid=(b, t // ch),
        in_specs=[blk, blk,
                  pl.BlockSpec((1, ch, RWKV_W), lambda i, c: (i, c, G_OFF // RWKV_W)),
                  vec, vec],
        out_specs=blk,
        out_shape=jax.ShapeDtypeStruct((b, t, RWKV_W), BF16),
        compiler_params=_params("arbitrary", "arbitrary"),
        name="post_scan",
    )(y, bonus, proj, gn_w.reshape(1, RWKV_W), gn_b.reshape(1, RWKV_W))


def _fnet_kernel(f_ref, cc_ref, ct_ref, o_ref, xcs_ref, *, t_len, scale):
    @pl.when(pl.program_id(2) == 0)
    def _():
        xc = jnp.dot(f_ref[0].astype(BF16), cc_ref[...], preferred_element_type=F32)
        xcs_ref[0:t_len, :] = xc[:, :FOURIER_GW].astype(BF16)
        xcs_ref[t_len:2 * t_len, :] = xc[:, FOURIER_GW:].astype(BF16)

    out = jnp.dot(ct_ref[...], xcs_ref[...], preferred_element_type=F32)
    o_ref[0] = (out * scale).astype(BF16)


def _fnet_short_kernel(f_ref, cc_ref, ct_ref, o_ref, *, t_len, scale):
    gw = FOURIER_GW
    for g in range(N_FOURIER_GROUPS):
        cols = slice(g * gw, (g + 1) * gw)
        xc = jnp.dot(f_ref[0, :, cols].astype(BF16), cc_ref[...], preferred_element_type=F32)
        out = (jnp.dot(ct_ref[:, :t_len], xc[:, :gw].astype(BF16), preferred_element_type=F32)
               + jnp.dot(ct_ref[:, t_len:], xc[:, gw:].astype(BF16), preferred_element_type=F32))
        o_ref[0, :, cols] = (out * scale).astype(BF16)


def _dft_tables(n):
    j = lax.broadcasted_iota(jnp.int32, (n, n), 0)
    k = lax.broadcasted_iota(jnp.int32, (n, n), 1)
    ang = ((j * k) % n).astype(F32) * (2.0 * math.pi / n)
    return jnp.cos(ang), jnp.sin(ang)


def fnet_mix(proj):
    b, t, _ = proj.shape
    cc, sc = _dft_tables(FOURIER_GW)
    ct, st = _dft_tables(t)
    cc2 = jnp.concatenate([cc, sc], axis=1).astype(BF16)
    ct2 = jnp.concatenate([ct, -st], axis=1).astype(BF16)
    scale = 1.0 / math.sqrt(t * FOURIER_GW)
    if t <= 512:
        blk = pl.BlockSpec((1, t, RWKV_W), lambda i: (i, 0, 0))
        return pl.pallas_call(
            functools.partial(_fnet_short_kernel, t_len=t, scale=scale),
            grid=(b,),
            in_specs=[pl.BlockSpec((1, t, RWKV_W), lambda i: (i, 0, F_OFF // RWKV_W)),
                      pl.BlockSpec((FOURIER_GW, 2 * FOURIER_GW), lambda i: (0, 0)),
                      pl.BlockSpec((t, 2 * t), lambda i: (0, 0))],
            out_specs=blk,
            out_shape=jax.ShapeDtypeStruct((b, t, RWKV_W), BF16),
            compiler_params=_params("arbitrary"),
            name="fnet_mix",
        )(proj, cc2, ct2)
    tq = min(512, t)
    return pl.pallas_call(
        functools.partial(_fnet_kernel, t_len=t, scale=scale),
        grid=(b, N_FOURIER_GROUPS, t // tq),
        in_specs=[pl.BlockSpec((1, t, FOURIER_GW), lambda i, g, q: (i, 0, F_OFF // FOURIER_GW + g)),
                  pl.BlockSpec((FOURIER_GW, 2 * FOURIER_GW), lambda i, g, q: (0, 0)),
                  pl.BlockSpec((tq, 2 * t), lambda i, g, q: (q, 0))],
        out_specs=pl.BlockSpec((1, tq, FOURIER_GW), lambda i, g, q: (i, q, g)),
        out_shape=jax.ShapeDtypeStruct((b, t, RWKV_W), BF16),
        scratch_shapes=[pltpu.VMEM((2 * t, FOURIER_GW), BF16)],
        compiler_params=_params("arbitrary", "arbitrary", "arbitrary"),
        name="fnet_mix",
    )(proj, cc2, ct2)


def _outproj_kernel(yr_ref, yf_ref, w_ref, x_ref, mod_ref, o_ref):
    m = jnp.dot(yr_ref[...], w_ref[0:RWKV_W, :], preferred_element_type=F32)
    m = m + jnp.dot(yf_ref[...], w_ref[RWKV_W:2 * RWKV_W, :], preferred_element_type=F32)
    o_ref[...] = x_ref[...] + mod_ref[0, 2:3, :] * m


def out_proj(yr, yf, w_bf, x2d, mod, rows_per_mod, mod_base):
    m, d = x2d.shape
    tm, tn = min(1024, rows_per_mod), 512
    row_of = _mod_row_map(tm, rows_per_mod, mod_base)
    return pl.pallas_call(
        _outproj_kernel,
        grid=(m // tm, d // tn),
        in_specs=[pl.BlockSpec((tm, RWKV_W), lambda i, j: (i, 0)),
                  pl.BlockSpec((tm, RWKV_W), lambda i, j: (i, 0)),
                  pl.BlockSpec((2 * RWKV_W, tn), lambda i, j: (0, j)),
                  pl.BlockSpec((tm, tn), lambda i, j: (i, j)),
                  pl.BlockSpec((1, N_MOD, tn), lambda i, j: (row_of(i), 0, j))],
        out_specs=pl.BlockSpec((tm, tn), lambda i, j: (i, j)),
        out_shape=jax.ShapeDtypeStruct((m, d), F32),
        compiler_params=_params("arbitrary", "arbitrary"),
        name="out_proj",
    )(yr, yf, w_bf, x2d, mod)


def _norm2_router_kernel(x_ref, mod_ref, g_ref, rw_ref, h_ref, aff_ref):
    h = _rms_modulate(x_ref[...], g_ref[...], mod_ref[0, 4:5, :], mod_ref[0, 3:4, :])
    h_ref[...] = h
    logits = jnp.dot(h, rw_ref[...], precision=HIGHEST, preferred_element_type=F32)
    z = logits - jnp.max(logits, axis=-1, keepdims=True)
    e = jnp.exp(z)
    aff_ref[...] = e / jnp.sum(e, axis=-1, keepdims=True)


def norm2_router(x2d, mod, norm_g, router_w, rows_per_mod, mod_base):
    m, d = x2d.shape
    tm = ROW_CHUNK
    row_of = _mod_row_map(tm, rows_per_mod, mod_base)
    return pl.pallas_call(
        _norm2_router_kernel,
        grid=(m // tm,),
        in_specs=[pl.BlockSpec((tm, d), lambda i: (i, 0)),
                  pl.BlockSpec((1, N_MOD, d), lambda i: (row_of(i), 0, 0)),
                  pl.BlockSpec((1, d), lambda i: (0, 0)),
                  pl.BlockSpec((d, N_EXPERTS), lambda i: (0, 0))],
        out_specs=[pl.BlockSpec((tm, d), lambda i: (i, 0)),
                   pl.BlockSpec((tm, N_EXPERTS), lambda i: (i, 0))],
        out_shape=[jax.ShapeDtypeStruct((m, d), F32),
                   jax.ShapeDtypeStruct((m, N_EXPERTS), F32)],
        compiler_params=_params("arbitrary"),
        name="norm2_router",
    )(x2d, mod, norm_g.reshape(1, d), router_w)


GATHER_ROWS = 256
PAIR_CHUNK = LANES
COMBINE_ROWS = 256


def _issue_row_gather(idx_ref, base, n_rows, src_hbm, dst, sem, unroll=8):
    def body(r, carry):
        row = idx_ref[base + r]
        pltpu.make_async_copy(src_hbm.at[pl.ds(row, 1), :], dst.at[pl.ds(r, 1), :], sem).start()
        return carry
    lax.fori_loop(0, n_rows, body, 0, unroll=unroll)


def _wait_row_gather(n_rows, src_hbm, dst, sem):
    pltpu.make_async_copy(src_hbm.at[pl.ds(0, n_rows), :], dst, sem).wait()


def _gather_kernel(idx_ref, h_hbm, o_ref, buf, sem):
    s = pl.program_id(0)
    slot = s % 2

    @pl.when(s == 0)
    def _():
        _issue_row_gather(idx_ref, 0, GATHER_ROWS, h_hbm, buf.at[0], sem.at[0])

    @pl.when(s + 1 < pl.num_programs(0))
    def _():
        _issue_row_gather(idx_ref, (s + 1) * GATHER_ROWS, GATHER_ROWS, h_hbm,
                          buf.at[1 - slot], sem.at[1 - slot])

    _wait_row_gather(GATHER_ROWS, h_hbm, buf.at[slot], sem.at[slot])
    o_ref[...] = buf[slot].astype(BF16)


def gather_tokens(h, idx_flat):
    n, d = h.shape
    rows = idx_flat.shape[0]
    return pl.pallas_call(
        _gather_kernel,
        grid_spec=pltpu.PrefetchScalarGridSpec(
            num_scalar_prefetch=1,
            grid=(rows // GATHER_ROWS,),
            in_specs=[pl.BlockSpec(memory_space=pl.ANY)],
            out_specs=pl.BlockSpec((GATHER_ROWS, d), lambda s, idx: (s, 0)),
            scratch_shapes=[pltpu.VMEM((2, GATHER_ROWS, d), F32), pltpu.SemaphoreType.DMA((2,))]),
        out_shape=jax.ShapeDtypeStruct((rows, d), BF16),
        compiler_params=_params("arbitrary"),
        name="gather_tokens",
    )(idx_flat, h)


def _ffn_accumulate(xe_ref, w1_ref, w3_ref, w2_ref, o_ref, *, tf, ff):
    f = pl.program_id(1)
    x = xe_ref[0]
    h1 = jnp.dot(x, w1_ref[0].astype(BF16), preferred_element_type=F32)
    h3 = jnp.dot(x, w3_ref[0].astype(BF16), preferred_element_type=F32)
    hid = h1 * jax.nn.sigmoid(h1) * h3
    valid = ff - f * tf
    hid = jnp.where(lax.broadcasted_iota(jnp.int32, (1, tf), 1) < valid, hid, 0.0)
    w2 = jnp.where(lax.broadcasted_iota(jnp.int32, (tf, 1), 0) < valid, w2_ref[0], 0.0)
    o_ref[0] += jnp.dot(hid.astype(BF16), w2.astype(BF16), preferred_element_type=F32)


def _expert_kernel(xe_ref, w1_ref, w3_ref, w2_ref, gate_ref, o_ref, *, tf, ff):
    f = pl.program_id(1)

    @pl.when(f == 0)
    def _():
        o_ref[...] = jnp.zeros_like(o_ref)

    _ffn_accumulate(xe_ref, w1_ref, w3_ref, w2_ref, o_ref, tf=tf, ff=ff)

    @pl.when(f == pl.num_programs(1) - 1)
    def _():
        o_ref[0] = o_ref[0] * gate_ref[0]


def expert_ffn(xe, w1, w3, w2, gate):
    e, cap, d = xe.shape
    ff = w1.shape[2]
    tf = 512
    nf = pl.cdiv(ff, tf)
    return pl.pallas_call(
        functools.partial(_expert_kernel, tf=tf, ff=ff),
        grid=(e, nf),
        in_specs=[pl.BlockSpec((1, cap, d), lambda i, f: (i, 0, 0), pipeline_mode=pl.Buffered(1)),
                  pl.BlockSpec((1, d, tf), lambda i, f: (i, 0, f)),
                  pl.BlockSpec((1, d, tf), lambda i, f: (i, 0, f)),
                  pl.BlockSpec((1, tf, d), lambda i, f: (i, f, 0)),
                  pl.BlockSpec((1, cap, 1), lambda i, f: (i, 0, 0))],
        out_specs=pl.BlockSpec((1, cap, d), lambda i, f: (i, 0, 0)),
        out_shape=jax.ShapeDtypeStruct((e, cap, d), F32),
        compiler_params=_params("arbitrary", "arbitrary"),
        name="expert_ffn",
    )(xe, w1, w3, w2, gate.reshape(e, cap, 1))


def _combine_final_kernel(rows_ref, ptr_ref, tok_ref, ye_hbm, x_ref, mod_ref, g_ref, o_ref,
                          buf, sem, acc_ref, *, tm):
    i = pl.program_id(0)
    p0 = ptr_ref[i]
    p1 = ptr_ref[i + 1]
    c0 = p0 // PAIR_CHUNK
    c1 = jnp.where(p1 > p0, (p1 - 1) // PAIR_CHUNK + 1, c0)
    acc_ref[...] = jnp.zeros_like(acc_ref)

    @pl.when(c1 > c0)
    def _():
        _issue_row_gather(rows_ref, c0 * PAIR_CHUNK, PAIR_CHUNK, ye_hbm, buf.at[0], sem.at[0])

    last_chunk = tok_ref.shape[0] - 1

    def chunk(c, carry):
        slot = (c - c0) % 2
        _wait_row_gather(PAIR_CHUNK, ye_hbm, buf.at[slot], sem.at[slot])
        _issue_row_gather(rows_ref, jnp.minimum(c + 1, last_chunk) * PAIR_CHUNK, PAIR_CHUNK, ye_hbm,
                          buf.at[1 - slot], sem.at[1 - slot], unroll=True)
        local = tok_ref[pl.ds(c, 1), :] - i * tm
        onehot = (lax.broadcasted_iota(jnp.int32, (tm, PAIR_CHUNK), 0) == local).astype(BF16)
        acc_ref[...] += jnp.dot(onehot, buf[slot].astype(BF16), preferred_element_type=F32)
        return carry

    lax.fori_loop(c0, c1, chunk, 0)

    @pl.when(c1 > c0)
    def _():
        tail = (c1 - c0) % 2
        _wait_row_gather(PAIR_CHUNK, ye_hbm, buf.at[tail], sem.at[tail])

    x = x_ref[...] + mod_ref[0, 5:6, :] * acc_ref[...]
    ms = jnp.mean(x * x, axis=-1, keepdims=True)
    o_ref[...] = x * lax.rsqrt(ms + NORM_EPS) * g_ref[...]


def combine_final(x2d, ye_flat, pair_rows, pair_toks, tile_ptr, mod, final_g, rows_per_mod, mod_base):
    m, d = x2d.shape
    tm = COMBINE_ROWS
    n_pairs = pair_rows.shape[0]
    row_of = _mod_row_map(tm, rows_per_mod, mod_base)
    return pl.pallas_call(
        functools.partial(_combine_final_kernel, tm=tm),
        grid_spec=pltpu.PrefetchScalarGridSpec(
            num_scalar_prefetch=2,
            grid=(m // tm,),
            in_specs=[pl.BlockSpec((n_pairs // PAIR_CHUNK, PAIR_CHUNK), lambda i, r, p: (0, 0)),
                      pl.BlockSpec(memory_space=pl.ANY),
                      pl.BlockSpec((tm, d), lambda i, r, p: (i, 0)),
                      pl.BlockSpec((1, N_MOD, d), lambda i, r, p: (row_of(i), 0, 0)),
                      pl.BlockSpec((1, d), lambda i, r, p: (0, 0))],
            out_specs=pl.BlockSpec((tm, d), lambda i, r, p: (i, 0)),
            scratch_shapes=[pltpu.VMEM((2, PAIR_CHUNK, d), F32), pltpu.SemaphoreType.DMA((2,)),
                            pltpu.VMEM((tm, d), F32)]),
        out_shape=jax.ShapeDtypeStruct((m, d), F32),
        compiler_params=_params("arbitrary"),
        name="combine_final",
    )(pair_rows, tile_ptr, pair_toks.reshape(n_pairs // PAIR_CHUNK, PAIR_CHUNK), ye_flat,
      x2d, mod, final_g.reshape(1, d))


def _prepare_in_proj(w_in, shift_mu):
    d = w_in.shape[0]
    rkv, lora, gf = w_in[:, :G_OFF], w_in[:, G_OFF:SHIFT_W], w_in[:, SHIFT_W:]
    w_p = jnp.concatenate([rkv, gf, lora, jnp.zeros((d, LANES), w_in.dtype)], axis=1).astype(BF16)
    mu = jnp.concatenate([shift_mu[:G_OFF], jnp.zeros((2 * RWKV_W,), F32), shift_mu[G_OFF:],
                          jnp.zeros((LANES,), F32)])
    return w_p, mu.reshape(1, IN_PAD_W)


def _mix_inputs(x, mod, rows_per_mod, mod_base, grid_shift, p):
    b, t, d = x.shape
    x2d = x.reshape(b * t, d)
    proj = in_proj(x2d, mod, p["norm1_g"], p["w_in_p"], rows_per_mod, mod_base)
    proj = proj.reshape(b, t, IN_PAD_W)
    r, v, a, bonus, w, kd, bd = pre_scan(proj, p["mu"], p["up_w"], p["up_b"],
                                         p["k_k"], p["k_a"], p["r_k"], grid_shift)
    per_dir = [_scan_rows(w), _scan_rows(kd), _scan_rows(bd)]
    shared = [_scan_rows(a), _scan_rows(r), _scan_rows(v)]
    if 2 * b * N_HEADS == LANES:
        ops = relayout_to_scan(per_dir + shared, [False] * 3 + [True] * 3)
    else:
        ops = (list(relayout_to_scan(per_dir, [False] * 3))
               + list(relayout_to_scan(shared, [False] * 3)))
    return {"x2d": x2d, "proj": proj, "bonus": bonus, "ops": list(ops),
            "mod": (mod, rows_per_mod, mod_base), "shape": (b, t, d)}


def _route(st, y, p):
    b, t, d = st["shape"]
    mod_args = st["mod"]
    yr = post_scan(_from_scan_layout(y, b), st["bonus"], st["proj"], p["gn_w"], p["gn_b"])
    yf = fnet_mix(st["proj"])
    x1 = out_proj(yr.reshape(b * t, RWKV_W), yf.reshape(b * t, RWKV_W), p["w_out_bf"],
                  st["x2d"], *mod_args)
    h2, aff = norm2_router(x1, mod_args[0], p["norm2_g"], p["router_w"], *mod_args[1:])
    n = b * t
    cap = n * CAPACITY_FACTOR // N_EXPERTS
    gate, idx = lax.top_k(aff.T, cap)
    idx_flat = idx.reshape(-1).astype(jnp.int32)
    xe = gather_tokens(h2, idx_flat).reshape(N_EXPERTS, cap, d)
    return {"x1": x1, "xe": xe, "gate": gate, "idx_flat": idx_flat, "mod": mod_args,
            "shape": (b, t, d)}


def _combine(rt, ye, p):
    b, t, d = rt["shape"]
    n = b * t
    idx_flat = rt["idx_flat"]
    pair_toks, pair_rows = lax.sort_key_val(idx_flat, jnp.arange(idx_flat.shape[0], dtype=jnp.int32))
    tile_ptr = jnp.searchsorted(pair_toks, jnp.arange(0, n + 1, COMBINE_ROWS, dtype=jnp.int32),
                                side="left").astype(jnp.int32)
    mod, rows_per_mod, mod_base = rt["mod"]
    out = combine_final(rt["x1"], ye.reshape(-1, d), pair_rows, pair_toks, tile_ptr, mod,
                        p["final_norm_g"], rows_per_mod, mod_base)
    return out.reshape(b, t, d)


def _experts(p):
    return p["exp_w1"], p["exp_w3"], p["exp_w2"]


def _layer(x, mod, rows_per_mod, mod_base, s0, grid_shift, p):
    st = _mix_inputs(x, mod, rows_per_mod, mod_base, grid_shift, p)
    y, s_fin = wkv_scan(st["ops"], s0)
    rt = _route(st, y, p)
    ye = expert_ffn(rt["xe"], *_experts(p), rt["gate"])
    return _combine(rt, ye, p), s_fin


def kernel(x_prompt, x_sample, state_wkv, c, c_ctx, ada_w, ada_b, norm1_g, norm2_g, w_in, shift_mu,
           w_decay_up, w_decay0, w_iclr_up, w_iclr0, k_k, k_a, r_k, gn_w, gn_b, w_out, router_w,
           exp_w1, exp_w3, exp_w2, final_norm_g):
    depth = ada_w.shape[0]
    assert depth == 1, "single-layer trunk"
    l = 0
    bc, tc_len, d = x_prompt.shape
    bl, tl, _ = x_sample.shape
    cvecs = jnp.concatenate([c_ctx[None], c], axis=0)
    mod = ada_mod(cvecs, ada_w[l], ada_b[l]).reshape(1 + bl, N_MOD, d)
    up_w, up_b = _lora_up_tables(w_decay_up[l], w_decay0[l], w_iclr_up[l], w_iclr0[l])
    w_in_p, mu = _prepare_in_proj(w_in[l], shift_mu[l])
    p = {
        "norm1_g": norm1_g[l], "norm2_g": norm2_g[l], "final_norm_g": final_norm_g,
        "w_in_p": w_in_p, "mu": mu, "up_w": up_w, "up_b": up_b,
        "k_k": k_k[l], "k_a": k_a[l], "r_k": r_k[l], "gn_w": gn_w[l], "gn_b": gn_b[l],
        "w_out_bf": w_out[l].astype(BF16), "router_w": router_w[l],
        "exp_w1": exp_w1[l], "exp_w3": exp_w3[l], "exp_w2": exp_w2[l],
    }
    groups_c = 2 * bc * N_HEADS // LANES
    s0_ctx = jnp.zeros((groups_c, HEAD_N, HEAD_N, LANES), F32)
    ctx = _mix_inputs(x_prompt, mod, bc * tc_len, 0, False, p)
    y_ctx, s_ctx = wkv_scan(ctx["ops"], s0_ctx)
    ctx_rt = _route(ctx, y_ctx, p)
    s_ctx = s_ctx.transpose(1, 2, 0, 3).reshape(HEAD_N, HEAD_N, 2, bc, N_HEADS)
    new_state = s_ctx.transpose(3, 2, 4, 1, 0)[:, None]
    groups_l = 2 * bl * N_HEADS // LANES
    s0_lat = state_wkv[:, l].astype(F32).transpose(4, 3, 1, 0, 2)
    s0_lat = s0_lat.reshape(HEAD_N, HEAD_N, groups_l, LANES).transpose(2, 0, 1, 3)
    lat = _mix_inputs(x_sample, mod, tl, 1, True, p)
    ye_ctx = expert_ffn(ctx_rt["xe"], *_experts(p), ctx_rt["gate"])
    y_lat, _ = wkv_scan(lat["ops"], s0_lat)
    lat_rt = _route(lat, y_lat, p)
    ye_lat = expert_ffn(lat_rt["xe"], *_experts(p), lat_rt["gate"])
    return (_combine(ctx_rt, ye_ctx, p), _combine(lat_rt, ye_lat, p), new_state)
```

```python
import functools
import math

import jax
import jax.numpy as jnp
from jax import lax
from jax.experimental import pallas as pl
from jax.experimental.pallas import tpu as pltpu

F32 = jnp.float32
BF16 = jnp.bfloat16

D_MODEL = 2048
RWKV_W = 1024
HEAD_N = 64
N_HEADS = 16
LORA_R = 96
N_EXPERTS = 16
EXPERT_FF = 5504
CAPACITY_FACTOR = 2
N_MOD = 6
NORM_EPS = 1e-6
GN_EPS = 64e-5
FOURIER_GW = 256
N_FOURIER_GROUPS = 4

LANES = 128
HEAD_TILES = RWKV_W // LANES
SHIFT_W = 3 * RWKV_W + 4 * LORA_R
LORA_W = 4 * LORA_R
LORA_BLK = LORA_W + LANES
G_OFF = 3 * RWKV_W
F_OFF = G_OFF + RWKV_W
LORA_OFF = F_OFF + RWKV_W
IN_PAD_W = LORA_OFF + LORA_BLK
ROW_CHUNK = 256
VMEM_LIMIT = 56 * 1024 * 1024
DECAY_SCALE = math.exp(-0.5)


def _params(*sem):
    return pltpu.CompilerParams(dimension_semantics=sem, vmem_limit_bytes=VMEM_LIMIT)


def _seg_ones():
    r = lax.broadcasted_iota(jnp.int32, (LANES, LANES), 0) // HEAD_N
    c = lax.broadcasted_iota(jnp.int32, (LANES, LANES), 1) // HEAD_N
    return (r == c).astype(BF16)


def _split2(x):
    hi = x.astype(BF16)
    return hi, (x - hi.astype(F32)).astype(BF16)


def _split3(x):
    hi = x.astype(BF16)
    rem = x - hi.astype(F32)
    mid = rem.astype(BF16)
    return hi, mid, (rem - mid.astype(F32)).astype(BF16)


def _seg_sum(x, seg):
    return sum(jnp.dot(p, seg, preferred_element_type=F32) for p in _split3(x))


def _ada_kernel(c_ref, w_ref, b_ref, o_ref):
    c = c_ref[...]
    s = c * jax.nn.sigmoid(c)
    o_ref[...] = jnp.dot(s.astype(BF16), w_ref[...].astype(BF16),
                         preferred_element_type=F32) + b_ref[...]


def ada_mod(cvecs, ada_w, ada_b):
    rows, d = cvecs.shape
    n = ada_w.shape[1]
    tn = 1024
    return pl.pallas_call(
        _ada_kernel,
        grid=(n // tn,),
        in_specs=[pl.BlockSpec((rows, d), lambda j: (0, 0)),
                  pl.BlockSpec((d, tn), lambda j: (0, j)),
                  pl.BlockSpec((1, tn), lambda j: (0, j))],
        out_specs=pl.BlockSpec((rows, tn), lambda j: (0, j)),
        out_shape=jax.ShapeDtypeStruct((rows, n), F32),
        compiler_params=_params("arbitrary"),
        name="ada_mod",
    )(cvecs, ada_w, ada_b.reshape(1, n))


def _mod_row_map(tm, rows_per_mod, mod_base):
    return lambda i: mod_base + (i * tm) // rows_per_mod


def _rms_modulate(x, g, scale, shift):
    ms = jnp.mean(x * x, axis=-1, keepdims=True)
    return x * lax.rsqrt(ms + NORM_EPS) * g * (1.0 + scale) + shift


def _inproj_kernel(x_ref, mod_ref, g_ref, w_ref, o_ref, h_ref, *, tm):
    @pl.when(pl.program_id(1) == 0)
    def _():
        def body(c, carry):
            rows = pl.ds(pl.multiple_of(c * ROW_CHUNK, ROW_CHUNK), ROW_CHUNK)
            h = _rms_modulate(x_ref[rows, :], g_ref[...], mod_ref[0, 1:2, :], mod_ref[0, 0:1, :])
            h_ref[rows, :] = h.astype(BF16)
            return carry
        lax.fori_loop(0, tm // ROW_CHUNK, body, 0)

    o_ref[...] = jnp.dot(h_ref[...], w_ref[...], preferred_element_type=F32)


def in_proj(x2d, mod, norm_g, w_p, rows_per_mod, mod_base):
    m, d = x2d.shape
    n = w_p.shape[1]
    tm, tn = min(1024, rows_per_mod), 512
    row_of = _mod_row_map(tm, rows_per_mod, mod_base)
    return pl.pallas_call(
        functools.partial(_inproj_kernel, tm=tm),
        grid=(m // tm, n // tn),
        in_specs=[pl.BlockSpec((tm, d), lambda i, j: (i, 0)),
                  pl.BlockSpec((1, N_MOD, d), lambda i, j: (row_of(i), 0, 0)),
                  pl.BlockSpec((1, d), lambda i, j: (0, 0)),
                  pl.BlockSpec((d, tn), lambda i, j: (0, j))],
        out_specs=pl.BlockSpec((tm, tn), lambda i, j: (i, j)),
        out_shape=jax.ShapeDtypeStruct((m, n), F32),
        scratch_shapes=[pltpu.VMEM((tm, d), BF16)],
        compiler_params=_params("arbitrary", "arbitrary"),
        name="in_proj",
    )(x2d, mod, norm_g.reshape(1, d), w_p)


def _shifted(ref, mu, c, n_chunks, t_len, grid_shift):
    ch = ROW_CHUNK
    base = pl.multiple_of(c * ch, ch)
    cur = ref[0, pl.ds(base, ch), :]
    row = lax.broadcasted_iota(jnp.int32, (ch, 1), 0)
    before = pltpu.roll(cur, 1, 0)
    after = pltpu.roll(cur, ch - 1, 0)
    if not grid_shift:
        prev = jnp.where(row == 0, 0.0, before)
        nxt = jnp.where(row == ch - 1, 0.0, after)
        mixed = 0.5 * (prev + nxt)
    else:
        gw = 64
        left = jnp.where(row % gw == 0, 0.0, before)
        right = jnp.where(row % gw == gw - 1, 0.0, after)
        up_start = pl.multiple_of(jnp.maximum(base - gw, 0), gw)
        dn_start = pl.multiple_of(jnp.minimum(base + ch, t_len - gw), gw)
        up_halo = jnp.where(c > 0, ref[0, pl.ds(up_start, gw), :], 0.0)
        dn_halo = jnp.where(c < n_chunks - 1, ref[0, pl.ds(dn_start, gw), :], 0.0)
        up = jnp.concatenate([up_halo, cur[: ch - gw]], axis=0)
        down = jnp.concatenate([cur[gw:], dn_halo], axis=0)
        mixed = 0.25 * (up + down + left + right)
    return cur + mu * (mixed - cur)


def _prescan_kernel(r_ref, k_ref, v_ref, lora_ref, mur_ref, muk_ref, muv_ref, mul_ref,
                    uph_ref, upl_ref, bias_ref, kk_ref, ka_ref, rk_ref,
                    r_o, v_o, a_o, bonus_o, w_o, kd_o, bd_o, lh_scr, ll_scr, *, t_len, grid_shift):
    n_chunks = t_len // ROW_CHUNK
    seg = _seg_ones()
    c = pl.program_id(2)
    rows = pl.ds(pl.multiple_of(c * ROW_CHUNK, ROW_CHUNK), ROW_CHUNK)
    sh = functools.partial(_shifted, c=c, n_chunks=n_chunks, t_len=t_len, grid_shift=grid_shift)

    @pl.when(pl.program_id(1) == 0)
    def _():
        lane = lax.broadcasted_iota(jnp.int32, (1, LORA_BLK), 1)
        lora = sh(lora_ref, mul_ref[...])
        lora = jnp.where(lane < 2 * LORA_R, jnp.tanh(lora), lora)
        hi, lo = _split2(lora)
        lh_scr[rows, :] = hi
        ll_scr[rows, :] = lo

    r = sh(r_ref, mur_ref[...])
    k = sh(k_ref, muk_ref[...])
    v = sh(v_ref, muv_ref[...])
    l_hi = lh_scr[rows, :]
    l_lo = ll_scr[rows, :]
    raw = (jnp.dot(l_hi, uph_ref[0], preferred_element_type=F32)
           + jnp.dot(l_hi, upl_ref[0], preferred_element_type=F32)
           + jnp.dot(l_lo, uph_ref[0], preferred_element_type=F32)) + bias_ref[0]
    kk = k * kk_ref[...]
    kk = kk * lax.rsqrt(_seg_sum(kk * kk, seg) + 1e-12)
    kd_sum = jnp.zeros_like(k)
    for d in range(2):
        decay = jnp.exp(-DECAY_SCALE * jax.nn.sigmoid(raw[:, d * LANES:(d + 1) * LANES]))
        iclr = jax.nn.sigmoid(raw[:, (2 + d) * LANES:(3 + d) * LANES])
        kd = k * (1.0 + (iclr - 1.0) * ka_ref[...])
        w_o[d, 0] = decay.T
        kd_o[d, 0] = kd.T
        bd_o[d, 0] = (kk * iclr).T
        kd_sum = kd_sum + kd
    r_o[0] = r.T
    v_o[0] = v.T
    a_o[0] = (-kk).T
    bonus_o[0] = _seg_sum(r * (0.5 * kd_sum) * rk_ref[...], seg) * v


def pre_scan(proj, mu, up_w, up_b, k_k, k_a, r_k, grid_shift):
    b, t, _ = proj.shape
    up_hi, up_lo = _split2(up_w)
    if not grid_shift:
        assert t == ROW_CHUNK, "sequence shift handles one chunk per sequence"
    assert t % ROW_CHUNK == 0
    col = lambda off: pl.BlockSpec((1, t, LANES), lambda i, j, c: (i, 0, off + j))
    vec = lambda off: pl.BlockSpec((1, LANES), lambda i, j, c: (0, off + j))
    out_t = pl.BlockSpec((1, LANES, ROW_CHUNK), lambda i, j, c: (i, j, c))
    out_t2 = pl.BlockSpec((2, 1, LANES, ROW_CHUNK), lambda i, j, c: (0, i, j, c))
    s_t = jax.ShapeDtypeStruct((b, RWKV_W, t), F32)
    s_t2 = jax.ShapeDtypeStruct((2, b, RWKV_W, t), F32)
    return pl.pallas_call(
        functools.partial(_prescan_kernel, t_len=t, grid_shift=grid_shift),
        grid=(b, HEAD_TILES, t // ROW_CHUNK),
        in_specs=[col(0), col(HEAD_TILES), col(2 * HEAD_TILES),
                  pl.BlockSpec((1, t, LORA_BLK), lambda i, j, c: (i, 0, LORA_OFF // LORA_BLK)),
                  vec(0), vec(HEAD_TILES), vec(2 * HEAD_TILES),
                  pl.BlockSpec((1, LORA_BLK), lambda i, j, c: (0, LORA_OFF // LORA_BLK)),
                  pl.BlockSpec((1, LORA_BLK, 4 * LANES), lambda i, j, c: (j, 0, 0)),
                  pl.BlockSpec((1, LORA_BLK, 4 * LANES), lambda i, j, c: (j, 0, 0)),
                  pl.BlockSpec((1, 1, 4 * LANES), lambda i, j, c: (j, 0, 0)),
                  vec(0), vec(0), vec(0)],
        out_specs=[out_t, out_t, out_t,
                   pl.BlockSpec((1, ROW_CHUNK, LANES), lambda i, j, c: (i, c, j)),
                   out_t2, out_t2, out_t2],
        out_shape=[s_t, s_t, s_t, jax.ShapeDtypeStruct((b, t, RWKV_W), F32), s_t2, s_t2, s_t2],
        scratch_shapes=[pltpu.VMEM((t, LORA_BLK), BF16), pltpu.VMEM((t, LORA_BLK), BF16)],
        compiler_params=_params("arbitrary", "arbitrary", "arbitrary"),
        name="pre_scan",
    )(proj, proj, proj, proj, mu, mu, mu, mu, up_hi, up_lo, up_b,
      k_k.reshape(1, RWKV_W), k_a.reshape(1, RWKV_W), r_k.reshape(1, RWKV_W))


def _lora_up_tables(w_decay_up, w_decay0, w_iclr_up, w_iclr0):
    mats = [w_decay_up[0], w_decay_up[1], w_iclr_up[0], w_iclr_up[1]]
    bias = [w_decay0[0], w_decay0[1], w_iclr0[0], w_iclr0[1]]
    up = jnp.zeros((HEAD_TILES, LORA_BLK, 4 * LANES), F32)
    for q, m in enumerate(mats):
        blk = m.astype(F32).reshape(LORA_R, HEAD_TILES, LANES).transpose(1, 0, 2)
        up = up.at[:, q * LORA_R:(q + 1) * LORA_R, q * LANES:(q + 1) * LANES].set(blk)
    b = jnp.stack([v.astype(F32).reshape(HEAD_TILES, LANES) for v in bias], axis=1)
    return up, b.reshape(HEAD_TILES, 1, 4 * LANES)


def _scan_chunk(ins, y_ref, yb_ref, s_scr, wr_scr, *, tc, backward=None):
    mixed = yb_ref is not None
    if mixed:
        fwd_lane = lax.broadcasted_iota(jnp.int32, (1, LANES), 1) < LANES // 2
        tt = lambda t: t

        def view(i):
            fw, bw = ins[i], ins[6 + i]
            full = lambda t: jnp.where(fwd_lane, fw[t], bw[tc - 1 - t])
            row = lambda t, k: jnp.where(fwd_lane, fw[t, k:k + 1, :], bw[tc - 1 - t, k:k + 1, :])
            return full, row
    else:
        tt = lambda t: jnp.where(backward, tc - 1 - t, t)

        def view(i):
            ref = ins[i]
            return (lambda t: ref[tt(t)]), (lambda t, k: ref[tt(t), k:k + 1, :])

    (w_full, w_row), (k_full, k_row), (b_full, b_row), (_, a_row), (r_full, _), (v_full, _) = (
        view(i) for i in range(6))

    wr_scr[0] = w_full(0) * r_full(0)
    sa = jnp.zeros((HEAD_N, LANES), F32)
    y0 = jnp.zeros((HEAD_N, LANES), F32)
    for k in range(HEAD_N):
        s = s_scr[k]
        sa = sa + s * a_row(0, k)
        y0 = y0 + s * wr_scr[0, k:k + 1, :]

    def step(t, carry):
        sa, y0 = carry
        r_t = r_full(t)
        v_t = v_full(t)
        br = jnp.sum(b_full(t) * r_t, axis=0, keepdims=True)
        kr = jnp.sum(k_full(t) * r_t, axis=0, keepdims=True)
        y = y0 + sa * br + v_t * kr
        y_ref[tt(t)] = y
        if mixed:
            yb_ref[tc - 1 - t] = y
        tn = jnp.minimum(t + 1, tc - 1)
        slot = (t + 1) % 2
        wr_scr[slot] = w_full(tn) * r_full(tn)
        sa_n = jnp.zeros((HEAD_N, LANES), F32)
        y0_n = jnp.zeros((HEAD_N, LANES), F32)
        for k in range(HEAD_N):
            s = s_scr[k] * w_row(t, k) + sa * b_row(t, k) + v_t * k_row(t, k)
            s_scr[k] = s
            sa_n = sa_n + s * a_row(tn, k)
            y0_n = y0_n + s * wr_scr[slot, k:k + 1, :]
        return sa_n, y0_n

    lax.fori_loop(0, tc, step, (sa, y0))


def _scan_kernel(*refs, tc, mixed):
    n_in = 12 if mixed else 6
    ins, rest = refs[:n_in], refs[n_in:]
    if mixed:
        s0_ref, y_ref, yb_ref, sfin_ref, s_scr, wr_scr = rest
        backward = None
    else:
        s0_ref, y_ref, sfin_ref, s_scr, wr_scr = rest
        yb_ref = None
        backward = pl.program_id(0) >= pl.num_programs(0) // 2
    c = pl.program_id(1)

    @pl.when(c == 0)
    def _():
        s_scr[...] = s0_ref[0]

    _scan_chunk(ins, y_ref, yb_ref, s_scr, wr_scr, tc=tc, backward=backward)

    @pl.when(c == pl.num_programs(1) - 1)
    def _():
        sfin_ref[0] = s_scr[...]


def wkv_scan(ops, s0, tc=32):
    t, _, lanes = ops[0].shape
    groups = lanes // LANES
    nc = t // tc
    mixed = groups == 1
    st = pl.BlockSpec((1, HEAD_N, HEAD_N, LANES), lambda g, c: (g, 0, 0, 0))
    y_shape = jax.ShapeDtypeStruct((t, HEAD_N, lanes), F32)
    if mixed:
        fwd = pl.BlockSpec((tc, HEAD_N, LANES), lambda g, c: (c, 0, g))
        bwd = pl.BlockSpec((tc, HEAD_N, LANES), lambda g, c: (nc - 1 - c, 0, g))
        in_specs, operands = [fwd] * 6 + [bwd] * 6 + [st], list(ops) + list(ops) + [s0]
        out_specs, out_shape = [fwd, bwd, st], [y_shape, y_shape]
    else:
        half_g = groups // 2
        tmap = lambda g, c: jnp.where(g >= half_g, nc - 1 - c, c)
        seq = pl.BlockSpec((tc, HEAD_N, LANES), lambda g, c: (tmap(g, c), 0, g))
        shared = pl.BlockSpec((tc, HEAD_N, LANES), lambda g, c: (tmap(g, c), 0, g % half_g))
        in_specs = [seq if o.shape[2] == lanes else shared for o in ops] + [st]
        operands = list(ops) + [s0]
        out_specs, out_shape = [seq, st], [y_shape]
    out_shape.append(jax.ShapeDtypeStruct((groups, HEAD_N, HEAD_N, LANES), F32))
    res = pl.pallas_call(
        functools.partial(_scan_kernel, tc=tc, mixed=mixed),
        grid=(groups, nc),
        in_specs=in_specs,
        out_specs=out_specs,
        out_shape=out_shape,
        scratch_shapes=[pltpu.VMEM((HEAD_N, HEAD_N, LANES), F32),
                        pltpu.VMEM((2, HEAD_N, LANES), F32)],
        compiler_params=_params("arbitrary", "arbitrary"),
        name="wkv_scan",
    )(*operands)
    half = lanes // 2
    if mixed:
        y_f, y_b, s_fin = res
        return y_f[..., :half] + y_b[..., half:], s_fin
    y, s_fin = res
    return y[..., :half] + y[..., half:], s_fin


K_BLOCK = 8


def _relayout_kernel(*refs, cfg):
    n_ops = len(cfg)
    ins, outs, scr = refs[:n_ops], refs[n_ops:2 * n_ops], refs[2 * n_ops]
    for x_ref, o_ref, (rb, dup) in zip(ins, outs, cfg):
        x2 = x_ref.reshape(rb * K_BLOCK, LANES)
        for r0 in range(0, rb, LANES):
            n_rows = min(LANES, rb - r0)
            for kk in range(K_BLOCK):
                m = x2[pl.ds(r0 * K_BLOCK + kk, n_rows, stride=K_BLOCK), :]
                if dup:
                    m = jnp.concatenate([m, m], axis=0)
                scr[pl.ds(kk, LANES, stride=K_BLOCK), :] = m.T
            o_ref[:, :, r0:r0 + LANES] = scr[...].reshape(LANES, K_BLOCK, LANES)


def relayout_to_scan(xs, dups):
    t = xs[0].shape[2]
    cfg, in_specs, out_specs, out_shape = [], [], [], []
    for x, dup in zip(xs, dups):
        rows = x.shape[0]
        if dup:
            assert 2 * rows == LANES
        else:
            assert rows % LANES == 0
        rb = min(rows, 512)
        lanes_blk = 2 * rb if dup else rb
        cfg.append((rb, dup))
        in_specs.append(pl.BlockSpec((rb, K_BLOCK, LANES), lambda r, k, c: (r, k, c)))
        out_specs.append(pl.BlockSpec((LANES, K_BLOCK, lanes_blk), lambda r, k, c: (c, k, r)))
        out_shape.append(jax.ShapeDtypeStruct((t, HEAD_N, lanes_blk * (rows // rb)), F32))
    n_row_blocks = xs[0].shape[0] // cfg[0][0]
    assert all(x.shape[0] // rb == n_row_blocks for x, (rb, _) in zip(xs, cfg))
    return pl.pallas_call(
        functools.partial(_relayout_kernel, cfg=tuple(cfg)),
        grid=(n_row_blocks, HEAD_N // K_BLOCK, t // LANES),
        in_specs=in_specs,
        out_specs=out_specs,
        out_shape=out_shape,
        scratch_shapes=[pltpu.VMEM((LANES * K_BLOCK, LANES), F32)],
        compiler_params=_params("arbitrary", "arbitrary", "arbitrary"),
        name="relayout_to_scan",
    )(*xs)


def _scan_rows(x):
    return x.reshape(-1, HEAD_N, x.shape[-1])


def _from_scan_layout(y, b):
    t = y.shape[0]
    return y.reshape(t, HEAD_N, b, N_HEADS).transpose(2, 0, 3, 1).reshape(b, t, RWKV_W)


def _postscan_kernel(y_ref, bonus_ref, g_ref, gw_ref, gb_ref, o_ref):
    seg = _seg_ones()
    for j in range(HEAD_TILES):
        cols = slice(j * LANES, (j + 1) * LANES)
        y = y_ref[0, :, cols]
        mu = _seg_sum(y, seg) * (1.0 / HEAD_N)
        d = y - mu
        var = _seg_sum(d * d, seg) * (1.0 / HEAD_N)
        yn = d * lax.rsqrt(var + GN_EPS) * gw_ref[:, cols] + gb_ref[:, cols]
        o_ref[0, :, cols] = ((yn + bonus_ref[0, :, cols]) * jax.nn.sigmoid(g_ref[0, :, cols])).astype(BF16)


def post_scan(y, bonus, proj, gn_w, gn_b):
    b, t, _ = y.shape
    ch = ROW_CHUNK
    blk = pl.BlockSpec((1, ch, RWKV_W), lambda i, c: (i, c, 0))
    vec = pl.BlockSpec((1, RWKV_W), lambda i, c: (0, 0))
    return pl.pallas_call(
        _postscan_kernel,
        grid=(b, t // ch),
        in_specs=[blk, blk,
                  pl.BlockSpec((1, ch, RWKV_W), lambda i, c: (i, c, G_OFF // RWKV_W)),
                  vec, vec],
        out_specs=blk,
        out_shape=jax.ShapeDtypeStruct((b, t, RWKV_W), BF16),
        compiler_params=_params("arbitrary", "arbitrary"),
        name="post_scan",
    )(y, bonus, proj, gn_w.reshape(1, RWKV_W), gn_b.reshape(1, RWKV_W))


def _fnet_kernel(f_ref, cc_ref, ct_ref, o_ref, xcs_ref, *, t_len, scale):
    @pl.when(pl.program_id(2) == 0)
    def _():
        xc = jnp.dot(f_ref[0].astype(BF16), cc_ref[...], preferred_element_type=F32)
        xcs_ref[0:t_len, :] = xc[:, :FOURIER_GW].astype(BF16)
        xcs_ref[t_len:2 * t_len, :] = xc[:, FOURIER_GW:].astype(BF16)

    out = jnp.dot(ct_ref[...], xcs_ref[...], preferred_element_type=F32)
    o_ref[0] = (out * scale).astype(BF16)


def _fnet_short_kernel(f_ref, cc_ref, ct_ref, o_ref, *, t_len, scale):
    gw = FOURIER_GW
    for g in range(N_FOURIER_GROUPS):
        cols = slice(g * gw, (g + 1) * gw)
        xc = jnp.dot(f_ref[0, :, cols].astype(BF16), cc_ref[...], preferred_element_type=F32)
        out = (jnp.dot(ct_ref[:, :t_len], xc[:, :gw].astype(BF16), preferred_element_type=F32)
               + jnp.dot(ct_ref[:, t_len:], xc[:, gw:].astype(BF16), preferred_element_type=F32))
        o_ref[0, :, cols] = (out * scale).astype(BF16)


def _dft_tables(n):
    j = lax.broadcasted_iota(jnp.int32, (n, n), 0)
    k = lax.broadcasted_iota(jnp.int32, (n, n), 1)
    ang = ((j * k) % n).astype(F32) * (2.0 * math.pi / n)
    return jnp.cos(ang), jnp.sin(ang)


def fnet_mix(proj):
    b, t, _ = proj.shape
    cc, sc = _dft_tables(FOURIER_GW)
    ct, st = _dft_tables(t)
    cc2 = jnp.concatenate([cc, sc], axis=1).astype(BF16)
    ct2 = jnp.concatenate([ct, -st], axis=1).astype(BF16)
    scale = 1.0 / math.sqrt(t * FOURIER_GW)
    if t <= 512:
        blk = pl.BlockSpec((1, t, RWKV_W), lambda i: (i, 0, 0))
        return pl.pallas_call(
            functools.partial(_fnet_short_kernel, t_len=t, scale=scale),
            grid=(b,),
            in_specs=[pl.BlockSpec((1, t, RWKV_W), lambda i: (i, 0, F_OFF // RWKV_W)),
                      pl.BlockSpec((FOURIER_GW, 2 * FOURIER_GW), lambda i: (0, 0)),
                      pl.BlockSpec((t, 2 * t), lambda i: (0, 0))],
            out_specs=blk,
            out_shape=jax.ShapeDtypeStruct((b, t, RWKV_W), BF16),
            compiler_params=_params("arbitrary"),
            name="fnet_mix",
        )(proj, cc2, ct2)
    tq = min(512, t)
    return pl.pallas_call(
        functools.partial(_fnet_kernel, t_len=t, scale=scale),
        grid=(b, N_FOURIER_GROUPS, t // tq),
        in_specs=[pl.BlockSpec((1, t, FOURIER_GW), lambda i, g, q: (i, 0, F_OFF // FOURIER_GW + g)),
                  pl.BlockSpec((FOURIER_GW, 2 * FOURIER_GW), lambda i, g, q: (0, 0)),
                  pl.BlockSpec((tq, 2 * t), lambda i, g, q: (q, 0))],
        out_specs=pl.BlockSpec((1, tq, FOURIER_GW), lambda i, g, q: (i, q, g)),
        out_shape=jax.ShapeDtypeStruct((b, t, RWKV_W), BF16),
        scratch_shapes=[pltpu.VMEM((2 * t, FOURIER_GW), BF16)],
        compiler_params=_params("arbitrary", "arbitrary", "arbitrary"),
        name="fnet_mix",
    )(proj, cc2, ct2)


def _outproj_kernel(yr_ref, yf_ref, w_ref, x_ref, mod_ref, o_ref):
    m = jnp.dot(yr_ref[...], w_ref[0:RWKV_W, :], preferred_element_type=F32)
    m = m + jnp.dot(yf_ref[...], w_ref[RWKV_W:2 * RWKV_W, :], preferred_element_type=F32)
    o_ref[...] = x_ref[...] + mod_ref[0, 2:3, :] * m


def out_proj(yr, yf, w_bf, x2d, mod, rows_per_mod, mod_base):
    m, d = x2d.shape
    tm, tn = min(1024, rows_per_mod), 512
    row_of = _mod_row_map(tm, rows_per_mod, mod_base)
    return pl.pallas_call(
        _outproj_kernel,
        grid=(m // tm, d // tn),
        in_specs=[pl.BlockSpec((tm, RWKV_W), lambda i, j: (i, 0)),
                  pl.BlockSpec((tm, RWKV_W), lambda i, j: (i, 0)),
                  pl.BlockSpec((2 * RWKV_W, tn), lambda i, j: (0, j)),
                  pl.BlockSpec((tm, tn), lambda i, j: (i, j)),
                  pl.BlockSpec((1, N_MOD, tn), lambda i, j: (row_of(i), 0, j))],
        out_specs=pl.BlockSpec((tm, tn), lambda i, j: (i, j)),
        out_shape=jax.ShapeDtypeStruct((m, d), F32),
        compiler_params=_params("arbitrary", "arbitrary"),
        name="out_proj",
    )(yr, yf, w_bf, x2d, mod)


def _norm2_router_kernel(x_ref, mod_ref, g_ref, rwh_ref, rwl_ref, h_ref, aff_ref):
    h = _rms_modulate(x_ref[...], g_ref[...], mod_ref[0, 4:5, :], mod_ref[0, 3:4, :])
    h_ref[...] = h
    h_hi, h_lo = _split2(h)
    logits = (jnp.dot(h_hi, rwh_ref[...], preferred_element_type=F32)
              + jnp.dot(h_hi, rwl_ref[...], preferred_element_type=F32)
              + jnp.dot(h_lo, rwh_ref[...], preferred_element_type=F32))
    z = logits - jnp.max(logits, axis=-1, keepdims=True)
    e = jnp.exp(z)
    aff_ref[...] = e / jnp.sum(e, axis=-1, keepdims=True)


def norm2_router(x2d, mod, norm_g, router_w, rows_per_mod, mod_base):
    m, d = x2d.shape
    tm = ROW_CHUNK
    row_of = _mod_row_map(tm, rows_per_mod, mod_base)
    rw_hi, rw_lo = _split2(router_w)
    return pl.pallas_call(
        _norm2_router_kernel,
        grid=(m // tm,),
        in_specs=[pl.BlockSpec((tm, d), lambda i: (i, 0)),
                  pl.BlockSpec((1, N_MOD, d), lambda i: (row_of(i), 0, 0)),
                  pl.BlockSpec((1, d), lambda i: (0, 0)),
                  pl.BlockSpec((d, N_EXPERTS), lambda i: (0, 0)),
                  pl.BlockSpec((d, N_EXPERTS), lambda i: (0, 0))],
        out_specs=[pl.BlockSpec((tm, d), lambda i: (i, 0)),
                   pl.BlockSpec((tm, N_EXPERTS), lambda i: (i, 0))],
        out_shape=[jax.ShapeDtypeStruct((m, d), F32),
                   jax.ShapeDtypeStruct((m, N_EXPERTS), F32)],
        compiler_params=_params("arbitrary"),
        name="norm2_router",
    )(x2d, mod, norm_g.reshape(1, d), rw_hi, rw_lo)


GATHER_ROWS = 256
PAIR_CHUNK = LANES
COMBINE_ROWS = 256


def _issue_row_gather(idx_ref, base, n_rows, src_hbm, dst, sem):
    def body(r, carry):
        row = idx_ref[base + r]
        pltpu.make_async_copy(src_hbm.at[pl.ds(row, 1), :], dst.at[pl.ds(r, 1), :], sem).start()
        return carry
    lax.fori_loop(0, n_rows, body, 0, unroll=8)


def _wait_row_gather(n_rows, src_hbm, dst, sem):
    pltpu.make_async_copy(src_hbm.at[pl.ds(0, n_rows), :], dst, sem).wait()


def _gather_kernel(idx_ref, h_hbm, o_ref, buf, sem):
    s = pl.program_id(0)
    slot = s % 2

    @pl.when(s == 0)
    def _():
        _issue_row_gather(idx_ref, 0, GATHER_ROWS, h_hbm, buf.at[0], sem.at[0])

    @pl.when(s + 1 < pl.num_programs(0))
    def _():
        _issue_row_gather(idx_ref, (s + 1) * GATHER_ROWS, GATHER_ROWS, h_hbm,
                          buf.at[1 - slot], sem.at[1 - slot])

    _wait_row_gather(GATHER_ROWS, h_hbm, buf.at[slot], sem.at[slot])
    o_ref[...] = buf[slot].astype(BF16)


def gather_tokens(h, idx_flat):
    n, d = h.shape
    rows = idx_flat.shape[0]
    return pl.pallas_call(
        _gather_kernel,
        grid_spec=pltpu.PrefetchScalarGridSpec(
            num_scalar_prefetch=1,
            grid=(rows // GATHER_ROWS,),
            in_specs=[pl.BlockSpec(memory_space=pl.ANY)],
            out_specs=pl.BlockSpec((GATHER_ROWS, d), lambda s, idx: (s, 0)),
            scratch_shapes=[pltpu.VMEM((2, GATHER_ROWS, d), F32), pltpu.SemaphoreType.DMA((2,))]),
        out_shape=jax.ShapeDtypeStruct((rows, d), BF16),
        compiler_params=_params("arbitrary"),
        name="gather_tokens",
    )(idx_flat, h)


def _ffn_accumulate(xe_ref, w1_ref, w3_ref, w2_ref, o_ref, width):
    x = xe_ref[0]
    h1 = jnp.dot(x, w1_ref[0, :, :width].astype(BF16), preferred_element_type=F32)
    h3 = jnp.dot(x, w3_ref[0, :, :width].astype(BF16), preferred_element_type=F32)
    hid = h1 * jax.nn.sigmoid(h1) * h3
    o_ref[0] += jnp.dot(hid.astype(BF16), w2_ref[0, :width, :].astype(BF16),
                        preferred_element_type=F32)


def _expert_kernel(xe_ref, w1_ref, w3_ref, w2_ref, gate_ref, o_ref, *, tf, tail):
    f = pl.program_id(1)
    last = pl.num_programs(1) - 1
    step = functools.partial(_ffn_accumulate, xe_ref, w1_ref, w3_ref, w2_ref, o_ref)

    @pl.when(f == 0)
    def _():
        o_ref[...] = jnp.zeros_like(o_ref)

    if tail == tf:
        step(tf)
    else:
        @pl.when(f < last)
        def _():
            step(tf)

        @pl.when(f == last)
        def _():
            step(tail)

    @pl.when(f == last)
    def _():
        o_ref[0] = o_ref[0] * gate_ref[0]


def expert_ffn(xe, w1, w3, w2, gate):
    e, cap, d = xe.shape
    ff = w1.shape[2]
    tf = 512
    nf = pl.cdiv(ff, tf)
    return pl.pallas_call(
        functools.partial(_expert_kernel, tf=tf, tail=ff - (nf - 1) * tf),
        grid=(e, nf),
        in_specs=[pl.BlockSpec((1, cap, d), lambda i, f: (i, 0, 0), pipeline_mode=pl.Buffered(1)),
                  pl.BlockSpec((1, d, tf), lambda i, f: (i, 0, f)),
                  pl.BlockSpec((1, d, tf), lambda i, f: (i, 0, f)),
                  pl.BlockSpec((1, tf, d), lambda i, f: (i, f, 0)),
                  pl.BlockSpec((1, cap, 1), lambda i, f: (i, 0, 0))],
        out_specs=pl.BlockSpec((1, cap, d), lambda i, f: (i, 0, 0)),
        out_shape=jax.ShapeDtypeStruct((e, cap, d), F32),
        compiler_params=_params("arbitrary", "arbitrary"),
        name="expert_ffn",
    )(xe, w1, w3, w2, gate.reshape(e, cap, 1))


def _combine_final_kernel(rows_ref, ptr_ref, tok_ref, ye_hbm, x_ref, mod_ref, g_ref, o_ref,
                          buf, sem, acc_ref, *, tm):
    i = pl.program_id(0)
    p0 = ptr_ref[i]
    p1 = ptr_ref[i + 1]
    c0 = p0 // PAIR_CHUNK
    c1 = jnp.where(p1 > p0, (p1 - 1) // PAIR_CHUNK + 1, c0)
    acc_ref[...] = jnp.zeros_like(acc_ref)

    @pl.when(c1 > c0)
    def _():
        _issue_row_gather(rows_ref, c0 * PAIR_CHUNK, PAIR_CHUNK, ye_hbm, buf.at[0], sem.at[0])

    def chunk(c, carry):
        slot = (c - c0) % 2

        @pl.when(c + 1 < c1)
        def _():
            _issue_row_gather(rows_ref, (c + 1) * PAIR_CHUNK, PAIR_CHUNK, ye_hbm,
                              buf.at[1 - slot], sem.at[1 - slot])

        _wait_row_gather(PAIR_CHUNK, ye_hbm, buf.at[slot], sem.at[slot])
        local = tok_ref[pl.ds(c, 1), :] - i * tm
        onehot = (lax.broadcasted_iota(jnp.int32, (tm, PAIR_CHUNK), 0) == local).astype(BF16)
        acc_ref[...] += jnp.dot(onehot, buf[slot].astype(BF16), preferred_element_type=F32)
        return carry

    lax.fori_loop(c0, c1, chunk, 0)
    x = x_ref[...] + mod_ref[0, 5:6, :] * acc_ref[...]
    ms = jnp.mean(x * x, axis=-1, keepdims=True)
    o_ref[...] = x * lax.rsqrt(ms + NORM_EPS) * g_ref[...]


def combine_final(x2d, ye_flat, pair_rows, pair_toks, tile_ptr, mod, final_g, rows_per_mod, mod_base):
    m, d = x2d.shape
    tm = COMBINE_ROWS
    n_pairs = pair_rows.shape[0]
    row_of = _mod_row_map(tm, rows_per_mod, mod_base)
    return pl.pallas_call(
        functools.partial(_combine_final_kernel, tm=tm),
        grid_spec=pltpu.PrefetchScalarGridSpec(
            num_scalar_prefetch=2,
            grid=(m // tm,),
            in_specs=[pl.BlockSpec((n_pairs // PAIR_CHUNK, PAIR_CHUNK), lambda i, r, p: (0, 0)),
                      pl.BlockSpec(memory_space=pl.ANY),
                      pl.BlockSpec((tm, d), lambda i, r, p: (i, 0)),
                      pl.BlockSpec((1, N_MOD, d), lambda i, r, p: (row_of(i), 0, 0)),
                      pl.BlockSpec((1, d), lambda i, r, p: (0, 0))],
            out_specs=pl.BlockSpec((tm, d), lambda i, r, p: (i, 0)),
            scratch_shapes=[pltpu.VMEM((2, PAIR_CHUNK, d), F32), pltpu.SemaphoreType.DMA((2,)),
                            pltpu.VMEM((tm, d), F32)]),
        out_shape=jax.ShapeDtypeStruct((m, d), F32),
        compiler_params=_params("arbitrary"),
        name="combine_final",
    )(pair_rows, tile_ptr, pair_toks.reshape(n_pairs // PAIR_CHUNK, PAIR_CHUNK), ye_flat,
      x2d, mod, final_g.reshape(1, d))


def _prepare_in_proj(w_in, shift_mu):
    d = w_in.shape[0]
    rkv, lora, gf = w_in[:, :G_OFF], w_in[:, G_OFF:SHIFT_W], w_in[:, SHIFT_W:]
    w_p = jnp.concatenate([rkv, gf, lora, jnp.zeros((d, LANES), w_in.dtype)], axis=1).astype(BF16)
    mu = jnp.concatenate([shift_mu[:G_OFF], jnp.zeros((2 * RWKV_W,), F32), shift_mu[G_OFF:],
                          jnp.zeros((LANES,), F32)])
    return w_p, mu.reshape(1, IN_PAD_W)


def _mix_inputs(x, mod, rows_per_mod, mod_base, grid_shift, p):
    b, t, d = x.shape
    x2d = x.reshape(b * t, d)
    proj = in_proj(x2d, mod, p["norm1_g"], p["w_in_p"], rows_per_mod, mod_base)
    proj = proj.reshape(b, t, IN_PAD_W)
    r, v, a, bonus, w, kd, bd = pre_scan(proj, p["mu"], p["up_w"], p["up_b"],
                                         p["k_k"], p["k_a"], p["r_k"], grid_shift)
    per_dir = [_scan_rows(w), _scan_rows(kd), _scan_rows(bd)]
    shared = [_scan_rows(a), _scan_rows(r), _scan_rows(v)]
    if 2 * b * N_HEADS == LANES:
        ops = relayout_to_scan(per_dir + shared, [False] * 3 + [True] * 3)
    else:
        ops = (list(relayout_to_scan(per_dir, [False] * 3))
               + list(relayout_to_scan(shared, [False] * 3)))
    return {"x2d": x2d, "proj": proj, "bonus": bonus, "ops": list(ops),
            "mod": (mod, rows_per_mod, mod_base), "shape": (b, t, d)}


def _route(st, y, p):
    b, t, d = st["shape"]
    mod_args = st["mod"]
    yr = post_scan(_from_scan_layout(y, b), st["bonus"], st["proj"], p["gn_w"], p["gn_b"])
    yf = fnet_mix(st["proj"])
    x1 = out_proj(yr.reshape(b * t, RWKV_W), yf.reshape(b * t, RWKV_W), p["w_out_bf"],
                  st["x2d"], *mod_args)
    h2, aff = norm2_router(x1, mod_args[0], p["norm2_g"], p["router_w"], *mod_args[1:])
    n = b * t
    cap = n * CAPACITY_FACTOR // N_EXPERTS
    gate, idx = lax.top_k(aff.T, cap)
    idx_flat = idx.reshape(-1).astype(jnp.int32)
    xe = gather_tokens(h2, idx_flat).reshape(N_EXPERTS, cap, d)
    return {"x1": x1, "xe": xe, "gate": gate, "idx_flat": idx_flat, "mod": mod_args,
            "shape": (b, t, d)}


def _combine(rt, ye, p):
    b, t, d = rt["shape"]
    n = b * t
    idx_flat = rt["idx_flat"]
    pair_toks, pair_rows = lax.sort_key_val(idx_flat, jnp.arange(idx_flat.shape[0], dtype=jnp.int32))
    tile_ptr = jnp.searchsorted(pair_toks, jnp.arange(0, n + 1, COMBINE_ROWS, dtype=jnp.int32),
                                side="left").astype(jnp.int32)
    mod, rows_per_mod, mod_base = rt["mod"]
    out = combine_final(rt["x1"], ye.reshape(-1, d), pair_rows, pair_toks, tile_ptr, mod,
                        p["final_norm_g"], rows_per_mod, mod_base)
    return out.reshape(b, t, d)


def _experts(p):
    return p["exp_w1"], p["exp_w3"], p["exp_w2"]


def _layer(x, mod, rows_per_mod, mod_base, s0, grid_shift, p):
    st = _mix_inputs(x, mod, rows_per_mod, mod_base, grid_shift, p)
    y, s_fin = wkv_scan(st["ops"], s0)
    rt = _route(st, y, p)
    ye = expert_ffn(rt["xe"], *_experts(p), rt["gate"])
    return _combine(rt, ye, p), s_fin


def kernel(x_prompt, x_sample, state_wkv, c, c_ctx, ada_w, ada_b, norm1_g, norm2_g, w_in, shift_mu,
           w_decay_up, w_decay0, w_iclr_up, w_iclr0, k_k, k_a, r_k, gn_w, gn_b, w_out, router_w,
           exp_w1, exp_w3, exp_w2, final_norm_g):
    depth = ada_w.shape[0]
    assert depth == 1, "single-layer trunk"
    l = 0
    bc, tc_len, d = x_prompt.shape
    bl, tl, _ = x_sample.shape
    cvecs = jnp.concatenate([c_ctx[None], c], axis=0)
    mod = ada_mod(cvecs, ada_w[l], ada_b[l]).reshape(1 + bl, N_MOD, d)
    up_w, up_b = _lora_up_tables(w_decay_up[l], w_decay0[l], w_iclr_up[l], w_iclr0[l])
    w_in_p, mu = _prepare_in_proj(w_in[l], shift_mu[l])
    p = {
        "norm1_g": norm1_g[l], "norm2_g": norm2_g[l], "final_norm_g": final_norm_g,
        "w_in_p": w_in_p, "mu": mu, "up_w": up_w, "up_b": up_b,
        "k_k": k_k[l], "k_a": k_a[l], "r_k": r_k[l], "gn_w": gn_w[l], "gn_b": gn_b[l],
        "w_out_bf": w_out[l].astype(BF16), "router_w": router_w[l],
        "exp_w1": exp_w1[l], "exp_w3": exp_w3[l], "exp_w2": exp_w2[l],
    }
    groups_c = 2 * bc * N_HEADS // LANES
    s0_ctx = jnp.zeros((groups_c, HEAD_N, HEAD_N, LANES), F32)
    ctx = _mix_inputs(x_prompt, mod, bc * tc_len, 0, False, p)
    y_ctx, s_ctx = wkv_scan(ctx["ops"], s0_ctx)
    ctx_rt = _route(ctx, y_ctx, p)
    s_ctx = s_ctx.transpose(1, 2, 0, 3).reshape(HEAD_N, HEAD_N, 2, bc, N_HEADS)
    new_state = s_ctx.transpose(3, 2, 4, 1, 0)[:, None]
    groups_l = 2 * bl * N_HEADS // LANES
    s0_lat = state_wkv[:, l].astype(F32).transpose(4, 3, 1, 0, 2)
    s0_lat = s0_lat.reshape(HEAD_N, HEAD_N, groups_l, LANES).transpose(2, 0, 1, 3)
    lat = _mix_inputs(x_sample, mod, tl, 1, True, p)
    ye_ctx = expert_ffn(ctx_rt["xe"], *_experts(p), ctx_rt["gate"])
    y_lat, _ = wkv_scan(lat["ops"], s0_lat)
    lat_rt = _route(lat, y_lat, p)
    ye_lat = expert_ffn(lat_rt["xe"], *_experts(p), lat_rt["gate"])
    return (_combine(ctx_rt, ye_ctx, p), _combine(lat_rt, ye_lat, p), new_state)
```

```python
import functools
import math

import jax
import jax.numpy as jnp
from jax import lax
from jax.experimental import pallas as pl
from jax.experimental.pallas import tpu as pltpu

F32 = jnp.float32
BF16 = jnp.bfloat16

D_MODEL = 2048
RWKV_W = 1024
HEAD_N = 64
N_HEADS = 16
LORA_R = 96
N_EXPERTS = 16
EXPERT_FF = 5504
CAPACITY_FACTOR = 2
N_MOD = 6
NORM_EPS = 1e-6
GN_EPS = 64e-5
FOURIER_GW = 256
N_FOURIER_GROUPS = 4

LANES = 128
HEAD_TILES = RWKV_W // LANES
SHIFT_W = 3 * RWKV_W + 4 * LORA_R
LORA_W = 4 * LORA_R
LORA_BLK = LORA_W + LANES
G_OFF = 3 * RWKV_W
F_OFF = G_OFF + RWKV_W
LORA_OFF = F_OFF + RWKV_W
IN_PAD_W = LORA_OFF + LORA_BLK
ROW_CHUNK = 256
VMEM_LIMIT = 56 * 1024 * 1024
DECAY_SCALE = math.exp(-0.5)


def _params(*sem):
    return pltpu.CompilerParams(dimension_semantics=sem, vmem_limit_bytes=VMEM_LIMIT)


def _seg_ones():
    r = lax.broadcasted_iota(jnp.int32, (LANES, LANES), 0) // HEAD_N
    c = lax.broadcasted_iota(jnp.int32, (LANES, LANES), 1) // HEAD_N
    return (r == c).astype(BF16)


def _split2(x):
    hi = x.astype(BF16)
    return hi, (x - hi.astype(F32)).astype(BF16)


def _split3(x):
    hi = x.astype(BF16)
    rem = x - hi.astype(F32)
    mid = rem.astype(BF16)
    return hi, mid, (rem - mid.astype(F32)).astype(BF16)


def _seg_sum(x, seg):
    return sum(jnp.dot(p, seg, preferred_element_type=F32) for p in _split3(x))


def _ada_kernel(c_ref, w_ref, b_ref, o_ref):
    c = c_ref[...]
    s = c * jax.nn.sigmoid(c)
    o_ref[...] = jnp.dot(s.astype(BF16), w_ref[...].astype(BF16),
                         preferred_element_type=F32) + b_ref[...]


def ada_mod(cvecs, ada_w, ada_b):
    rows, d = cvecs.shape
    n = ada_w.shape[1]
    tn = 1024
    return pl.pallas_call(
        _ada_kernel,
        grid=(n // tn,),
        in_specs=[pl.BlockSpec((rows, d), lambda j: (0, 0)),
                  pl.BlockSpec((d, tn), lambda j: (0, j)),
                  pl.BlockSpec((1, tn), lambda j: (0, j))],
        out_specs=pl.BlockSpec((rows, tn), lambda j: (0, j)),
        out_shape=jax.ShapeDtypeStruct((rows, n), F32),
        compiler_params=_params("arbitrary"),
        name="ada_mod",
    )(cvecs, ada_w, ada_b.reshape(1, n))


def _mod_row_map(tm, rows_per_mod, mod_base):
    return lambda i: mod_base + (i * tm) // rows_per_mod


def _rms_modulate(x, g, scale, shift):
    ms = jnp.mean(x * x, axis=-1, keepdims=True)
    return x * lax.rsqrt(ms + NORM_EPS) * g * (1.0 + scale) + shift


def _inproj_kernel(x_ref, mod_ref, g_ref, w_ref, o_ref, h_ref, *, tm):
    @pl.when(pl.program_id(1) == 0)
    def _():
        def body(c, carry):
            rows = pl.ds(pl.multiple_of(c * ROW_CHUNK, ROW_CHUNK), ROW_CHUNK)
            h = _rms_modulate(x_ref[rows, :], g_ref[...], mod_ref[0, 1:2, :], mod_ref[0, 0:1, :])
            h_ref[rows, :] = h.astype(BF16)
            return carry
        lax.fori_loop(0, tm // ROW_CHUNK, body, 0)

    o_ref[...] = jnp.dot(h_ref[...], w_ref[...], preferred_element_type=F32)


def in_proj(x2d, mod, norm_g, w_p, rows_per_mod, mod_base):
    m, d = x2d.shape
    n = w_p.shape[1]
    tm, tn = min(1024, rows_per_mod), 512
    row_of = _mod_row_map(tm, rows_per_mod, mod_base)
    return pl.pallas_call(
        functools.partial(_inproj_kernel, tm=tm),
        grid=(m // tm, n // tn),
        in_specs=[pl.BlockSpec((tm, d), lambda i, j: (i, 0)),
                  pl.BlockSpec((1, N_MOD, d), lambda i, j: (row_of(i), 0, 0)),
                  pl.BlockSpec((1, d), lambda i, j: (0, 0)),
                  pl.BlockSpec((d, tn), lambda i, j: (0, j))],
        out_specs=pl.BlockSpec((tm, tn), lambda i, j: (i, j)),
        out_shape=jax.ShapeDtypeStruct((m, n), F32),
        scratch_shapes=[pltpu.VMEM((tm, d), BF16)],
        compiler_params=_params("arbitrary", "arbitrary"),
        name="in_proj",
    )(x2d, mod, norm_g.reshape(1, d), w_p)


def _shifted(ref, mu, c, n_chunks, t_len, grid_shift):
    ch = ROW_CHUNK
    base = pl.multiple_of(c * ch, ch)
    cur = ref[0, pl.ds(base, ch), :]
    row = lax.broadcasted_iota(jnp.int32, (ch, 1), 0)
    before = pltpu.roll(cur, 1, 0)
    after = pltpu.roll(cur, ch - 1, 0)
    if not grid_shift:
        prev = jnp.where(row == 0, 0.0, before)
        nxt = jnp.where(row == ch - 1, 0.0, after)
        mixed = 0.5 * (prev + nxt)
    else:
        gw = 64
        left = jnp.where(row % gw == 0, 0.0, before)
        right = jnp.where(row % gw == gw - 1, 0.0, after)
        up_start = pl.multiple_of(jnp.maximum(base - gw, 0), gw)
        dn_start = pl.multiple_of(jnp.minimum(base + ch, t_len - gw), gw)
        up_halo = jnp.where(c > 0, ref[0, pl.ds(up_start, gw), :], 0.0)
        dn_halo = jnp.where(c < n_chunks - 1, ref[0, pl.ds(dn_start, gw), :], 0.0)
        up = jnp.concatenate([up_halo, cur[: ch - gw]], axis=0)
        down = jnp.concatenate([cur[gw:], dn_halo], axis=0)
        mixed = 0.25 * (up + down + left + right)
    return cur + mu * (mixed - cur)


def _prescan_kernel(r_ref, k_ref, v_ref, lora_ref, mur_ref, muk_ref, muv_ref, mul_ref,
                    uph_ref, upl_ref, bias_ref, kk_ref, ka_ref, rk_ref,
                    r_o, v_o, a_o, bonus_o, w_o, kd_o, bd_o, lh_scr, ll_scr, *, t_len, grid_shift):
    n_chunks = t_len // ROW_CHUNK
    seg = _seg_ones()
    c = pl.program_id(2)
    rows = pl.ds(pl.multiple_of(c * ROW_CHUNK, ROW_CHUNK), ROW_CHUNK)
    sh = functools.partial(_shifted, c=c, n_chunks=n_chunks, t_len=t_len, grid_shift=grid_shift)

    @pl.when(pl.program_id(1) == 0)
    def _():
        lane = lax.broadcasted_iota(jnp.int32, (1, LORA_BLK), 1)
        lora = sh(lora_ref, mul_ref[...])
        lora = jnp.where(lane < 2 * LORA_R, jnp.tanh(lora), lora)
        hi, lo = _split2(lora)
        lh_scr[rows, :] = hi
        ll_scr[rows, :] = lo

    r = sh(r_ref, mur_ref[...])
    k = sh(k_ref, muk_ref[...])
    v = sh(v_ref, muv_ref[...])
    l_hi = lh_scr[rows, :]
    l_lo = ll_scr[rows, :]
    raw = (jnp.dot(l_hi, uph_ref[0], preferred_element_type=F32)
           + jnp.dot(l_hi, upl_ref[0], preferred_element_type=F32)
           + jnp.dot(l_lo, uph_ref[0], preferred_element_type=F32)) + bias_ref[0]
    kk = k * kk_ref[...]
    kk = kk * lax.rsqrt(_seg_sum(kk * kk, seg) + 1e-12)
    kd_sum = jnp.zeros_like(k)
    for d in range(2):
        decay = jnp.exp(-DECAY_SCALE * jax.nn.sigmoid(raw[:, d * LANES:(d + 1) * LANES]))
        iclr = jax.nn.sigmoid(raw[:, (2 + d) * LANES:(3 + d) * LANES])
        kd = k * (1.0 + (iclr - 1.0) * ka_ref[...])
        w_o[d, 0] = decay.T
        kd_o[d, 0] = kd.T
        bd_o[d, 0] = (kk * iclr).T
        kd_sum = kd_sum + kd
    r_o[0] = r.T
    v_o[0] = v.T
    a_o[0] = (-kk).T
    bonus_o[0] = _seg_sum(r * (0.5 * kd_sum) * rk_ref[...], seg) * v


def pre_scan(proj, mu, up_w, up_b, k_k, k_a, r_k, grid_shift):
    b, t, _ = proj.shape
    up_hi, up_lo = _split2(up_w)
    if not grid_shift:
        assert t == ROW_CHUNK, "sequence shift handles one chunk per sequence"
    assert t % ROW_CHUNK == 0
    col = lambda off: pl.BlockSpec((1, t, LANES), lambda i, j, c: (i, 0, off + j))
    vec = lambda off: pl.BlockSpec((1, LANES), lambda i, j, c: (0, off + j))
    out_t = pl.BlockSpec((1, LANES, ROW_CHUNK), lambda i, j, c: (i, j, c))
    out_t2 = pl.BlockSpec((2, 1, LANES, ROW_CHUNK), lambda i, j, c: (0, i, j, c))
    s_t = jax.ShapeDtypeStruct((b, RWKV_W, t), F32)
    s_t2 = jax.ShapeDtypeStruct((2, b, RWKV_W, t), F32)
    return pl.pallas_call(
        functools.partial(_prescan_kernel, t_len=t, grid_shift=grid_shift),
        grid=(b, HEAD_TILES, t // ROW_CHUNK),
        in_specs=[col(0), col(HEAD_TILES), col(2 * HEAD_TILES),
                  pl.BlockSpec((1, t, LORA_BLK), lambda i, j, c: (i, 0, LORA_OFF // LORA_BLK)),
                  vec(0), vec(HEAD_TILES), vec(2 * HEAD_TILES),
                  pl.BlockSpec((1, LORA_BLK), lambda i, j, c: (0, LORA_OFF // LORA_BLK)),
                  pl.BlockSpec((1, LORA_BLK, 4 * LANES), lambda i, j, c: (j, 0, 0)),
                  pl.BlockSpec((1, LORA_BLK, 4 * LANES), lambda i, j, c: (j, 0, 0)),
                  pl.BlockSpec((1, 1, 4 * LANES), lambda i, j, c: (j, 0, 0)),
                  vec(0), vec(0), vec(0)],
        out_specs=[out_t, out_t, out_t,
                   pl.BlockSpec((1, ROW_CHUNK, LANES), lambda i, j, c: (i, c, j)),
                   out_t2, out_t2, out_t2],
        out_shape=[s_t, s_t, s_t, jax.ShapeDtypeStruct((b, t, RWKV_W), F32), s_t2, s_t2, s_t2],
        scratch_shapes=[pltpu.VMEM((t, LORA_BLK), BF16), pltpu.VMEM((t, LORA_BLK), BF16)],
        compiler_params=_params("arbitrary", "arbitrary", "arbitrary"),
        name="pre_scan",
    )(proj, proj, proj, proj, mu, mu, mu, mu, up_hi, up_lo, up_b,
      k_k.reshape(1, RWKV_W), k_a.reshape(1, RWKV_W), r_k.reshape(1, RWKV_W))


def _lora_up_tables(w_decay_up, w_decay0, w_iclr_up, w_iclr0):
    mats = [w_decay_up[0], w_decay_up[1], w_iclr_up[0], w_iclr_up[1]]
    bias = [w_decay0[0], w_decay0[1], w_iclr0[0], w_iclr0[1]]
    up = jnp.zeros((HEAD_TILES, LORA_BLK, 4 * LANES), F32)
    for q, m in enumerate(mats):
        blk = m.astype(F32).reshape(LORA_R, HEAD_TILES, LANES).transpose(1, 0, 2)
        up = up.at[:, q * LORA_R:(q + 1) * LORA_R, q * LANES:(q + 1) * LANES].set(blk)
    b = jnp.stack([v.astype(F32).reshape(HEAD_TILES, LANES) for v in bias], axis=1)
    return up, b.reshape(HEAD_TILES, 1, 4 * LANES)


def _scan_chunk(ins, y_ref, yb_ref, s_scr, wr_scr, *, tc, backward=None):
    mixed = yb_ref is not None
    tt = (lambda t: t) if mixed else (lambda t: jnp.where(backward, tc - 1 - t, t))

    def view(i):
        ref = ins[i]
        return (lambda t: ref[tt(t)]), (lambda t, k: ref[tt(t), k:k + 1, :])

    (w_full, w_row), (k_full, k_row), (b_full, b_row), (_, a_row), (r_full, _), (v_full, _) = (
        view(i) for i in range(6))

    wr_scr[0] = w_full(0) * r_full(0)
    sa = jnp.zeros((HEAD_N, LANES), F32)
    y0 = jnp.zeros((HEAD_N, LANES), F32)
    for k in range(HEAD_N):
        s = s_scr[k]
        sa = sa + s * a_row(0, k)
        y0 = y0 + s * wr_scr[0, k:k + 1, :]

    def step(t, carry):
        sa, y0 = carry
        r_t = r_full(t)
        v_t = v_full(t)
        br = jnp.sum(b_full(t) * r_t, axis=0, keepdims=True)
        kr = jnp.sum(k_full(t) * r_t, axis=0, keepdims=True)
        y = y0 + sa * br + v_t * kr
        y_ref[tt(t)] = y
        if mixed:
            yb_ref[tc - 1 - t] = y
        tn = jnp.minimum(t + 1, tc - 1)
        slot = (t + 1) % 2
        wr_scr[slot] = w_full(tn) * r_full(tn)
        sa_n = jnp.zeros((HEAD_N, LANES), F32)
        y0_n = jnp.zeros((HEAD_N, LANES), F32)
        for k in range(HEAD_N):
            s = s_scr[k] * w_row(t, k) + sa * b_row(t, k) + v_t * k_row(t, k)
            s_scr[k] = s
            sa_n = sa_n + s * a_row(tn, k)
            y0_n = y0_n + s * wr_scr[slot, k:k + 1, :]
        return sa_n, y0_n

    lax.fori_loop(0, tc, step, (sa, y0))


def _scan_kernel(*refs, tc, mixed):
    ins, rest = refs[:6], refs[6:]
    if mixed:
        s0_ref, y_ref, yb_ref, sfin_ref, s_scr, wr_scr = rest
        backward = None
    else:
        s0_ref, y_ref, sfin_ref, s_scr, wr_scr = rest
        yb_ref = None
        backward = pl.program_id(0) >= pl.num_programs(0) // 2
    c = pl.program_id(1)

    @pl.when(c == 0)
    def _():
        s_scr[...] = s0_ref[0]

    _scan_chunk(ins, y_ref, yb_ref, s_scr, wr_scr, tc=tc, backward=backward)

    @pl.when(c == pl.num_programs(1) - 1)
    def _():
        sfin_ref[0] = s_scr[...]


def wkv_scan(ops, s0):
    t, _, lanes = ops[0].shape
    groups = lanes // LANES
    mixed = groups == 1
    tc = 32 if mixed else 64
    nc = t // tc
    st = pl.BlockSpec((1, HEAD_N, HEAD_N, LANES), lambda g, c: (g, 0, 0, 0))
    y_shape = jax.ShapeDtypeStruct((t, HEAD_N, lanes), F32)
    if mixed:
        fwd = pl.BlockSpec((tc, HEAD_N, LANES), lambda g, c: (c, 0, g))
        bwd = pl.BlockSpec((tc, HEAD_N, LANES), lambda g, c: (nc - 1 - c, 0, g))
        in_specs, operands = [fwd] * 6 + [st], list(ops) + [s0]
        out_specs, out_shape = [fwd, bwd, st], [y_shape, y_shape]
    else:
        half_g = groups // 2
        tmap = lambda g, c: jnp.where(g >= half_g, nc - 1 - c, c)
        seq = pl.BlockSpec((tc, HEAD_N, LANES), lambda g, c: (tmap(g, c), 0, g))
        shared = pl.BlockSpec((tc, HEAD_N, LANES), lambda g, c: (tmap(g, c), 0, g % half_g))
        in_specs = [seq if o.shape[2] == lanes else shared for o in ops] + [st]
        operands = list(ops) + [s0]
        out_specs, out_shape = [seq, st], [y_shape]
    out_shape.append(jax.ShapeDtypeStruct((groups, HEAD_N, HEAD_N, LANES), F32))
    res = pl.pallas_call(
        functools.partial(_scan_kernel, tc=tc, mixed=mixed),
        grid=(groups, nc),
        in_specs=in_specs,
        out_specs=out_specs,
        out_shape=out_shape,
        scratch_shapes=[pltpu.VMEM((HEAD_N, HEAD_N, LANES), F32),
                        pltpu.VMEM((2, HEAD_N, LANES), F32)],
        compiler_params=_params("arbitrary", "arbitrary"),
        name="wkv_scan",
    )(*operands)
    half = lanes // 2
    if mixed:
        y_f, y_b, s_fin = res
        return y_f[..., :half] + y_b[..., half:], s_fin
    y, s_fin = res
    return y[..., :half] + y[..., half:], s_fin


K_BLOCK = 8


def _relayout_kernel(*refs, n_ops, rb):
    ins, outs, scr = refs[:n_ops], refs[n_ops:2 * n_ops], refs[2 * n_ops]
    for x_ref, o_ref in zip(ins, outs):
        x2 = x_ref.reshape(rb * K_BLOCK, LANES)
        for r0 in range(0, rb, LANES):
            for kk in range(K_BLOCK):
                m = x2[pl.ds(r0 * K_BLOCK + kk, LANES, stride=K_BLOCK), :]
                scr[pl.ds(kk, LANES, stride=K_BLOCK), :] = m.T
            o_ref[:, :, r0:r0 + LANES] = scr[...].reshape(LANES, K_BLOCK, LANES)


def relayout_to_scan(xs):
    rows, _, t = xs[0].shape
    assert rows % LANES == 0
    rb = min(rows, 512)
    n = len(xs)
    return pl.pallas_call(
        functools.partial(_relayout_kernel, n_ops=n, rb=rb),
        grid=(rows // rb, HEAD_N // K_BLOCK, t // LANES),
        in_specs=[pl.BlockSpec((rb, K_BLOCK, LANES), lambda r, k, c: (r, k, c))] * n,
        out_specs=[pl.BlockSpec((LANES, K_BLOCK, rb), lambda r, k, c: (c, k, r))] * n,
        out_shape=[jax.ShapeDtypeStruct((t, HEAD_N, rows), F32)] * n,
        scratch_shapes=[pltpu.VMEM((LANES * K_BLOCK, LANES), F32)],
        compiler_params=_params("arbitrary", "arbitrary", "arbitrary"),
        name="relayout_to_scan",
    )(*xs)


def _relayout_merged_kernel(*refs, rbs):
    n = len(rbs)
    fw, bw, outs, scr = refs[:n], refs[n:2 * n], refs[2 * n:3 * n], refs[3 * n]
    half = LANES // 2
    r = lax.broadcasted_iota(jnp.int32, (LANES, LANES), 0)
    c = lax.broadcasted_iota(jnp.int32, (LANES, LANES), 1)
    flip = (r + c == LANES - 1).astype(BF16)
    for f_ref, b_ref, o_ref, rb in zip(fw, bw, outs, rbs):
        f2 = f_ref.reshape(rb * K_BLOCK, LANES)
        b2 = b_ref.reshape(rb * K_BLOCK, LANES)
        b_row0 = rb - half
        for kk in range(K_BLOCK):
            m_f = f2[pl.ds(kk, half, stride=K_BLOCK), :]
            m_b = b2[pl.ds(b_row0 * K_BLOCK + kk, half, stride=K_BLOCK), :]
            m_b = sum(jnp.dot(part, flip, preferred_element_type=F32) for part in _split3(m_b))
            scr[pl.ds(kk, LANES, stride=K_BLOCK), :] = jnp.concatenate([m_f, m_b], axis=0).T
        o_ref[...] = scr[...].reshape(LANES, K_BLOCK, LANES)


def relayout_merged(xs):
    t = xs[0].shape[2]
    nt = t // LANES
    rbs = tuple(x.shape[0] for x in xs)
    assert all(rb in (LANES, LANES // 2) for rb in rbs)
    fwd = [pl.BlockSpec((rb, K_BLOCK, LANES), lambda k, c: (0, k, c)) for rb in rbs]
    bwd = [pl.BlockSpec((rb, K_BLOCK, LANES), lambda k, c: (0, k, nt - 1 - c)) for rb in rbs]
    n = len(xs)
    return pl.pallas_call(
        functools.partial(_relayout_merged_kernel, rbs=rbs),
        grid=(HEAD_N // K_BLOCK, nt),
        in_specs=fwd + bwd,
        out_specs=[pl.BlockSpec((LANES, K_BLOCK, LANES), lambda k, c: (c, k, 0))] * n,
        out_shape=[jax.ShapeDtypeStruct((t, HEAD_N, LANES), F32)] * n,
        scratch_shapes=[pltpu.VMEM((LANES * K_BLOCK, LANES), F32)],
        compiler_params=_params("arbitrary", "arbitrary"),
        name="relayout_merged",
    )(*xs, *xs)


def _scan_rows(x):
    return x.reshape(-1, HEAD_N, x.shape[-1])


def _from_scan_layout(y, b):
    t = y.shape[0]
    return y.reshape(t, HEAD_N, b, N_HEADS).transpose(2, 0, 3, 1).reshape(b, t, RWKV_W)


def _postscan_kernel(y_ref, bonus_ref, g_ref, gw_ref, gb_ref, o_ref):
    seg = _seg_ones()
    for j in range(HEAD_TILES):
        cols = slice(j * LANES, (j + 1) * LANES)
        y = y_ref[0, :, cols]
        mu = _seg_sum(y, seg) * (1.0 / HEAD_N)
        d = y - mu
        var = _seg_sum(d * d, seg) * (1.0 / HEAD_N)
        yn = d * lax.rsqrt(var + GN_EPS) * gw_ref[:, cols] + gb_ref[:, cols]
        o_ref[0, :, cols] = ((yn + bonus_ref[0, :, cols]) * jax.nn.sigmoid(g_ref[0, :, cols])).astype(BF16)


def post_scan(y, bonus, proj, gn_w, gn_b):
    b, t, _ = y.shape
    ch = ROW_CHUNK
    blk = pl.BlockSpec((1, ch, RWKV_W), lambda i, c: (i, c, 0))
    vec = pl.BlockSpec((1, RWKV_W), lambda i, c: (0, 0))
    return pl.pallas_call(
        _postscan_kernel,
        grid=(b, t // ch),
        in_specs=[blk, blk,
                  pl.BlockSpec((1, ch, RWKV_W), lambda i, c: (i, c, G_OFF // RWKV_W)),
                  vec, vec],
        out_specs=blk,
        out_shape=jax.ShapeDtypeStruct((b, t, RWKV_W), BF16),
        compiler_params=_params("arbitrary", "arbitrary"),
        name="post_scan",
    )(y, bonus, proj, gn_w.reshape(1, RWKV_W), gn_b.reshape(1, RWKV_W))


def _fnet_kernel(f_ref, cc_ref, ct_ref, o_ref, xcs_ref, *, t_len, scale):
    @pl.when(pl.program_id(2) == 0)
    def _():
        xc = jnp.dot(f_ref[0].astype(BF16), cc_ref[...], preferred_element_type=F32)
        xcs_ref[0:t_len, :] = xc[:, :FOURIER_GW].astype(BF16)
        xcs_ref[t_len:2 * t_len, :] = xc[:, FOURIER_GW:].astype(BF16)

    out = jnp.dot(ct_ref[...], xcs_ref[...], preferred_element_type=F32)
    o_ref[0] = (out * scale).astype(BF16)


def _fnet_short_kernel(f_ref, cc_ref, ct_ref, o_ref, *, t_len, scale):
    gw = FOURIER_GW
    for g in range(N_FOURIER_GROUPS):
        cols = slice(g * gw, (g + 1) * gw)
        xc = jnp.dot(f_ref[0, :, cols].astype(BF16), cc_ref[...], preferred_element_type=F32)
        out = (jnp.dot(ct_ref[:, :t_len], xc[:, :gw].astype(BF16), preferred_element_type=F32)
               + jnp.dot(ct_ref[:, t_len:], xc[:, gw:].astype(BF16), preferred_element_type=F32))
        o_ref[0, :, cols] = (out * scale).astype(BF16)


def _dft_tables(n):
    j = lax.broadcasted_iota(jnp.int32, (n, n), 0)
    k = lax.broadcasted_iota(jnp.int32, (n, n), 1)
    ang = ((j * k) % n).astype(F32) * (2.0 * math.pi / n)
    return jnp.cos(ang), jnp.sin(ang)


def fnet_mix(proj):
    b, t, _ = proj.shape
    cc, sc = _dft_tables(FOURIER_GW)
    ct, st = _dft_tables(t)
    cc2 = jnp.concatenate([cc, sc], axis=1).astype(BF16)
    ct2 = jnp.concatenate([ct, -st], axis=1).astype(BF16)
    scale = 1.0 / math.sqrt(t * FOURIER_GW)
    if t <= 512:
        blk = pl.BlockSpec((1, t, RWKV_W), lambda i: (i, 0, 0))
        return pl.pallas_call(
            functools.partial(_fnet_short_kernel, t_len=t, scale=scale),
            grid=(b,),
            in_specs=[pl.BlockSpec((1, t, RWKV_W), lambda i: (i, 0, F_OFF // RWKV_W)),
                      pl.BlockSpec((FOURIER_GW, 2 * FOURIER_GW), lambda i: (0, 0)),
                      pl.BlockSpec((t, 2 * t), lambda i: (0, 0))],
            out_specs=blk,
            out_shape=jax.ShapeDtypeStruct((b, t, RWKV_W), BF16),
            compiler_params=_params("arbitrary"),
            name="fnet_mix",
        )(proj, cc2, ct2)
    tq = min(512, t)
    return pl.pallas_call(
        functools.partial(_fnet_kernel, t_len=t, scale=scale),
        grid=(b, N_FOURIER_GROUPS, t // tq),
        in_specs=[pl.BlockSpec((1, t, FOURIER_GW), lambda i, g, q: (i, 0, F_OFF // FOURIER_GW + g)),
                  pl.BlockSpec((FOURIER_GW, 2 * FOURIER_GW), lambda i, g, q: (0, 0)),
                  pl.BlockSpec((tq, 2 * t), lambda i, g, q: (q, 0))],
        out_specs=pl.BlockSpec((1, tq, FOURIER_GW), lambda i, g, q: (i, q, g)),
        out_shape=jax.ShapeDtypeStruct((b, t, RWKV_W), BF16),
        scratch_shapes=[pltpu.VMEM((2 * t, FOURIER_GW), BF16)],
        compiler_params=_params("arbitrary", "arbitrary", "arbitrary"),
        name="fnet_mix",
    )(proj, cc2, ct2)


def _outproj_kernel(yr_ref, yf_ref, w_ref, x_ref, mod_ref, o_ref):
    m = jnp.dot(yr_ref[...], w_ref[0:RWKV_W, :], preferred_element_type=F32)
    m = m + jnp.dot(yf_ref[...], w_ref[RWKV_W:2 * RWKV_W, :], preferred_element_type=F32)
    o_ref[...] = x_ref[...] + mod_ref[0, 2:3, :] * m


def out_proj(yr, yf, w_bf, x2d, mod, rows_per_mod, mod_base):
    m, d = x2d.shape
    tm, tn = min(1024, rows_per_mod), 512
    row_of = _mod_row_map(tm, rows_per_mod, mod_base)
    return pl.pallas_call(
        _outproj_kernel,
        grid=(m // tm, d // tn),
        in_specs=[pl.BlockSpec((tm, RWKV_W), lambda i, j: (i, 0)),
                  pl.BlockSpec((tm, RWKV_W), lambda i, j: (i, 0)),
                  pl.BlockSpec((2 * RWKV_W, tn), lambda i, j: (0, j)),
                  pl.BlockSpec((tm, tn), lambda i, j: (i, j)),
                  pl.BlockSpec((1, N_MOD, tn), lambda i, j: (row_of(i), 0, j))],
        out_specs=pl.BlockSpec((tm, tn), lambda i, j: (i, j)),
        out_shape=jax.ShapeDtypeStruct((m, d), F32),
        compiler_params=_params("arbitrary", "arbitrary"),
        name="out_proj",
    )(yr, yf, w_bf, x2d, mod)


def _norm2_router_kernel(x_ref, mod_ref, g_ref, rwh_ref, rwl_ref, h_ref, aff_ref):
    h = _rms_modulate(x_ref[...], g_ref[...], mod_ref[0, 4:5, :], mod_ref[0, 3:4, :])
    h_ref[...] = h
    h_hi, h_lo = _split2(h)
    logits = (jnp.dot(h_hi, rwh_ref[...], preferred_element_type=F32)
              + jnp.dot(h_hi, rwl_ref[...], preferred_element_type=F32)
              + jnp.dot(h_lo, rwh_ref[...], preferred_element_type=F32))
    z = logits - jnp.max(logits, axis=-1, keepdims=True)
    e = jnp.exp(z)
    aff_ref[...] = e / jnp.sum(e, axis=-1, keepdims=True)


def norm2_router(x2d, mod, norm_g, router_w, rows_per_mod, mod_base):
    m, d = x2d.shape
    tm = ROW_CHUNK
    row_of = _mod_row_map(tm, rows_per_mod, mod_base)
    rw_hi, rw_lo = _split2(router_w)
    return pl.pallas_call(
        _norm2_router_kernel,
        grid=(m // tm,),
        in_specs=[pl.BlockSpec((tm, d), lambda i: (i, 0)),
                  pl.BlockSpec((1, N_MOD, d), lambda i: (row_of(i), 0, 0)),
                  pl.BlockSpec((1, d), lambda i: (0, 0)),
                  pl.BlockSpec((d, N_EXPERTS), lambda i: (0, 0)),
                  pl.BlockSpec((d, N_EXPERTS), lambda i: (0, 0))],
        out_specs=[pl.BlockSpec((tm, d), lambda i: (i, 0)),
                   pl.BlockSpec((tm, N_EXPERTS), lambda i: (i, 0))],
        out_shape=[jax.ShapeDtypeStruct((m, d), F32),
                   jax.ShapeDtypeStruct((m, N_EXPERTS), F32)],
        compiler_params=_params("arbitrary"),
        name="norm2_router",
    )(x2d, mod, norm_g.reshape(1, d), rw_hi, rw_lo)


GATHER_ROWS = 256
PAIR_CHUNK = LANES
COMBINE_ROWS = 256


def _issue_row_gather(idx_ref, base, n_rows, src_hbm, dst, sem):
    def body(r, carry):
        row = idx_ref[base + r]
        pltpu.make_async_copy(src_hbm.at[pl.ds(row, 1), :], dst.at[pl.ds(r, 1), :], sem).start()
        return carry
    lax.fori_loop(0, n_rows, body, 0, unroll=8)


def _wait_row_gather(n_rows, src_hbm, dst, sem):
    pltpu.make_async_copy(src_hbm.at[pl.ds(0, n_rows), :], dst, sem).wait()


def _gather_kernel(idx_ref, h_hbm, o_ref, buf, sem):
    s = pl.program_id(0)
    slot = s % 2

    @pl.when(s == 0)
    def _():
        _issue_row_gather(idx_ref, 0, GATHER_ROWS, h_hbm, buf.at[0], sem.at[0])

    @pl.when(s + 1 < pl.num_programs(0))
    def _():
        _issue_row_gather(idx_ref, (s + 1) * GATHER_ROWS, GATHER_ROWS, h_hbm,
                          buf.at[1 - slot], sem.at[1 - slot])

    _wait_row_gather(GATHER_ROWS, h_hbm, buf.at[slot], sem.at[slot])
    o_ref[...] = buf[slot].astype(BF16)


def gather_tokens(h, idx_flat):
    n, d = h.shape
    rows = idx_flat.shape[0]
    return pl.pallas_call(
        _gather_kernel,
        grid_spec=pltpu.PrefetchScalarGridSpec(
            num_scalar_prefetch=1,
            grid=(rows // GATHER_ROWS,),
            in_specs=[pl.BlockSpec(memory_space=pl.ANY)],
            out_specs=pl.BlockSpec((GATHER_ROWS, d), lambda s, idx: (s, 0)),
            scratch_shapes=[pltpu.VMEM((2, GATHER_ROWS, d), F32), pltpu.SemaphoreType.DMA((2,))]),
        out_shape=jax.ShapeDtypeStruct((rows, d), BF16),
        compiler_params=_params("arbitrary"),
        name="gather_tokens",
    )(idx_flat, h)


def _ffn_accumulate(xe_ref, w1_ref, w3_ref, w2_ref, o_ref, width):
    x = xe_ref[0]
    h1 = jnp.dot(x, w1_ref[0, :, :width].astype(BF16), preferred_element_type=F32)
    h3 = jnp.dot(x, w3_ref[0, :, :width].astype(BF16), preferred_element_type=F32)
    hid = h1 * jax.nn.sigmoid(h1) * h3
    o_ref[0] += jnp.dot(hid.astype(BF16), w2_ref[0, :width, :].astype(BF16),
                        preferred_element_type=F32)


def _expert_kernel(xe_ref, w1_ref, w3_ref, w2_ref, gate_ref, o_ref, *, tf, tail):
    f = pl.program_id(1)
    last = pl.num_programs(1) - 1
    step = functools.partial(_ffn_accumulate, xe_ref, w1_ref, w3_ref, w2_ref, o_ref)

    @pl.when(f == 0)
    def _():
        o_ref[...] = jnp.zeros_like(o_ref)

    if tail == tf:
        step(tf)
    else:
        @pl.when(f < last)
        def _():
            step(tf)

        @pl.when(f == last)
        def _():
            step(tail)

    @pl.when(f == last)
    def _():
        o_ref[0] = o_ref[0] * gate_ref[0]


def expert_ffn(xe, w1, w3, w2, gate):
    e, cap, d = xe.shape
    ff = w1.shape[2]
    tf = 512
    nf = pl.cdiv(ff, tf)
    return pl.pallas_call(
        functools.partial(_expert_kernel, tf=tf, tail=ff - (nf - 1) * tf),
        grid=(e, nf),
        in_specs=[pl.BlockSpec((1, cap, d), lambda i, f: (i, 0, 0), pipeline_mode=pl.Buffered(1)),
                  pl.BlockSpec((1, d, tf), lambda i, f: (i, 0, f)),
                  pl.BlockSpec((1, d, tf), lambda i, f: (i, 0, f)),
                  pl.BlockSpec((1, tf, d), lambda i, f: (i, f, 0)),
                  pl.BlockSpec((1, cap, 1), lambda i, f: (i, 0, 0))],
        out_specs=pl.BlockSpec((1, cap, d), lambda i, f: (i, 0, 0)),
        out_shape=jax.ShapeDtypeStruct((e, cap, d), F32),
        compiler_params=_params("arbitrary", "arbitrary"),
        name="expert_ffn",
    )(xe, w1, w3, w2, gate.reshape(e, cap, 1))


def _combine_final_kernel(rows_ref, ptr_ref, tok_ref, ye_hbm, x_ref, mod_ref, g_ref, o_ref,
                          buf, sem, acc_ref, *, tm):
    i = pl.program_id(0)
    p0 = ptr_ref[i]
    p1 = ptr_ref[i + 1]
    c0 = p0 // PAIR_CHUNK
    c1 = jnp.where(p1 > p0, (p1 - 1) // PAIR_CHUNK + 1, c0)
    acc_ref[...] = jnp.zeros_like(acc_ref)

    @pl.when(c1 > c0)
    def _():
        _issue_row_gather(rows_ref, c0 * PAIR_CHUNK, PAIR_CHUNK, ye_hbm, buf.at[0], sem.at[0])

    def chunk(c, carry):
        slot = (c - c0) % 2

        @pl.when(c + 1 < c1)
        def _():
            _issue_row_gather(rows_ref, (c + 1) * PAIR_CHUNK, PAIR_CHUNK, ye_hbm,
                              buf.at[1 - slot], sem.at[1 - slot])

        _wait_row_gather(PAIR_CHUNK, ye_hbm, buf.at[slot], sem.at[slot])
        local = tok_ref[pl.ds(c, 1), :] - i * tm
        onehot = (lax.broadcasted_iota(jnp.int32, (tm, PAIR_CHUNK), 0) == local).astype(BF16)
        acc_ref[...] += jnp.dot(onehot, buf[slot].astype(BF16), preferred_element_type=F32)
        return carry

    lax.fori_loop(c0, c1, chunk, 0)
    x = x_ref[...] + mod_ref[0, 5:6, :] * acc_ref[...]
    ms = jnp.mean(x * x, axis=-1, keepdims=True)
    o_ref[...] = x * lax.rsqrt(ms + NORM_EPS) * g_ref[...]


def combine_final(x2d, ye_flat, pair_rows, pair_toks, tile_ptr, mod, final_g, rows_per_mod, mod_base):
    m, d = x2d.shape
    tm = COMBINE_ROWS
    n_pairs = pair_rows.shape[0]
    row_of = _mod_row_map(tm, rows_per_mod, mod_base)
    return pl.pallas_call(
        functools.partial(_combine_final_kernel, tm=tm),
        grid_spec=pltpu.PrefetchScalarGridSpec(
            num_scalar_prefetch=2,
            grid=(m // tm,),
            in_specs=[pl.BlockSpec((n_pairs // PAIR_CHUNK, PAIR_CHUNK), lambda i, r, p: (0, 0)),
                      pl.BlockSpec(memory_space=pl.ANY),
                      pl.BlockSpec((tm, d), lambda i, r, p: (i, 0)),
                      pl.BlockSpec((1, N_MOD, d), lambda i, r, p: (row_of(i), 0, 0)),
                      pl.BlockSpec((1, d), lambda i, r, p: (0, 0))],
            out_specs=pl.BlockSpec((tm, d), lambda i, r, p: (i, 0)),
            scratch_shapes=[pltpu.VMEM((2, PAIR_CHUNK, d), F32), pltpu.SemaphoreType.DMA((2,)),
                            pltpu.VMEM((tm, d), F32)]),
        out_shape=jax.ShapeDtypeStruct((m, d), F32),
        compiler_params=_params("arbitrary"),
        name="combine_final",
    )(pair_rows, tile_ptr, pair_toks.reshape(n_pairs // PAIR_CHUNK, PAIR_CHUNK), ye_flat,
      x2d, mod, final_g.reshape(1, d))


def _prepare_in_proj(w_in, shift_mu):
    d = w_in.shape[0]
    rkv, lora, gf = w_in[:, :G_OFF], w_in[:, G_OFF:SHIFT_W], w_in[:, SHIFT_W:]
    w_p = jnp.concatenate([rkv, gf, lora, jnp.zeros((d, LANES), w_in.dtype)], axis=1).astype(BF16)
    mu = jnp.concatenate([shift_mu[:G_OFF], jnp.zeros((2 * RWKV_W,), F32), shift_mu[G_OFF:],
                          jnp.zeros((LANES,), F32)])
    return w_p, mu.reshape(1, IN_PAD_W)


def _mix_inputs(x, mod, rows_per_mod, mod_base, grid_shift, p):
    b, t, d = x.shape
    x2d = x.reshape(b * t, d)
    proj = in_proj(x2d, mod, p["norm1_g"], p["w_in_p"], rows_per_mod, mod_base)
    proj = proj.reshape(b, t, IN_PAD_W)
    r, v, a, bonus, w, kd, bd = pre_scan(proj, p["mu"], p["up_w"], p["up_b"],
                                         p["k_k"], p["k_a"], p["r_k"], grid_shift)
    per_dir = [_scan_rows(w), _scan_rows(kd), _scan_rows(bd)]
    shared = [_scan_rows(a), _scan_rows(r), _scan_rows(v)]
    if 2 * b * N_HEADS == LANES:
        ops = relayout_merged(per_dir + shared)
    else:
        ops = list(relayout_to_scan(per_dir)) + list(relayout_to_scan(shared))
    return {"x2d": x2d, "proj": proj, "bonus": bonus, "ops": list(ops),
            "mod": (mod, rows_per_mod, mod_base), "shape": (b, t, d)}


def _route(st, y, p):
    b, t, d = st["shape"]
    mod_args = st["mod"]
    yr = post_scan(_from_scan_layout(y, b), st["bonus"], st["proj"], p["gn_w"], p["gn_b"])
    yf = fnet_mix(st["proj"])
    x1 = out_proj(yr.reshape(b * t, RWKV_W), yf.reshape(b * t, RWKV_W), p["w_out_bf"],
                  st["x2d"], *mod_args)
    h2, aff = norm2_router(x1, mod_args[0], p["norm2_g"], p["router_w"], *mod_args[1:])
    n = b * t
    cap = n * CAPACITY_FACTOR // N_EXPERTS
    gate, idx = lax.top_k(aff.T, cap)
    idx_flat = idx.reshape(-1).astype(jnp.int32)
    xe = gather_tokens(h2, idx_flat).reshape(N_EXPERTS, cap, d)
    return {"x1": x1, "xe": xe, "gate": gate, "idx_flat": idx_flat, "mod": mod_args,
            "shape": (b, t, d)}


def _combine(rt, ye, p):
    b, t, d = rt["shape"]
    n = b * t
    idx_flat = rt["idx_flat"]
    pair_toks, pair_rows = lax.sort_key_val(idx_flat, jnp.arange(idx_flat.shape[0], dtype=jnp.int32))
    tile_ptr = jnp.searchsorted(pair_toks, jnp.arange(0, n + 1, COMBINE_ROWS, dtype=jnp.int32),
                                side="left").astype(jnp.int32)
    mod, rows_per_mod, mod_base = rt["mod"]
    out = combine_final(rt["x1"], ye.reshape(-1, d), pair_rows, pair_toks, tile_ptr, mod,
                        p["final_norm_g"], rows_per_mod, mod_base)
    return out.reshape(b, t, d)


def _experts(p):
    return p["exp_w1"], p["exp_w3"], p["exp_w2"]


def _layer(x, mod, rows_per_mod, mod_base, s0, grid_shift, p):
    st = _mix_inputs(x, mod, rows_per_mod, mod_base, grid_shift, p)
    y, s_fin = wkv_scan(st["ops"], s0)
    rt = _route(st, y, p)
    ye = expert_ffn(rt["xe"], *_experts(p), rt["gate"])
    return _combine(rt, ye, p), s_fin


def kernel(x_prompt, x_sample, state_wkv, c, c_ctx, ada_w, ada_b, norm1_g, norm2_g, w_in, shift_mu,
           w_decay_up, w_decay0, w_iclr_up, w_iclr0, k_k, k_a, r_k, gn_w, gn_b, w_out, router_w,
           exp_w1, exp_w3, exp_w2, final_norm_g):
    depth = ada_w.shape[0]
    assert depth == 1, "single-layer trunk"
    l = 0
    bc, tc_len, d = x_prompt.shape
    bl, tl, _ = x_sample.shape
    cvecs = jnp.concatenate([c_ctx[None], c], axis=0)
    mod = ada_mod(cvecs, ada_w[l], ada_b[l]).reshape(1 + bl, N_MOD, d)
    up_w, up_b = _lora_up_tables(w_decay_up[l], w_decay0[l], w_iclr_up[l], w_iclr0[l])
    w_in_p, mu = _prepare_in_proj(w_in[l], shift_mu[l])
    p = {
        "norm1_g": norm1_g[l], "norm2_g": norm2_g[l], "final_norm_g": final_norm_g,
        "w_in_p": w_in_p, "mu": mu, "up_w": up_w, "up_b": up_b,
        "k_k": k_k[l], "k_a": k_a[l], "r_k": r_k[l], "gn_w": gn_w[l], "gn_b": gn_b[l],
        "w_out_bf": w_out[l].astype(BF16), "router_w": router_w[l],
        "exp_w1": exp_w1[l], "exp_w3": exp_w3[l], "exp_w2": exp_w2[l],
    }
    groups_c = 2 * bc * N_HEADS // LANES
    s0_ctx = jnp.zeros((groups_c, HEAD_N, HEAD_N, LANES), F32)
    ctx = _mix_inputs(x_prompt, mod, bc * tc_len, 0, False, p)
    y_ctx, s_ctx = wkv_scan(ctx["ops"], s0_ctx)
    ctx_rt = _route(ctx, y_ctx, p)
    s_ctx = s_ctx.transpose(1, 2, 0, 3).reshape(HEAD_N, HEAD_N, 2, bc, N_HEADS)
    new_state = s_ctx.transpose(3, 2, 4, 1, 0)[:, None]
    groups_l = 2 * bl * N_HEADS // LANES
    s0_lat = state_wkv[:, l].astype(F32).transpose(4, 3, 1, 0, 2)
    s0_lat = s0_lat.reshape(HEAD_N, HEAD_N, groups_l, LANES).transpose(2, 0, 1, 3)
    lat = _mix_inputs(x_sample, mod, tl, 1, True, p)
    ye_ctx = expert_ffn(ctx_rt["xe"], *_experts(p), ctx_rt["gate"])
    y_lat, _ = wkv_scan(lat["ops"], s0_lat)
    lat_rt = _route(lat, y_lat, p)
    ye_lat = expert_ffn(lat_rt["xe"], *_experts(p), lat_rt["gate"])
    return (_combine(ctx_rt, ye_ctx, p), _combine(lat_rt, ye_lat, p), new_state)
```

```python
import functools
import math

import jax
import jax.numpy as jnp
from jax import lax
from jax.experimental import pallas as pl
from jax.experimental.pallas import tpu as pltpu

F32 = jnp.float32
BF16 = jnp.bfloat16

D_MODEL = 2048
RWKV_W = 1024
HEAD_N = 64
N_HEADS = 16
LORA_R = 96
N_EXPERTS = 16
EXPERT_FF = 5504
CAPACITY_FACTOR = 2
N_MOD = 6
NORM_EPS = 1e-6
GN_EPS = 64e-5
FOURIER_GW = 256
N_FOURIER_GROUPS = 4

LANES = 128
HEAD_TILES = RWKV_W // LANES
SHIFT_W = 3 * RWKV_W + 4 * LORA_R
LORA_W = 4 * LORA_R
LORA_BLK = LORA_W + LANES
G_OFF = 3 * RWKV_W
F_OFF = G_OFF + RWKV_W
LORA_OFF = F_OFF + RWKV_W
IN_PAD_W = LORA_OFF + LORA_BLK
ROW_CHUNK = 256
VMEM_LIMIT = 56 * 1024 * 1024
DECAY_SCALE = math.exp(-0.5)


def _params(*sem):
    return pltpu.CompilerParams(dimension_semantics=sem, vmem_limit_bytes=VMEM_LIMIT)


def _seg_ones():
    r = lax.broadcasted_iota(jnp.int32, (LANES, LANES), 0) // HEAD_N
    c = lax.broadcasted_iota(jnp.int32, (LANES, LANES), 1) // HEAD_N
    return (r == c).astype(BF16)


def _split2(x):
    hi = x.astype(BF16)
    return hi, (x - hi.astype(F32)).astype(BF16)


def _split3(x):
    hi = x.astype(BF16)
    rem = x - hi.astype(F32)
    mid = rem.astype(BF16)
    return hi, mid, (rem - mid.astype(F32)).astype(BF16)


def _seg_sum(x, seg):
    return sum(jnp.dot(p, seg, preferred_element_type=F32) for p in _split3(x))


def _ada_kernel(c_ref, w_ref, b_ref, o_ref):
    c = c_ref[...]
    s = c * jax.nn.sigmoid(c)
    o_ref[...] = jnp.dot(s.astype(BF16), w_ref[...].astype(BF16),
                         preferred_element_type=F32) + b_ref[...]


def ada_mod(cvecs, ada_w, ada_b):
    rows, d = cvecs.shape
    n = ada_w.shape[1]
    tn = 1024
    return pl.pallas_call(
        _ada_kernel,
        grid=(n // tn,),
        in_specs=[pl.BlockSpec((rows, d), lambda j: (0, 0)),
                  pl.BlockSpec((d, tn), lambda j: (0, j)),
                  pl.BlockSpec((1, tn), lambda j: (0, j))],
        out_specs=pl.BlockSpec((rows, tn), lambda j: (0, j)),
        out_shape=jax.ShapeDtypeStruct((rows, n), F32),
        compiler_params=_params("arbitrary"),
        name="ada_mod",
    )(cvecs, ada_w, ada_b.reshape(1, n))


def _mod_row_map(tm, rows_per_mod, mod_base):
    return lambda i: mod_base + (i * tm) // rows_per_mod


def _rms_modulate(x, g, scale, shift):
    ms = jnp.mean(x * x, axis=-1, keepdims=True)
    return x * lax.rsqrt(ms + NORM_EPS) * g * (1.0 + scale) + shift


def _inproj_kernel(x_ref, mod_ref, g_ref, w_ref, o_ref, h_ref, *, tm):
    @pl.when(pl.program_id(1) == 0)
    def _():
        def body(c, carry):
            rows = pl.ds(pl.multiple_of(c * ROW_CHUNK, ROW_CHUNK), ROW_CHUNK)
            h = _rms_modulate(x_ref[rows, :], g_ref[...], mod_ref[0, 1:2, :], mod_ref[0, 0:1, :])
            h_ref[rows, :] = h.astype(BF16)
            return carry
        lax.fori_loop(0, tm // ROW_CHUNK, body, 0)

    o_ref[...] = jnp.dot(h_ref[...], w_ref[...], preferred_element_type=F32)


def in_proj(x2d, mod, norm_g, w_p, rows_per_mod, mod_base):
    m, d = x2d.shape
    n = w_p.shape[1]
    tm, tn = min(1024, rows_per_mod), 512
    row_of = _mod_row_map(tm, rows_per_mod, mod_base)
    return pl.pallas_call(
        functools.partial(_inproj_kernel, tm=tm),
        grid=(m // tm, n // tn),
        in_specs=[pl.BlockSpec((tm, d), lambda i, j: (i, 0)),
                  pl.BlockSpec((1, N_MOD, d), lambda i, j: (row_of(i), 0, 0)),
                  pl.BlockSpec((1, d), lambda i, j: (0, 0)),
                  pl.BlockSpec((d, tn), lambda i, j: (0, j))],
        out_specs=pl.BlockSpec((tm, tn), lambda i, j: (i, j)),
        out_shape=jax.ShapeDtypeStruct((m, n), F32),
        scratch_shapes=[pltpu.VMEM((tm, d), BF16)],
        compiler_params=_params("arbitrary", "arbitrary"),
        name="in_proj",
    )(x2d, mod, norm_g.reshape(1, d), w_p)


def _shifted(ref, mu, c, n_chunks, t_len, grid_shift):
    ch = ROW_CHUNK
    base = pl.multiple_of(c * ch, ch)
    cur = ref[0, pl.ds(base, ch), :]
    row = lax.broadcasted_iota(jnp.int32, (ch, 1), 0)
    before = pltpu.roll(cur, 1, 0)
    after = pltpu.roll(cur, ch - 1, 0)
    if not grid_shift:
        prev = jnp.where(row == 0, 0.0, before)
        nxt = jnp.where(row == ch - 1, 0.0, after)
        mixed = 0.5 * (prev + nxt)
    else:
        gw = 64
        left = jnp.where(row % gw == 0, 0.0, before)
        right = jnp.where(row % gw == gw - 1, 0.0, after)
        up_start = pl.multiple_of(jnp.maximum(base - gw, 0), gw)
        dn_start = pl.multiple_of(jnp.minimum(base + ch, t_len - gw), gw)
        up_halo = jnp.where(c > 0, ref[0, pl.ds(up_start, gw), :], 0.0)
        dn_halo = jnp.where(c < n_chunks - 1, ref[0, pl.ds(dn_start, gw), :], 0.0)
        up = jnp.concatenate([up_halo, cur[: ch - gw]], axis=0)
        down = jnp.concatenate([cur[gw:], dn_halo], axis=0)
        mixed = 0.25 * (up + down + left + right)
    return cur + mu * (mixed - cur)


def _prescan_kernel(r_ref, k_ref, v_ref, lora_ref, mur_ref, muk_ref, muv_ref, mul_ref,
                    uph_ref, upl_ref, bias_ref, kk_ref, ka_ref, rk_ref,
                    r_o, v_o, a_o, bonus_o, w_o, kd_o, bd_o, lh_scr, ll_scr, *, t_len, grid_shift):
    n_chunks = t_len // ROW_CHUNK
    seg = _seg_ones()
    c = pl.program_id(2)
    rows = pl.ds(pl.multiple_of(c * ROW_CHUNK, ROW_CHUNK), ROW_CHUNK)
    sh = functools.partial(_shifted, c=c, n_chunks=n_chunks, t_len=t_len, grid_shift=grid_shift)

    @pl.when(pl.program_id(1) == 0)
    def _():
        lane = lax.broadcasted_iota(jnp.int32, (1, LORA_BLK), 1)
        lora = sh(lora_ref, mul_ref[...])
        lora = jnp.where(lane < 2 * LORA_R, jnp.tanh(lora), lora)
        hi, lo = _split2(lora)
        lh_scr[rows, :] = hi
        ll_scr[rows, :] = lo

    r = sh(r_ref, mur_ref[...])
    k = sh(k_ref, muk_ref[...])
    v = sh(v_ref, muv_ref[...])
    l_hi = lh_scr[rows, :]
    l_lo = ll_scr[rows, :]
    raw = (jnp.dot(l_hi, uph_ref[0], preferred_element_type=F32)
           + jnp.dot(l_hi, upl_ref[0], preferred_element_type=F32)
           + jnp.dot(l_lo, uph_ref[0], preferred_element_type=F32)) + bias_ref[0]
    kk = k * kk_ref[...]
    kk = kk * lax.rsqrt(_seg_sum(kk * kk, seg) + 1e-12)
    kd_sum = jnp.zeros_like(k)
    for d in range(2):
        decay = jnp.exp(-DECAY_SCALE * jax.nn.sigmoid(raw[:, d * LANES:(d + 1) * LANES]))
        iclr = jax.nn.sigmoid(raw[:, (2 + d) * LANES:(3 + d) * LANES])
        kd = k * (1.0 + (iclr - 1.0) * ka_ref[...])
        w_o[d, 0] = decay.T
        kd_o[d, 0] = kd.T
        bd_o[d, 0] = (kk * iclr).T
        kd_sum = kd_sum + kd
    r_o[0] = r.T
    v_o[0] = v.T
    a_o[0] = (-kk).T
    bonus_o[0] = _seg_sum(r * (0.5 * kd_sum) * rk_ref[...], seg) * v


def pre_scan(proj, mu, up_w, up_b, k_k, k_a, r_k, grid_shift):
    b, t, _ = proj.shape
    up_hi, up_lo = _split2(up_w)
    if not grid_shift:
        assert t == ROW_CHUNK, "sequence shift handles one chunk per sequence"
    assert t % ROW_CHUNK == 0
    col = lambda off: pl.BlockSpec((1, t, LANES), lambda i, j, c: (i, 0, off + j))
    vec = lambda off: pl.BlockSpec((1, LANES), lambda i, j, c: (0, off + j))
    out_t = pl.BlockSpec((1, LANES, ROW_CHUNK), lambda i, j, c: (i, j, c))
    out_t2 = pl.BlockSpec((2, 1, LANES, ROW_CHUNK), lambda i, j, c: (0, i, j, c))
    s_t = jax.ShapeDtypeStruct((b, RWKV_W, t), F32)
    s_t2 = jax.ShapeDtypeStruct((2, b, RWKV_W, t), F32)
    return pl.pallas_call(
        functools.partial(_prescan_kernel, t_len=t, grid_shift=grid_shift),
        grid=(b, HEAD_TILES, t // ROW_CHUNK),
        in_specs=[col(0), col(HEAD_TILES), col(2 * HEAD_TILES),
                  pl.BlockSpec((1, t, LORA_BLK), lambda i, j, c: (i, 0, LORA_OFF // LORA_BLK)),
                  vec(0), vec(HEAD_TILES), vec(2 * HEAD_TILES),
                  pl.BlockSpec((1, LORA_BLK), lambda i, j, c: (0, LORA_OFF // LORA_BLK)),
                  pl.BlockSpec((1, LORA_BLK, 4 * LANES), lambda i, j, c: (j, 0, 0)),
                  pl.BlockSpec((1, LORA_BLK, 4 * LANES), lambda i, j, c: (j, 0, 0)),
                  pl.BlockSpec((1, 1, 4 * LANES), lambda i, j, c: (j, 0, 0)),
                  vec(0), vec(0), vec(0)],
        out_specs=[out_t, out_t, out_t,
                   pl.BlockSpec((1, ROW_CHUNK, LANES), lambda i, j, c: (i, c, j)),
                   out_t2, out_t2, out_t2],
        out_shape=[s_t, s_t, s_t, jax.ShapeDtypeStruct((b, t, RWKV_W), F32), s_t2, s_t2, s_t2],
        scratch_shapes=[pltpu.VMEM((t, LORA_BLK), BF16), pltpu.VMEM((t, LORA_BLK), BF16)],
        compiler_params=_params("arbitrary", "arbitrary", "arbitrary"),
        name="pre_scan",
    )(proj, proj, proj, proj, mu, mu, mu, mu, up_hi, up_lo, up_b,
      k_k.reshape(1, RWKV_W), k_a.reshape(1, RWKV_W), r_k.reshape(1, RWKV_W))


def _lora_up_tables(w_decay_up, w_decay0, w_iclr_up, w_iclr0):
    mats = [w_decay_up[0], w_decay_up[1], w_iclr_up[0], w_iclr_up[1]]
    bias = [w_decay0[0], w_decay0[1], w_iclr0[0], w_iclr0[1]]
    up = jnp.zeros((HEAD_TILES, LORA_BLK, 4 * LANES), F32)
    for q, m in enumerate(mats):
        blk = m.astype(F32).reshape(LORA_R, HEAD_TILES, LANES).transpose(1, 0, 2)
        up = up.at[:, q * LORA_R:(q + 1) * LORA_R, q * LANES:(q + 1) * LANES].set(blk)
    b = jnp.stack([v.astype(F32).reshape(HEAD_TILES, LANES) for v in bias], axis=1)
    return up, b.reshape(HEAD_TILES, 1, 4 * LANES)


def _scan_chunk(ins, y_ref, yb_ref, s_scr, wr_scr, *, tc, backward=None):
    mixed = yb_ref is not None
    tt = (lambda t: t) if mixed else (lambda t: jnp.where(backward, tc - 1 - t, t))

    def view(i):
        ref = ins[i]
        return (lambda t: ref[tt(t)]), (lambda t, k: ref[tt(t), k:k + 1, :])

    (w_full, w_row), (k_full, k_row), (b_full, b_row), (_, a_row), (r_full, _), (v_full, _) = (
        view(i) for i in range(6))

    wr_scr[0] = w_full(0) * r_full(0)
    sa = jnp.zeros((HEAD_N, LANES), F32)
    y0 = jnp.zeros((HEAD_N, LANES), F32)
    for k in range(HEAD_N):
        s = s_scr[k]
        sa = sa + s * a_row(0, k)
        y0 = y0 + s * wr_scr[0, k:k + 1, :]

    def step(t, carry):
        sa, y0 = carry
        r_t = r_full(t)
        v_t = v_full(t)
        br = jnp.sum(b_full(t) * r_t, axis=0, keepdims=True)
        kr = jnp.sum(k_full(t) * r_t, axis=0, keepdims=True)
        y = y0 + sa * br + v_t * kr
        y_ref[tt(t)] = y
        if mixed:
            yb_ref[tc - 1 - t] = y
        tn = jnp.minimum(t + 1, tc - 1)
        slot = (t + 1) % 2
        wr_scr[slot] = w_full(tn) * r_full(tn)
        sa_n = jnp.zeros((HEAD_N, LANES), F32)
        y0_n = jnp.zeros((HEAD_N, LANES), F32)
        for k in range(HEAD_N):
            s = s_scr[k] * w_row(t, k) + sa * b_row(t, k) + v_t * k_row(t, k)
            s_scr[k] = s
            sa_n = sa_n + s * a_row(tn, k)
            y0_n = y0_n + s * wr_scr[slot, k:k + 1, :]
        return sa_n, y0_n

    lax.fori_loop(0, tc, step, (sa, y0))


def _scan_kernel(*refs, tc, mixed):
    ins, rest = refs[:6], refs[6:]
    if mixed:
        s0_ref, y_ref, yb_ref, sfin_ref, s_scr, wr_scr = rest
        backward = None
    else:
        s0_ref, y_ref, sfin_ref, s_scr, wr_scr = rest
        yb_ref = None
        backward = pl.program_id(0) >= pl.num_programs(0) // 2
    c = pl.program_id(1)

    @pl.when(c == 0)
    def _():
        s_scr[...] = s0_ref[0]

    _scan_chunk(ins, y_ref, yb_ref, s_scr, wr_scr, tc=tc, backward=backward)

    @pl.when(c == pl.num_programs(1) - 1)
    def _():
        sfin_ref[0] = s_scr[...]


def wkv_scan(ops, s0):
    t, _, lanes = ops[0].shape
    groups = lanes // LANES
    mixed = groups == 1
    tc = 32 if mixed else 64
    nc = t // tc
    st = pl.BlockSpec((1, HEAD_N, HEAD_N, LANES), lambda g, c: (g, 0, 0, 0))
    y_shape = jax.ShapeDtypeStruct((t, HEAD_N, lanes), F32)
    if mixed:
        fwd = pl.BlockSpec((tc, HEAD_N, LANES), lambda g, c: (c, 0, g))
        bwd = pl.BlockSpec((tc, HEAD_N, LANES), lambda g, c: (nc - 1 - c, 0, g))
        in_specs, operands = [fwd] * 6 + [st], list(ops) + [s0]
        out_specs, out_shape = [fwd, bwd, st], [y_shape, y_shape]
    else:
        half_g = groups // 2
        tmap = lambda g, c: jnp.where(g >= half_g, nc - 1 - c, c)
        seq = pl.BlockSpec((tc, HEAD_N, LANES), lambda g, c: (tmap(g, c), 0, g))
        shared = pl.BlockSpec((tc, HEAD_N, LANES), lambda g, c: (tmap(g, c), 0, g % half_g))
        in_specs = [seq if o.shape[2] == lanes else shared for o in ops] + [st]
        operands = list(ops) + [s0]
        out_specs, out_shape = [seq, st], [y_shape]
    out_shape.append(jax.ShapeDtypeStruct((groups, HEAD_N, HEAD_N, LANES), F32))
    res = pl.pallas_call(
        functools.partial(_scan_kernel, tc=tc, mixed=mixed),
        grid=(groups, nc),
        in_specs=in_specs,
        out_specs=out_specs,
        out_shape=out_shape,
        scratch_shapes=[pltpu.VMEM((HEAD_N, HEAD_N, LANES), F32),
                        pltpu.VMEM((2, HEAD_N, LANES), F32)],
        compiler_params=_params("arbitrary", "arbitrary"),
        name="wkv_scan",
    )(*operands)
    half = lanes // 2
    if mixed:
        y_f, y_b, s_fin = res
        return y_f[..., :half] + y_b[..., half:], s_fin
    y, s_fin = res
    return y[..., :half] + y[..., half:], s_fin


K_BLOCK = 8


def _relayout_kernel(*refs, n_ops, rb):
    ins, outs, scr = refs[:n_ops], refs[n_ops:2 * n_ops], refs[2 * n_ops]
    for x_ref, o_ref in zip(ins, outs):
        x2 = x_ref.reshape(rb * K_BLOCK, LANES)
        for r0 in range(0, rb, LANES):
            for kk in range(K_BLOCK):
                m = x2[pl.ds(r0 * K_BLOCK + kk, LANES, stride=K_BLOCK), :]
                scr[pl.ds(kk, LANES, stride=K_BLOCK), :] = m.T
            o_ref[:, :, r0:r0 + LANES] = scr[...].reshape(LANES, K_BLOCK, LANES)


def relayout_to_scan(xs):
    rows, _, t = xs[0].shape
    assert rows % LANES == 0
    rb = min(rows, 512)
    n = len(xs)
    return pl.pallas_call(
        functools.partial(_relayout_kernel, n_ops=n, rb=rb),
        grid=(rows // rb, HEAD_N // K_BLOCK, t // LANES),
        in_specs=[pl.BlockSpec((rb, K_BLOCK, LANES), lambda r, k, c: (r, k, c))] * n,
        out_specs=[pl.BlockSpec((LANES, K_BLOCK, rb), lambda r, k, c: (c, k, r))] * n,
        out_shape=[jax.ShapeDtypeStruct((t, HEAD_N, rows), F32)] * n,
        scratch_shapes=[pltpu.VMEM((LANES * K_BLOCK, LANES), F32)],
        compiler_params=_params("arbitrary", "arbitrary", "arbitrary"),
        name="relayout_to_scan",
    )(*xs)


def _relayout_merged_kernel(*refs, n_ops):
    n = n_ops
    fw, bw, outs, scr = refs[:n], refs[n:2 * n], refs[2 * n:3 * n], refs[3 * n]
    half = LANES // 2
    r = lax.broadcasted_iota(jnp.int32, (LANES, LANES), 0)
    c = lax.broadcasted_iota(jnp.int32, (LANES, LANES), 1)
    flip = (r + c == LANES - 1).astype(BF16)
    for f_ref, b_ref, o_ref in zip(fw, bw, outs):
        f2 = f_ref.reshape(half * K_BLOCK, LANES)
        b2 = b_ref.reshape(half * K_BLOCK, LANES)
        for kk in range(K_BLOCK):
            m_f = f2[pl.ds(kk, half, stride=K_BLOCK), :]
            m_b = b2[pl.ds(kk, half, stride=K_BLOCK), :]
            m_b = sum(jnp.dot(part, flip, preferred_element_type=F32) for part in _split3(m_b))
            scr[pl.ds(kk, LANES, stride=K_BLOCK), :] = jnp.concatenate([m_f, m_b], axis=0).T
        o_ref[...] = scr[...].reshape(LANES, K_BLOCK, LANES)


def relayout_merged(xs):
    t = xs[0].shape[2]
    nt = t // LANES
    half = LANES // 2
    assert all(x.shape[0] in (LANES, half) for x in xs)
    fwd = [pl.BlockSpec((half, K_BLOCK, LANES), lambda k, c: (0, k, c))] * len(xs)
    bwd = [pl.BlockSpec((half, K_BLOCK, LANES),
                        functools.partial(lambda blk, k, c: (blk, k, nt - 1 - c), x.shape[0] // half - 1))
           for x in xs]
    n = len(xs)
    return pl.pallas_call(
        functools.partial(_relayout_merged_kernel, n_ops=n),
        grid=(HEAD_N // K_BLOCK, nt),
        in_specs=fwd + bwd,
        out_specs=[pl.BlockSpec((LANES, K_BLOCK, LANES), lambda k, c: (c, k, 0))] * n,
        out_shape=[jax.ShapeDtypeStruct((t, HEAD_N, LANES), F32)] * n,
        scratch_shapes=[pltpu.VMEM((LANES * K_BLOCK, LANES), F32)],
        compiler_params=_params("arbitrary", "arbitrary"),
        name="relayout_merged",
    )(*xs, *xs)


def _scan_rows(x):
    return x.reshape(-1, HEAD_N, x.shape[-1])


def _from_scan_layout(y, b):
    t = y.shape[0]
    return y.reshape(t, HEAD_N, b, N_HEADS).transpose(2, 0, 3, 1).reshape(b, t, RWKV_W)


def _postscan_kernel(y_ref, bonus_ref, g_ref, gw_ref, gb_ref, o_ref):
    seg = _seg_ones()
    for j in range(HEAD_TILES):
        cols = slice(j * LANES, (j + 1) * LANES)
        y = y_ref[0, :, cols]
        mu = _seg_sum(y, seg) * (1.0 / HEAD_N)
        d = y - mu
        var = _seg_sum(d * d, seg) * (1.0 / HEAD_N)
        yn = d * lax.rsqrt(var + GN_EPS) * gw_ref[:, cols] + gb_ref[:, cols]
        o_ref[0, :, cols] = ((yn + bonus_ref[0, :, cols]) * jax.nn.sigmoid(g_ref[0, :, cols])).astype(BF16)


def post_scan(y, bonus, proj, gn_w, gn_b):
    b, t, _ = y.shape
    ch = ROW_CHUNK
    blk = pl.BlockSpec((1, ch, RWKV_W), lambda i, c: (i, c, 0))
    vec = pl.BlockSpec((1, RWKV_W), lambda i, c: (0, 0))
    return pl.pallas_call(
        _postscan_kernel,
        grid=(b, t // ch),
        in_specs=[blk, blk,
                  pl.BlockSpec((1, ch, RWKV_W), lambda i, c: (i, c, G_OFF // RWKV_W)),
                  vec, vec],
        out_specs=blk,
        out_shape=jax.ShapeDtypeStruct((b, t, RWKV_W), BF16),
        compiler_params=_params("arbitrary", "arbitrary"),
        name="post_scan",
    )(y, bonus, proj, gn_w.reshape(1, RWKV_W), gn_b.reshape(1, RWKV_W))


def _fnet_kernel(f_ref, cc_ref, ct_ref, o_ref, xcs_ref, *, t_len, scale):
    @pl.when(pl.program_id(2) == 0)
    def _():
        xc = jnp.dot(f_ref[0].astype(BF16), cc_ref[...], preferred_element_type=F32)
        xcs_ref[0:t_len, :] = xc[:, :FOURIER_GW].astype(BF16)
        xcs_ref[t_len:2 * t_len, :] = xc[:, FOURIER_GW:].astype(BF16)

    out = jnp.dot(ct_ref[...], xcs_ref[...], preferred_element_type=F32)
    o_ref[0] = (out * scale).astype(BF16)


def _fnet_short_kernel(f_ref, cc_ref, ct_ref, o_ref, *, t_len, scale):
    gw = FOURIER_GW
    for g in range(N_FOURIER_GROUPS):
        cols = slice(g * gw, (g + 1) * gw)
        xc = jnp.dot(f_ref[0, :, cols].astype(BF16), cc_ref[...], preferred_element_type=F32)
        out = (jnp.dot(ct_ref[:, :t_len], xc[:, :gw].astype(BF16), preferred_element_type=F32)
               + jnp.dot(ct_ref[:, t_len:], xc[:, gw:].astype(BF16), preferred_element_type=F32))
        o_ref[0, :, cols] = (out * scale).astype(BF16)


def _dft_tables(n):
    j = lax.broadcasted_iota(jnp.int32, (n, n), 0)
    k = lax.broadcasted_iota(jnp.int32, (n, n), 1)
    ang = ((j * k) % n).astype(F32) * (2.0 * math.pi / n)
    return jnp.cos(ang), jnp.sin(ang)


def fnet_mix(proj):
    b, t, _ = proj.shape
    cc, sc = _dft_tables(FOURIER_GW)
    ct, st = _dft_tables(t)
    cc2 = jnp.concatenate([cc, sc], axis=1).astype(BF16)
    ct2 = jnp.concatenate([ct, -st], axis=1).astype(BF16)
    scale = 1.0 / math.sqrt(t * FOURIER_GW)
    if t <= 512:
        blk = pl.BlockSpec((1, t, RWKV_W), lambda i: (i, 0, 0))
        return pl.pallas_call(
            functools.partial(_fnet_short_kernel, t_len=t, scale=scale),
            grid=(b,),
            in_specs=[pl.BlockSpec((1, t, RWKV_W), lambda i: (i, 0, F_OFF // RWKV_W)),
                      pl.BlockSpec((FOURIER_GW, 2 * FOURIER_GW), lambda i: (0, 0)),
                      pl.BlockSpec((t, 2 * t), lambda i: (0, 0))],
            out_specs=blk,
            out_shape=jax.ShapeDtypeStruct((b, t, RWKV_W), BF16),
            compiler_params=_params("arbitrary"),
            name="fnet_mix",
        )(proj, cc2, ct2)
    tq = min(512, t)
    return pl.pallas_call(
        functools.partial(_fnet_kernel, t_len=t, scale=scale),
        grid=(b, N_FOURIER_GROUPS, t // tq),
        in_specs=[pl.BlockSpec((1, t, FOURIER_GW), lambda i, g, q: (i, 0, F_OFF // FOURIER_GW + g)),
                  pl.BlockSpec((FOURIER_GW, 2 * FOURIER_GW), lambda i, g, q: (0, 0)),
                  pl.BlockSpec((tq, 2 * t), lambda i, g, q: (q, 0))],
        out_specs=pl.BlockSpec((1, tq, FOURIER_GW), lambda i, g, q: (i, q, g)),
        out_shape=jax.ShapeDtypeStruct((b, t, RWKV_W), BF16),
        scratch_shapes=[pltpu.VMEM((2 * t, FOURIER_GW), BF16)],
        compiler_params=_params("arbitrary", "arbitrary", "arbitrary"),
        name="fnet_mix",
    )(proj, cc2, ct2)


def _outproj_kernel(yr_ref, yf_ref, w_ref, x_ref, mod_ref, o_ref):
    m = jnp.dot(yr_ref[...], w_ref[0:RWKV_W, :], preferred_element_type=F32)
    m = m + jnp.dot(yf_ref[...], w_ref[RWKV_W:2 * RWKV_W, :], preferred_element_type=F32)
    o_ref[...] = x_ref[...] + mod_ref[0, 2:3, :] * m


def out_proj(yr, yf, w_bf, x2d, mod, rows_per_mod, mod_base):
    m, d = x2d.shape
    tm, tn = min(1024, rows_per_mod), 512
    row_of = _mod_row_map(tm, rows_per_mod, mod_base)
    return pl.pallas_call(
        _outproj_kernel,
        grid=(m // tm, d // tn),
        in_specs=[pl.BlockSpec((tm, RWKV_W), lambda i, j: (i, 0)),
                  pl.BlockSpec((tm, RWKV_W), lambda i, j: (i, 0)),
                  pl.BlockSpec((2 * RWKV_W, tn), lambda i, j: (0, j)),
                  pl.BlockSpec((tm, tn), lambda i, j: (i, j)),
                  pl.BlockSpec((1, N_MOD, tn), lambda i, j: (row_of(i), 0, j))],
        out_specs=pl.BlockSpec((tm, tn), lambda i, j: (i, j)),
        out_shape=jax.ShapeDtypeStruct((m, d), F32),
        compiler_params=_params("arbitrary", "arbitrary"),
        name="out_proj",
    )(yr, yf, w_bf, x2d, mod)


def _norm2_router_kernel(x_ref, mod_ref, g_ref, rwh_ref, rwl_ref, h_ref, aff_ref):
    h = _rms_modulate(x_ref[...], g_ref[...], mod_ref[0, 4:5, :], mod_ref[0, 3:4, :])
    h_ref[...] = h
    h_hi, h_lo = _split2(h)
    logits = (jnp.dot(h_hi, rwh_ref[...], preferred_element_type=F32)
              + jnp.dot(h_hi, rwl_ref[...], preferred_element_type=F32)
              + jnp.dot(h_lo, rwh_ref[...], preferred_element_type=F32))
    z = logits - jnp.max(logits, axis=-1, keepdims=True)
    e = jnp.exp(z)
    aff_ref[...] = e / jnp.sum(e, axis=-1, keepdims=True)


def norm2_router(x2d, mod, norm_g, router_w, rows_per_mod, mod_base):
    m, d = x2d.shape
    tm = ROW_CHUNK
    row_of = _mod_row_map(tm, rows_per_mod, mod_base)
    rw_hi, rw_lo = _split2(router_w)
    return pl.pallas_call(
        _norm2_router_kernel,
        grid=(m // tm,),
        in_specs=[pl.BlockSpec((tm, d), lambda i: (i, 0)),
                  pl.BlockSpec((1, N_MOD, d), lambda i: (row_of(i), 0, 0)),
                  pl.BlockSpec((1, d), lambda i: (0, 0)),
                  pl.BlockSpec((d, N_EXPERTS), lambda i: (0, 0)),
                  pl.BlockSpec((d, N_EXPERTS), lambda i: (0, 0))],
        out_specs=[pl.BlockSpec((tm, d), lambda i: (i, 0)),
                   pl.BlockSpec((tm, N_EXPERTS), lambda i: (i, 0))],
        out_shape=[jax.ShapeDtypeStruct((m, d), F32),
                   jax.ShapeDtypeStruct((m, N_EXPERTS), F32)],
        compiler_params=_params("arbitrary"),
        name="norm2_router",
    )(x2d, mod, norm_g.reshape(1, d), rw_hi, rw_lo)


GATHER_ROWS = 256
PAIR_CHUNK = LANES
COMBINE_ROWS = 256


def _issue_row_gather(idx_ref, base, n_rows, src_hbm, dst, sem):
    def body(r, carry):
        row = idx_ref[base + r]
        pltpu.make_async_copy(src_hbm.at[pl.ds(row, 1), :], dst.at[pl.ds(r, 1), :], sem).start()
        return carry
    lax.fori_loop(0, n_rows, body, 0, unroll=8)


def _wait_row_gather(n_rows, src_hbm, dst, sem):
    pltpu.make_async_copy(src_hbm.at[pl.ds(0, n_rows), :], dst, sem).wait()


def _gather_kernel(idx_ref, h_hbm, o_ref, buf, sem):
    s = pl.program_id(0)
    slot = s % 2

    @pl.when(s == 0)
    def _():
        _issue_row_gather(idx_ref, 0, GATHER_ROWS, h_hbm, buf.at[0], sem.at[0])

    @pl.when(s + 1 < pl.num_programs(0))
    def _():
        _issue_row_gather(idx_ref, (s + 1) * GATHER_ROWS, GATHER_ROWS, h_hbm,
                          buf.at[1 - slot], sem.at[1 - slot])

    _wait_row_gather(GATHER_ROWS, h_hbm, buf.at[slot], sem.at[slot])
    o_ref[...] = buf[slot].astype(BF16)


def gather_tokens(h, idx_flat):
    n, d = h.shape
    rows = idx_flat.shape[0]
    return pl.pallas_call(
        _gather_kernel,
        grid_spec=pltpu.PrefetchScalarGridSpec(
            num_scalar_prefetch=1,
            grid=(rows // GATHER_ROWS,),
            in_specs=[pl.BlockSpec(memory_space=pl.ANY)],
            out_specs=pl.BlockSpec((GATHER_ROWS, d), lambda s, idx: (s, 0)),
            scratch_shapes=[pltpu.VMEM((2, GATHER_ROWS, d), F32), pltpu.SemaphoreType.DMA((2,))]),
        out_shape=jax.ShapeDtypeStruct((rows, d), BF16),
        compiler_params=_params("arbitrary"),
        name="gather_tokens",
    )(idx_flat, h)


def _ffn_accumulate(xe_ref, w1_ref, w3_ref, w2_ref, o_ref, width):
    x = xe_ref[0]
    h1 = jnp.dot(x, w1_ref[0, :, :width].astype(BF16), preferred_element_type=F32)
    h3 = jnp.dot(x, w3_ref[0, :, :width].astype(BF16), preferred_element_type=F32)
    hid = h1 * jax.nn.sigmoid(h1) * h3
    o_ref[0] += jnp.dot(hid.astype(BF16), w2_ref[0, :width, :].astype(BF16),
                        preferred_element_type=F32)


def _expert_kernel(xe_ref, w1_ref, w3_ref, w2_ref, gate_ref, o_ref, *, tf, tail):
    f = pl.program_id(1)
    last = pl.num_programs(1) - 1
    step = functools.partial(_ffn_accumulate, xe_ref, w1_ref, w3_ref, w2_ref, o_ref)

    @pl.when(f == 0)
    def _():
        o_ref[...] = jnp.zeros_like(o_ref)

    if tail == tf:
        step(tf)
    else:
        @pl.when(f < last)
        def _():
            step(tf)

        @pl.when(f == last)
        def _():
            step(tail)

    @pl.when(f == last)
    def _():
        o_ref[0] = o_ref[0] * gate_ref[0]


def expert_ffn(xe, w1, w3, w2, gate):
    e, cap, d = xe.shape
    ff = w1.shape[2]
    tf = 512
    nf = pl.cdiv(ff, tf)
    return pl.pallas_call(
        functools.partial(_expert_kernel, tf=tf, tail=ff - (nf - 1) * tf),
        grid=(e, nf),
        in_specs=[pl.BlockSpec((1, cap, d), lambda i, f: (i, 0, 0)),
                  pl.BlockSpec((1, d, tf), lambda i, f: (i, 0, f)),
                  pl.BlockSpec((1, d, tf), lambda i, f: (i, 0, f)),
                  pl.BlockSpec((1, tf, d), lambda i, f: (i, f, 0)),
                  pl.BlockSpec((1, cap, 1), lambda i, f: (i, 0, 0))],
        out_specs=pl.BlockSpec((1, cap, d), lambda i, f: (i, 0, 0)),
        out_shape=jax.ShapeDtypeStruct((e, cap, d), F32),
        compiler_params=_params("arbitrary", "arbitrary"),
        name="expert_ffn",
    )(xe, w1, w3, w2, gate.reshape(e, cap, 1))


def _combine_final_kernel(rows_ref, ptr_ref, tok_ref, ye_hbm, x_ref, mod_ref, g_ref, o_ref,
                          buf, sem, acc_ref, *, tm):
    i = pl.program_id(0)
    p0 = ptr_ref[i]
    p1 = ptr_ref[i + 1]
    c0 = p0 // PAIR_CHUNK
    c1 = jnp.where(p1 > p0, (p1 - 1) // PAIR_CHUNK + 1, c0)
    acc_ref[...] = jnp.zeros_like(acc_ref)

    @pl.when(c1 > c0)
    def _():
        _issue_row_gather(rows_ref, c0 * PAIR_CHUNK, PAIR_CHUNK, ye_hbm, buf.at[0], sem.at[0])

    def chunk(c, carry):
        slot = (c - c0) % 2

        @pl.when(c + 1 < c1)
        def _():
            _issue_row_gather(rows_ref, (c + 1) * PAIR_CHUNK, PAIR_CHUNK, ye_hbm,
                              buf.at[1 - slot], sem.at[1 - slot])

        _wait_row_gather(PAIR_CHUNK, ye_hbm, buf.at[slot], sem.at[slot])
        local = tok_ref[pl.ds(c, 1), :] - i * tm
        onehot = (lax.broadcasted_iota(jnp.int32, (tm, PAIR_CHUNK), 0) == local).astype(BF16)
        acc_ref[...] += jnp.dot(onehot, buf[slot].astype(BF16), preferred_element_type=F32)
        return carry

    lax.fori_loop(c0, c1, chunk, 0)
    x = x_ref[...] + mod_ref[0, 5:6, :] * acc_ref[...]
    ms = jnp.mean(x * x, axis=-1, keepdims=True)
    o_ref[...] = x * lax.rsqrt(ms + NORM_EPS) * g_ref[...]


def combine_final(x2d, ye_flat, pair_rows, pair_toks, tile_ptr, mod, final_g, rows_per_mod, mod_base):
    m, d = x2d.shape
    tm = COMBINE_ROWS
    n_pairs = pair_rows.shape[0]
    row_of = _mod_row_map(tm, rows_per_mod, mod_base)
    return pl.pallas_call(
        functools.partial(_combine_final_kernel, tm=tm),
        grid_spec=pltpu.PrefetchScalarGridSpec(
            num_scalar_prefetch=2,
            grid=(m // tm,),
            in_specs=[pl.BlockSpec((n_pairs // PAIR_CHUNK, PAIR_CHUNK), lambda i, r, p: (0, 0)),
                      pl.BlockSpec(memory_space=pl.ANY),
                      pl.BlockSpec((tm, d), lambda i, r, p: (i, 0)),
                      pl.BlockSpec((1, N_MOD, d), lambda i, r, p: (row_of(i), 0, 0)),
                      pl.BlockSpec((1, d), lambda i, r, p: (0, 0))],
            out_specs=pl.BlockSpec((tm, d), lambda i, r, p: (i, 0)),
            scratch_shapes=[pltpu.VMEM((2, PAIR_CHUNK, d), F32), pltpu.SemaphoreType.DMA((2,)),
                            pltpu.VMEM((tm, d), F32)]),
        out_shape=jax.ShapeDtypeStruct((m, d), F32),
        compiler_params=_params("arbitrary"),
        name="combine_final",
    )(pair_rows, tile_ptr, pair_toks.reshape(n_pairs // PAIR_CHUNK, PAIR_CHUNK), ye_flat,
      x2d, mod, final_g.reshape(1, d))


def _prepare_in_proj(w_in, shift_mu):
    d = w_in.shape[0]
    rkv, lora, gf = w_in[:, :G_OFF], w_in[:, G_OFF:SHIFT_W], w_in[:, SHIFT_W:]
    w_p = jnp.concatenate([rkv, gf, lora, jnp.zeros((d, LANES), w_in.dtype)], axis=1).astype(BF16)
    mu = jnp.concatenate([shift_mu[:G_OFF], jnp.zeros((2 * RWKV_W,), F32), shift_mu[G_OFF:],
                          jnp.zeros((LANES,), F32)])
    return w_p, mu.reshape(1, IN_PAD_W)


def _mix_inputs(x, mod, rows_per_mod, mod_base, grid_shift, p):
    b, t, d = x.shape
    x2d = x.reshape(b * t, d)
    proj = in_proj(x2d, mod, p["norm1_g"], p["w_in_p"], rows_per_mod, mod_base)
    proj = proj.reshape(b, t, IN_PAD_W)
    r, v, a, bonus, w, kd, bd = pre_scan(proj, p["mu"], p["up_w"], p["up_b"],
                                         p["k_k"], p["k_a"], p["r_k"], grid_shift)
    per_dir = [_scan_rows(w), _scan_rows(kd), _scan_rows(bd)]
    shared = [_scan_rows(a), _scan_rows(r), _scan_rows(v)]
    if 2 * b * N_HEADS == LANES:
        ops = relayout_merged(per_dir + shared)
    else:
        ops = list(relayout_to_scan(per_dir)) + list(relayout_to_scan(shared))
    return {"x2d": x2d, "proj": proj, "bonus": bonus, "ops": list(ops),
            "mod": (mod, rows_per_mod, mod_base), "shape": (b, t, d)}


def _route(st, y, p):
    b, t, d = st["shape"]
    mod_args = st["mod"]
    yr = post_scan(_from_scan_layout(y, b), st["bonus"], st["proj"], p["gn_w"], p["gn_b"])
    yf = fnet_mix(st["proj"])
    x1 = out_proj(yr.reshape(b * t, RWKV_W), yf.reshape(b * t, RWKV_W), p["w_out_bf"],
                  st["x2d"], *mod_args)
    h2, aff = norm2_router(x1, mod_args[0], p["norm2_g"], p["router_w"], *mod_args[1:])
    n = b * t
    cap = n * CAPACITY_FACTOR // N_EXPERTS
    gate, idx = lax.top_k(aff.T, cap)
    idx_flat = idx.reshape(-1).astype(jnp.int32)
    xe = gather_tokens(h2, idx_flat).reshape(N_EXPERTS, cap, d)
    return {"x1": x1, "xe": xe, "gate": gate, "idx_flat": idx_flat, "mod": mod_args,
            "shape": (b, t, d)}


def _combine(rt, ye, p):
    b, t, d = rt["shape"]
    n = b * t
    idx_flat = rt["idx_flat"]
    pair_toks, pair_rows = lax.sort_key_val(idx_flat, jnp.arange(idx_flat.shape[0], dtype=jnp.int32))
    tile_ptr = jnp.searchsorted(pair_toks, jnp.arange(0, n + 1, COMBINE_ROWS, dtype=jnp.int32),
                                side="left").astype(jnp.int32)
    mod, rows_per_mod, mod_base = rt["mod"]
    out = combine_final(rt["x1"], ye.reshape(-1, d), pair_rows, pair_toks, tile_ptr, mod,
                        p["final_norm_g"], rows_per_mod, mod_base)
    return out.reshape(b, t, d)


def _experts(p):
    return p["exp_w1"], p["exp_w3"], p["exp_w2"]


def _layer(x, mod, rows_per_mod, mod_base, s0, grid_shift, p):
    st = _mix_inputs(x, mod, rows_per_mod, mod_base, grid_shift, p)
    y, s_fin = wkv_scan(st["ops"], s0)
    rt = _route(st, y, p)
    ye = expert_ffn(rt["xe"], *_experts(p), rt["gate"])
    return _combine(rt, ye, p), s_fin


def kernel(x_prompt, x_sample, state_wkv, c, c_ctx, ada_w, ada_b, norm1_g, norm2_g, w_in, shift_mu,
           w_decay_up, w_decay0, w_iclr_up, w_iclr0, k_k, k_a, r_k, gn_w, gn_b, w_out, router_w,
           exp_w1, exp_w3, exp_w2, final_norm_g):
    depth = ada_w.shape[0]
    assert depth == 1, "single-layer trunk"
    l = 0
    bc, tc_len, d = x_prompt.shape
    bl, tl, _ = x_sample.shape
    cvecs = jnp.concatenate([c_ctx[None], c], axis=0)
    mod = ada_mod(cvecs, ada_w[l], ada_b[l]).reshape(1 + bl, N_MOD, d)
    up_w, up_b = _lora_up_tables(w_decay_up[l], w_decay0[l], w_iclr_up[l], w_iclr0[l])
    w_in_p, mu = _prepare_in_proj(w_in[l], shift_mu[l])
    p = {
        "norm1_g": norm1_g[l], "norm2_g": norm2_g[l], "final_norm_g": final_norm_g,
        "w_in_p": w_in_p, "mu": mu, "up_w": up_w, "up_b": up_b,
        "k_k": k_k[l], "k_a": k_a[l], "r_k": r_k[l], "gn_w": gn_w[l], "gn_b": gn_b[l],
        "w_out_bf": w_out[l].astype(BF16), "router_w": router_w[l],
        "exp_w1": exp_w1[l], "exp_w3": exp_w3[l], "exp_w2": exp_w2[l],
    }
    groups_c = 2 * bc * N_HEADS // LANES
    s0_ctx = jnp.zeros((groups_c, HEAD_N, HEAD_N, LANES), F32)
    ctx = _mix_inputs(x_prompt, mod, bc * tc_len, 0, False, p)
    y_ctx, s_ctx = wkv_scan(ctx["ops"], s0_ctx)
    ctx_rt = _route(ctx, y_ctx, p)
    s_ctx = s_ctx.transpose(1, 2, 0, 3).reshape(HEAD_N, HEAD_N, 2, bc, N_HEADS)
    new_state = s_ctx.transpose(3, 2, 4, 1, 0)[:, None]
    groups_l = 2 * bl * N_HEADS // LANES
    s0_lat = state_wkv[:, l].astype(F32).transpose(4, 3, 1, 0, 2)
    s0_lat = s0_lat.reshape(HEAD_N, HEAD_N, groups_l, LANES).transpose(2, 0, 1, 3)
    lat = _mix_inputs(x_sample, mod, tl, 1, True, p)
    ye_ctx = expert_ffn(ctx_rt["xe"], *_experts(p), ctx_rt["gate"])
    y_lat, _ = wkv_scan(lat["ops"], s0_lat)
    lat_rt = _route(lat, y_lat, p)
    ye_lat = expert_ffn(lat_rt["xe"], *_experts(p), lat_rt["gate"])
    return (_combine(ctx_rt, ye_ctx, p), _combine(lat_rt, ye_lat, p), new_state)
```

```python
import functools
import math

import jax
import jax.numpy as jnp
from jax import lax
from jax.experimental import pallas as pl
from jax.experimental.pallas import tpu as pltpu

F32 = jnp.float32
BF16 = jnp.bfloat16

D_MODEL = 2048
RWKV_W = 1024
HEAD_N = 64
N_HEADS = 16
LORA_R = 96
N_EXPERTS = 16
EXPERT_FF = 5504
CAPACITY_FACTOR = 2
N_MOD = 6
NORM_EPS = 1e-6
GN_EPS = 64e-5
FOURIER_GW = 256
N_FOURIER_GROUPS = 4

LANES = 128
HEAD_TILES = RWKV_W // LANES
SHIFT_W = 3 * RWKV_W + 4 * LORA_R
LORA_W = 4 * LORA_R
LORA_BLK = LORA_W + LANES
G_OFF = 3 * RWKV_W
F_OFF = G_OFF + RWKV_W
LORA_OFF = F_OFF + RWKV_W
IN_PAD_W = LORA_OFF + LORA_BLK
ROW_CHUNK = 256
VMEM_LIMIT = 56 * 1024 * 1024
DECAY_SCALE = math.exp(-0.5)


def _params(*sem):
    return pltpu.CompilerParams(dimension_semantics=sem, vmem_limit_bytes=VMEM_LIMIT)


def _seg_ones():
    r = lax.broadcasted_iota(jnp.int32, (LANES, LANES), 0) // HEAD_N
    c = lax.broadcasted_iota(jnp.int32, (LANES, LANES), 1) // HEAD_N
    return (r == c).astype(BF16)


def _split2(x):
    hi = x.astype(BF16)
    return hi, (x - hi.astype(F32)).astype(BF16)


def _split3(x):
    hi = x.astype(BF16)
    rem = x - hi.astype(F32)
    mid = rem.astype(BF16)
    return hi, mid, (rem - mid.astype(F32)).astype(BF16)


def _seg_sum(x, seg):
    return sum(jnp.dot(p, seg, preferred_element_type=F32) for p in _split3(x))


def _ada_kernel(c_ref, w_ref, b_ref, o_ref):
    c = c_ref[...]
    s = c * jax.nn.sigmoid(c)
    o_ref[...] = jnp.dot(s.astype(BF16), w_ref[...].astype(BF16),
                         preferred_element_type=F32) + b_ref[...]


def ada_mod(cvecs, ada_w, ada_b):
    rows, d = cvecs.shape
    n = ada_w.shape[1]
    tn = 1024
    return pl.pallas_call(
        _ada_kernel,
        grid=(n // tn,),
        in_specs=[pl.BlockSpec((rows, d), lambda j: (0, 0)),
                  pl.BlockSpec((d, tn), lambda j: (0, j)),
                  pl.BlockSpec((1, tn), lambda j: (0, j))],
        out_specs=pl.BlockSpec((rows, tn), lambda j: (0, j)),
        out_shape=jax.ShapeDtypeStruct((rows, n), F32),
        compiler_params=_params("arbitrary"),
        name="ada_mod",
    )(cvecs, ada_w, ada_b.reshape(1, n))


def _mod_row_map(tm, rows_per_mod, mod_base):
    return lambda i: mod_base + (i * tm) // rows_per_mod


def _rms_modulate(x, g, scale, shift):
    ms = jnp.mean(x * x, axis=-1, keepdims=True)
    return x * lax.rsqrt(ms + NORM_EPS) * g * (1.0 + scale) + shift


def _inproj_kernel(x_ref, mod_ref, g_ref, w_ref, o_ref, h_ref, *, tm):
    @pl.when(pl.program_id(1) == 0)
    def _():
        def body(c, carry):
            rows = pl.ds(pl.multiple_of(c * ROW_CHUNK, ROW_CHUNK), ROW_CHUNK)
            h = _rms_modulate(x_ref[rows, :], g_ref[...], mod_ref[0, 1:2, :], mod_ref[0, 0:1, :])
            h_ref[rows, :] = h.astype(BF16)
            return carry
        lax.fori_loop(0, tm // ROW_CHUNK, body, 0)

    o_ref[...] = jnp.dot(h_ref[...], w_ref[...], preferred_element_type=F32)


def in_proj(x2d, mod, norm_g, w_p, rows_per_mod, mod_base):
    m, d = x2d.shape
    n = w_p.shape[1]
    tm, tn = min(1024, rows_per_mod), 512
    row_of = _mod_row_map(tm, rows_per_mod, mod_base)
    return pl.pallas_call(
        functools.partial(_inproj_kernel, tm=tm),
        grid=(m // tm, n // tn),
        in_specs=[pl.BlockSpec((tm, d), lambda i, j: (i, 0)),
                  pl.BlockSpec((1, N_MOD, d), lambda i, j: (row_of(i), 0, 0)),
                  pl.BlockSpec((1, d), lambda i, j: (0, 0)),
                  pl.BlockSpec((d, tn), lambda i, j: (0, j))],
        out_specs=pl.BlockSpec((tm, tn), lambda i, j: (i, j)),
        out_shape=jax.ShapeDtypeStruct((m, n), F32),
        scratch_shapes=[pltpu.VMEM((tm, d), BF16)],
        compiler_params=_params("arbitrary", "arbitrary"),
        name="in_proj",
    )(x2d, mod, norm_g.reshape(1, d), w_p)


def _shifted(ref, mu, c, n_chunks, t_len, grid_shift):
    ch = ROW_CHUNK
    base = pl.multiple_of(c * ch, ch)
    cur = ref[0, pl.ds(base, ch), :]
    row = lax.broadcasted_iota(jnp.int32, (ch, 1), 0)
    before = pltpu.roll(cur, 1, 0)
    after = pltpu.roll(cur, ch - 1, 0)
    if not grid_shift:
        prev = jnp.where(row == 0, 0.0, before)
        nxt = jnp.where(row == ch - 1, 0.0, after)
        mixed = 0.5 * (prev + nxt)
    else:
        gw = 64
        left = jnp.where(row % gw == 0, 0.0, before)
        right = jnp.where(row % gw == gw - 1, 0.0, after)
        up_start = pl.multiple_of(jnp.maximum(base - gw, 0), gw)
        dn_start = pl.multiple_of(jnp.minimum(base + ch, t_len - gw), gw)
        up_halo = jnp.where(c > 0, ref[0, pl.ds(up_start, gw), :], 0.0)
        dn_halo = jnp.where(c < n_chunks - 1, ref[0, pl.ds(dn_start, gw), :], 0.0)
        up = jnp.concatenate([up_halo, cur[: ch - gw]], axis=0)
        down = jnp.concatenate([cur[gw:], dn_halo], axis=0)
        mixed = 0.25 * (up + down + left + right)
    return cur + mu * (mixed - cur)


def _prescan_kernel(r_ref, k_ref, v_ref, lora_ref, mur_ref, muk_ref, muv_ref, mul_ref,
                    uph_ref, upl_ref, bias_ref, kk_ref, ka_ref, rk_ref,
                    r_o, v_o, a_o, bonus_o, w_o, kd_o, bd_o, lh_scr, ll_scr,
                    *, t_len, grid_shift, n_sub):
    n_chunks = t_len // ROW_CHUNK
    seg = _seg_ones()
    for sub in range(n_sub):
        c = pl.program_id(2) * n_sub + sub
        rows = pl.ds(pl.multiple_of(c * ROW_CHUNK, ROW_CHUNK), ROW_CHUNK)
        cols = slice(sub * ROW_CHUNK, (sub + 1) * ROW_CHUNK)
        sh = functools.partial(_shifted, c=c, n_chunks=n_chunks, t_len=t_len, grid_shift=grid_shift)

        @pl.when(pl.program_id(1) == 0)
        def _():
            lane = lax.broadcasted_iota(jnp.int32, (1, LORA_BLK), 1)
            lora = sh(lora_ref, mul_ref[...])
            lora = jnp.where(lane < 2 * LORA_R, jnp.tanh(lora), lora)
            hi, lo = _split2(lora)
            lh_scr[rows, :] = hi
            ll_scr[rows, :] = lo

        r = sh(r_ref, mur_ref[...])
        k = sh(k_ref, muk_ref[...])
        v = sh(v_ref, muv_ref[...])
        l_hi = lh_scr[rows, :]
        l_lo = ll_scr[rows, :]
        raw = (jnp.dot(l_hi, uph_ref[0], preferred_element_type=F32)
               + jnp.dot(l_hi, upl_ref[0], preferred_element_type=F32)
               + jnp.dot(l_lo, uph_ref[0], preferred_element_type=F32)) + bias_ref[0]
        kk = k * kk_ref[...]
        kk = kk * lax.rsqrt(_seg_sum(kk * kk, seg) + 1e-12)
        kd_sum = jnp.zeros_like(k)
        for d in range(2):
            decay = jnp.exp(-DECAY_SCALE * jax.nn.sigmoid(raw[:, d * LANES:(d + 1) * LANES]))
            iclr = jax.nn.sigmoid(raw[:, (2 + d) * LANES:(3 + d) * LANES])
            kd = k * (1.0 + (iclr - 1.0) * ka_ref[...])
            w_o[d, 0, :, cols] = decay.T
            kd_o[d, 0, :, cols] = kd.T
            bd_o[d, 0, :, cols] = (kk * iclr).T
            kd_sum = kd_sum + kd
        r_o[0, :, cols] = r.T
        v_o[0, :, cols] = v.T
        a_o[0, :, cols] = (-kk).T
        bonus_o[0, cols, :] = _seg_sum(r * (0.5 * kd_sum) * rk_ref[...], seg) * v


def pre_scan(proj, mu, up_w, up_b, k_k, k_a, r_k, grid_shift):
    b, t, _ = proj.shape
    up_hi, up_lo = _split2(up_w)
    if not grid_shift:
        assert t == ROW_CHUNK, "sequence shift handles one chunk per sequence"
    assert t % ROW_CHUNK == 0
    col = lambda off: pl.BlockSpec((1, t, LANES), lambda i, j, c: (i, 0, off + j))
    vec = lambda off: pl.BlockSpec((1, LANES), lambda i, j, c: (0, off + j))
    n_sub = 2 if t % (2 * ROW_CHUNK) == 0 else 1
    rows_step = n_sub * ROW_CHUNK
    out_t = pl.BlockSpec((1, LANES, rows_step), lambda i, j, c: (i, j, c))
    out_t2 = pl.BlockSpec((2, 1, LANES, rows_step), lambda i, j, c: (0, i, j, c))
    s_t = jax.ShapeDtypeStruct((b, RWKV_W, t), F32)
    s_t2 = jax.ShapeDtypeStruct((2, b, RWKV_W, t), F32)
    return pl.pallas_call(
        functools.partial(_prescan_kernel, t_len=t, grid_shift=grid_shift, n_sub=n_sub),
        grid=(b, HEAD_TILES, t // rows_step),
        in_specs=[col(0), col(HEAD_TILES), col(2 * HEAD_TILES),
                  pl.BlockSpec((1, t, LORA_BLK), lambda i, j, c: (i, 0, LORA_OFF // LORA_BLK)),
                  vec(0), vec(HEAD_TILES), vec(2 * HEAD_TILES),
                  pl.BlockSpec((1, LORA_BLK), lambda i, j, c: (0, LORA_OFF // LORA_BLK)),
                  pl.BlockSpec((1, LORA_BLK, 4 * LANES), lambda i, j, c: (j, 0, 0)),
                  pl.BlockSpec((1, LORA_BLK, 4 * LANES), lambda i, j, c: (j, 0, 0)),
                  pl.BlockSpec((1, 1, 4 * LANES), lambda i, j, c: (j, 0, 0)),
                  vec(0), vec(0), vec(0)],
        out_specs=[out_t, out_t, out_t,
                   pl.BlockSpec((1, rows_step, LANES), lambda i, j, c: (i, c, j)),
                   out_t2, out_t2, out_t2],
        out_shape=[s_t, s_t, s_t, jax.ShapeDtypeStruct((b, t, RWKV_W), F32), s_t2, s_t2, s_t2],
        scratch_shapes=[pltpu.VMEM((t, LORA_BLK), BF16), pltpu.VMEM((t, LORA_BLK), BF16)],
        compiler_params=_params("arbitrary", "arbitrary", "arbitrary"),
        name="pre_scan",
    )(proj, proj, proj, proj, mu, mu, mu, mu, up_hi, up_lo, up_b,
      k_k.reshape(1, RWKV_W), k_a.reshape(1, RWKV_W), r_k.reshape(1, RWKV_W))


def _lora_up_tables(w_decay_up, w_decay0, w_iclr_up, w_iclr0):
    mats = [w_decay_up[0], w_decay_up[1], w_iclr_up[0], w_iclr_up[1]]
    bias = [w_decay0[0], w_decay0[1], w_iclr0[0], w_iclr0[1]]
    up = jnp.zeros((HEAD_TILES, LORA_BLK, 4 * LANES), F32)
    for q, m in enumerate(mats):
        blk = m.astype(F32).reshape(LORA_R, HEAD_TILES, LANES).transpose(1, 0, 2)
        up = up.at[:, q * LORA_R:(q + 1) * LORA_R, q * LANES:(q + 1) * LANES].set(blk)
    b = jnp.stack([v.astype(F32).reshape(HEAD_TILES, LANES) for v in bias], axis=1)
    return up, b.reshape(HEAD_TILES, 1, 4 * LANES)


def _scan_chunk(ins, y_ref, yb_ref, s_scr, wr_scr, *, tc, backward=None):
    mixed = yb_ref is not None
    tt = (lambda t: t) if mixed else (lambda t: jnp.where(backward, tc - 1 - t, t))

    def view(i):
        ref = ins[i]
        return (lambda t: ref[tt(t)]), (lambda t, k: ref[tt(t), k:k + 1, :])

    (w_full, w_row), (k_full, k_row), (b_full, b_row), (_, a_row), (r_full, _), (v_full, _) = (
        view(i) for i in range(6))

    wr_scr[0] = w_full(0) * r_full(0)
    sa = jnp.zeros((HEAD_N, LANES), F32)
    y0 = jnp.zeros((HEAD_N, LANES), F32)
    for k in range(HEAD_N):
        s = s_scr[k]
        sa = sa + s * a_row(0, k)
        y0 = y0 + s * wr_scr[0, k:k + 1, :]

    def step(t, carry):
        sa, y0 = carry
        r_t = r_full(t)
        v_t = v_full(t)
        br = jnp.sum(b_full(t) * r_t, axis=0, keepdims=True)
        kr = jnp.sum(k_full(t) * r_t, axis=0, keepdims=True)
        y = y0 + sa * br + v_t * kr
        y_ref[tt(t)] = y
        if mixed:
            yb_ref[tc - 1 - t] = y
        tn = jnp.minimum(t + 1, tc - 1)
        slot = (t + 1) % 2
        wr_scr[slot] = w_full(tn) * r_full(tn)
        sa_n = jnp.zeros((HEAD_N, LANES), F32)
        y0_n = jnp.zeros((HEAD_N, LANES), F32)
        for k in range(HEAD_N):
            s = s_scr[k] * w_row(t, k) + sa * b_row(t, k) + v_t * k_row(t, k)
            s_scr[k] = s
            sa_n = sa_n + s * a_row(tn, k)
            y0_n = y0_n + s * wr_scr[slot, k:k + 1, :]
        return sa_n, y0_n

    lax.fori_loop(0, tc, step, (sa, y0))


def _scan_kernel(*refs, tc, mixed):
    ins, rest = refs[:6], refs[6:]
    if mixed:
        s0_ref, y_ref, yb_ref, sfin_ref, s_scr, wr_scr = rest
        backward = None
    else:
        s0_ref, y_ref, sfin_ref, s_scr, wr_scr = rest
        yb_ref = None
        backward = pl.program_id(0) >= pl.num_programs(0) // 2
    c = pl.program_id(1)

    @pl.when(c == 0)
    def _():
        s_scr[...] = s0_ref[0]

    _scan_chunk(ins, y_ref, yb_ref, s_scr, wr_scr, tc=tc, backward=backward)

    @pl.when(c == pl.num_programs(1) - 1)
    def _():
        sfin_ref[0] = s_scr[...]


def wkv_scan(ops, s0):
    t, _, lanes = ops[0].shape
    groups = lanes // LANES
    mixed = groups == 1
    tc = 64
    nc = t // tc
    st = pl.BlockSpec((1, HEAD_N, HEAD_N, LANES), lambda g, c: (g, 0, 0, 0))
    y_shape = jax.ShapeDtypeStruct((t, HEAD_N, lanes), F32)
    if mixed:
        fwd = pl.BlockSpec((tc, HEAD_N, LANES), lambda g, c: (c, 0, g))
        bwd = pl.BlockSpec((tc, HEAD_N, LANES), lambda g, c: (nc - 1 - c, 0, g))
        in_specs, operands = [fwd] * 6 + [st], list(ops) + [s0]
        out_specs, out_shape = [fwd, bwd, st], [y_shape, y_shape]
    else:
        half_g = groups // 2
        tmap = lambda g, c: jnp.where(g >= half_g, nc - 1 - c, c)
        seq = pl.BlockSpec((tc, HEAD_N, LANES), lambda g, c: (tmap(g, c), 0, g))
        shared = pl.BlockSpec((tc, HEAD_N, LANES), lambda g, c: (tmap(g, c), 0, g % half_g))
        in_specs = [seq if o.shape[2] == lanes else shared for o in ops] + [st]
        operands = list(ops) + [s0]
        out_specs, out_shape = [seq, st], [y_shape]
    out_shape.append(jax.ShapeDtypeStruct((groups, HEAD_N, HEAD_N, LANES), F32))
    res = pl.pallas_call(
        functools.partial(_scan_kernel, tc=tc, mixed=mixed),
        grid=(groups, nc),
        in_specs=in_specs,
        out_specs=out_specs,
        out_shape=out_shape,
        scratch_shapes=[pltpu.VMEM((HEAD_N, HEAD_N, LANES), F32),
                        pltpu.VMEM((2, HEAD_N, LANES), F32)],
        compiler_params=_params("arbitrary", "arbitrary"),
        name="wkv_scan",
    )(*operands)
    half = lanes // 2
    if mixed:
        y_f, y_b, s_fin = res
        return y_f[..., :half] + y_b[..., half:], s_fin
    y, s_fin = res
    return y[..., :half] + y[..., half:], s_fin


K_BLOCK = 8


def _relayout_kernel(*refs, n_ops, rb):
    ins, outs, scr = refs[:n_ops], refs[n_ops:2 * n_ops], refs[2 * n_ops]
    for x_ref, o_ref in zip(ins, outs):
        x2 = x_ref.reshape(rb * K_BLOCK, LANES)
        for r0 in range(0, rb, LANES):
            for kk in range(K_BLOCK):
                m = x2[pl.ds(r0 * K_BLOCK + kk, LANES, stride=K_BLOCK), :]
                scr[pl.ds(kk, LANES, stride=K_BLOCK), :] = m.T
            o_ref[:, :, r0:r0 + LANES] = scr[...].reshape(LANES, K_BLOCK, LANES)


def relayout_to_scan(xs):
    rows, _, t = xs[0].shape
    assert rows % LANES == 0
    rb = min(rows, 512)
    n = len(xs)
    return pl.pallas_call(
        functools.partial(_relayout_kernel, n_ops=n, rb=rb),
        grid=(rows // rb, HEAD_N // K_BLOCK, t // LANES),
        in_specs=[pl.BlockSpec((rb, K_BLOCK, LANES), lambda r, k, c: (r, k, c))] * n,
        out_specs=[pl.BlockSpec((LANES, K_BLOCK, rb), lambda r, k, c: (c, k, r))] * n,
        out_shape=[jax.ShapeDtypeStruct((t, HEAD_N, rows), F32)] * n,
        scratch_shapes=[pltpu.VMEM((LANES * K_BLOCK, LANES), F32)],
        compiler_params=_params("arbitrary", "arbitrary", "arbitrary"),
        name="relayout_to_scan",
    )(*xs)


def _relayout_merged_kernel(*refs, n_ops):
    n = n_ops
    fw, bw, outs, scr = refs[:n], refs[n:2 * n], refs[2 * n:3 * n], refs[3 * n]
    half = LANES // 2
    r = lax.broadcasted_iota(jnp.int32, (LANES, LANES), 0)
    c = lax.broadcasted_iota(jnp.int32, (LANES, LANES), 1)
    flip = (r + c == LANES - 1).astype(BF16)
    for f_ref, b_ref, o_ref in zip(fw, bw, outs):
        f2 = f_ref.reshape(half * K_BLOCK, LANES)
        b2 = b_ref.reshape(half * K_BLOCK, LANES)
        for kk in range(K_BLOCK):
            m_f = f2[pl.ds(kk, half, stride=K_BLOCK), :]
            m_b = b2[pl.ds(kk, half, stride=K_BLOCK), :]
            m_b = sum(jnp.dot(part, flip, preferred_element_type=F32) for part in _split3(m_b))
            scr[pl.ds(kk, LANES, stride=K_BLOCK), :] = jnp.concatenate([m_f, m_b], axis=0).T
        o_ref[...] = scr[...].reshape(LANES, K_BLOCK, LANES)


def relayout_merged(xs):
    t = xs[0].shape[2]
    nt = t // LANES
    half = LANES // 2
    assert all(x.shape[0] in (LANES, half) for x in xs)
    fwd = [pl.BlockSpec((half, K_BLOCK, LANES), lambda k, c: (0, k, c))] * len(xs)
    bwd = [pl.BlockSpec((half, K_BLOCK, LANES),
                        functools.partial(lambda blk, k, c: (blk, k, nt - 1 - c), x.shape[0] // half - 1))
           for x in xs]
    n = len(xs)
    return pl.pallas_call(
        functools.partial(_relayout_merged_kernel, n_ops=n),
        grid=(HEAD_N // K_BLOCK, nt),
        in_specs=fwd + bwd,
        out_specs=[pl.BlockSpec((LANES, K_BLOCK, LANES), lambda k, c: (c, k, 0))] * n,
        out_shape=[jax.ShapeDtypeStruct((t, HEAD_N, LANES), F32)] * n,
        scratch_shapes=[pltpu.VMEM((LANES * K_BLOCK, LANES), F32)],
        compiler_params=_params("arbitrary", "arbitrary"),
        name="relayout_merged",
    )(*xs, *xs)


def _scan_rows(x):
    return x.reshape(-1, HEAD_N, x.shape[-1])


def _from_scan_layout(y, b):
    t = y.shape[0]
    return y.reshape(t, HEAD_N, b, N_HEADS).transpose(2, 0, 3, 1).reshape(b, t, RWKV_W)


def _postscan_kernel(y_ref, bonus_ref, g_ref, gw_ref, gb_ref, o_ref):
    seg = _seg_ones()
    for j in range(HEAD_TILES):
        cols = slice(j * LANES, (j + 1) * LANES)
        y = y_ref[0, :, cols]
        mu = _seg_sum(y, seg) * (1.0 / HEAD_N)
        d = y - mu
        var = _seg_sum(d * d, seg) * (1.0 / HEAD_N)
        yn = d * lax.rsqrt(var + GN_EPS) * gw_ref[:, cols] + gb_ref[:, cols]
        o_ref[0, :, cols] = ((yn + bonus_ref[0, :, cols]) * jax.nn.sigmoid(g_ref[0, :, cols])).astype(BF16)


def post_scan(y, bonus, proj, gn_w, gn_b):
    b, t, _ = y.shape
    ch = ROW_CHUNK
    blk = pl.BlockSpec((1, ch, RWKV_W), lambda i, c: (i, c, 0))
    vec = pl.BlockSpec((1, RWKV_W), lambda i, c: (0, 0))
    return pl.pallas_call(
        _postscan_kernel,
        grid=(b, t // ch),
        in_specs=[blk, blk,
                  pl.BlockSpec((1, ch, RWKV_W), lambda i, c: (i, c, G_OFF // RWKV_W)),
                  vec, vec],
        out_specs=blk,
        out_shape=jax.ShapeDtypeStruct((b, t, RWKV_W), BF16),
        compiler_params=_params("arbitrary", "arbitrary"),
        name="post_scan",
    )(y, bonus, proj, gn_w.reshape(1, RWKV_W), gn_b.reshape(1, RWKV_W))


def _fnet_kernel(f_ref, cc_ref, ct_ref, o_ref, xcs_ref, *, t_len, scale):
    @pl.when(pl.program_id(2) == 0)
    def _():
        xc = jnp.dot(f_ref[0].astype(BF16), cc_ref[...], preferred_element_type=F32)
        xcs_ref[0:t_len, :] = xc[:, :FOURIER_GW].astype(BF16)
        xcs_ref[t_len:2 * t_len, :] = xc[:, FOURIER_GW:].astype(BF16)

    out = jnp.dot(ct_ref[...], xcs_ref[...], preferred_element_type=F32)
    o_ref[0] = (out * scale).astype(BF16)


def _fnet_short_kernel(f_ref, cc_ref, ct_ref, o_ref, *, t_len, scale):
    gw = FOURIER_GW
    for g in range(N_FOURIER_GROUPS):
        cols = slice(g * gw, (g + 1) * gw)
        xc = jnp.dot(f_ref[0, :, cols].astype(BF16), cc_ref[...], preferred_element_type=F32)
        out = (jnp.dot(ct_ref[:, :t_len], xc[:, :gw].astype(BF16), preferred_element_type=F32)
               + jnp.dot(ct_ref[:, t_len:], xc[:, gw:].astype(BF16), preferred_element_type=F32))
        o_ref[0, :, cols] = (out * scale).astype(BF16)


def _dft_tables(n):
    j = lax.broadcasted_iota(jnp.int32, (n, n), 0)
    k = lax.broadcasted_iota(jnp.int32, (n, n), 1)
    ang = ((j * k) % n).astype(F32) * (2.0 * math.pi / n)
    return jnp.cos(ang), jnp.sin(ang)


def fnet_mix(proj):
    b, t, _ = proj.shape
    cc, sc = _dft_tables(FOURIER_GW)
    ct, st = _dft_tables(t)
    cc2 = jnp.concatenate([cc, sc], axis=1).astype(BF16)
    ct2 = jnp.concatenate([ct, -st], axis=1).astype(BF16)
    scale = 1.0 / math.sqrt(t * FOURIER_GW)
    if t <= 512:
        blk = pl.BlockSpec((1, t, RWKV_W), lambda i: (i, 0, 0))
        return pl.pallas_call(
            functools.partial(_fnet_short_kernel, t_len=t, scale=scale),
            grid=(b,),
            in_specs=[pl.BlockSpec((1, t, RWKV_W), lambda i: (i, 0, F_OFF // RWKV_W)),
                      pl.BlockSpec((FOURIER_GW, 2 * FOURIER_GW), lambda i: (0, 0)),
                      pl.BlockSpec((t, 2 * t), lambda i: (0, 0))],
            out_specs=blk,
            out_shape=jax.ShapeDtypeStruct((b, t, RWKV_W), BF16),
            compiler_params=_params("arbitrary"),
            name="fnet_mix",
        )(proj, cc2, ct2)
    tq = min(512, t)
    return pl.pallas_call(
        functools.partial(_fnet_kernel, t_len=t, scale=scale),
        grid=(b, N_FOURIER_GROUPS, t // tq),
        in_specs=[pl.BlockSpec((1, t, FOURIER_GW), lambda i, g, q: (i, 0, F_OFF // FOURIER_GW + g)),
                  pl.BlockSpec((FOURIER_GW, 2 * FOURIER_GW), lambda i, g, q: (0, 0)),
                  pl.BlockSpec((tq, 2 * t), lambda i, g, q: (q, 0))],
        out_specs=pl.BlockSpec((1, tq, FOURIER_GW), lambda i, g, q: (i, q, g)),
        out_shape=jax.ShapeDtypeStruct((b, t, RWKV_W), BF16),
        scratch_shapes=[pltpu.VMEM((2 * t, FOURIER_GW), BF16)],
        compiler_params=_params("arbitrary", "arbitrary", "arbitrary"),
        name="fnet_mix",
    )(proj, cc2, ct2)


def _outproj_kernel(yr_ref, yf_ref, w_ref, x_ref, mod_ref, o_ref):
    m = jnp.dot(yr_ref[...], w_ref[0:RWKV_W, :], preferred_element_type=F32)
    m = m + jnp.dot(yf_ref[...], w_ref[RWKV_W:2 * RWKV_W, :], preferred_element_type=F32)
    o_ref[...] = x_ref[...] + mod_ref[0, 2:3, :] * m


def out_proj(yr, yf, w_bf, x2d, mod, rows_per_mod, mod_base):
    m, d = x2d.shape
    tm, tn = min(1024, rows_per_mod), 512
    row_of = _mod_row_map(tm, rows_per_mod, mod_base)
    return pl.pallas_call(
        _outproj_kernel,
        grid=(m // tm, d // tn),
        in_specs=[pl.BlockSpec((tm, RWKV_W), lambda i, j: (i, 0)),
                  pl.BlockSpec((tm, RWKV_W), lambda i, j: (i, 0)),
                  pl.BlockSpec((2 * RWKV_W, tn), lambda i, j: (0, j)),
                  pl.BlockSpec((tm, tn), lambda i, j: (i, j)),
                  pl.BlockSpec((1, N_MOD, tn), lambda i, j: (row_of(i), 0, j))],
        out_specs=pl.BlockSpec((tm, tn), lambda i, j: (i, j)),
        out_shape=jax.ShapeDtypeStruct((m, d), F32),
        compiler_params=_params("arbitrary", "arbitrary"),
        name="out_proj",
    )(yr, yf, w_bf, x2d, mod)


def _norm2_router_kernel(x_ref, mod_ref, g_ref, rwh_ref, rwl_ref, h_ref, aff_ref):
    h = _rms_modulate(x_ref[...], g_ref[...], mod_ref[0, 4:5, :], mod_ref[0, 3:4, :])
    h_ref[...] = h
    h_hi, h_lo = _split2(h)
    logits = (jnp.dot(h_hi, rwh_ref[...], preferred_element_type=F32)
              + jnp.dot(h_hi, rwl_ref[...], preferred_element_type=F32)
              + jnp.dot(h_lo, rwh_ref[...], preferred_element_type=F32))
    z = logits - jnp.max(logits, axis=-1, keepdims=True)
    e = jnp.exp(z)
    aff_ref[...] = e / jnp.sum(e, axis=-1, keepdims=True)


def norm2_router(x2d, mod, norm_g, router_w, rows_per_mod, mod_base):
    m, d = x2d.shape
    tm = ROW_CHUNK
    row_of = _mod_row_map(tm, rows_per_mod, mod_base)
    rw_hi, rw_lo = _split2(router_w)
    return pl.pallas_call(
        _norm2_router_kernel,
        grid=(m // tm,),
        in_specs=[pl.BlockSpec((tm, d), lambda i: (i, 0)),
                  pl.BlockSpec((1, N_MOD, d), lambda i: (row_of(i), 0, 0)),
                  pl.BlockSpec((1, d), lambda i: (0, 0)),
                  pl.BlockSpec((d, N_EXPERTS), lambda i: (0, 0)),
                  pl.BlockSpec((d, N_EXPERTS), lambda i: (0, 0))],
        out_specs=[pl.BlockSpec((tm, d), lambda i: (i, 0)),
                   pl.BlockSpec((tm, N_EXPERTS), lambda i: (i, 0))],
        out_shape=[jax.ShapeDtypeStruct((m, d), F32),
                   jax.ShapeDtypeStruct((m, N_EXPERTS), F32)],
        compiler_params=_params("arbitrary"),
        name="norm2_router",
    )(x2d, mod, norm_g.reshape(1, d), rw_hi, rw_lo)


GATHER_ROWS = 256
PAIR_CHUNK = LANES
COMBINE_ROWS = 256


def _issue_row_gather(idx_ref, base, n_rows, src_hbm, dst, sem):
    def body(r, carry):
        row = idx_ref[base + r]
        pltpu.make_async_copy(src_hbm.at[pl.ds(row, 1), :], dst.at[pl.ds(r, 1), :], sem).start()
        return carry
    lax.fori_loop(0, n_rows, body, 0, unroll=8)


def _wait_row_gather(n_rows, src_hbm, dst, sem):
    pltpu.make_async_copy(src_hbm.at[pl.ds(0, n_rows), :], dst, sem).wait()


def _gather_kernel(idx_ref, h_hbm, o_ref, buf, sem):
    s = pl.program_id(0)
    slot = s % 2

    @pl.when(s == 0)
    def _():
        _issue_row_gather(idx_ref, 0, GATHER_ROWS, h_hbm, buf.at[0], sem.at[0])

    @pl.when(s + 1 < pl.num_programs(0))
    def _():
        _issue_row_gather(idx_ref, (s + 1) * GATHER_ROWS, GATHER_ROWS, h_hbm,
                          buf.at[1 - slot], sem.at[1 - slot])

    _wait_row_gather(GATHER_ROWS, h_hbm, buf.at[slot], sem.at[slot])
    o_ref[...] = buf[slot].astype(BF16)


def gather_tokens(h, idx_flat):
    n, d = h.shape
    rows = idx_flat.shape[0]
    return pl.pallas_call(
        _gather_kernel,
        grid_spec=pltpu.PrefetchScalarGridSpec(
            num_scalar_prefetch=1,
            grid=(rows // GATHER_ROWS,),
            in_specs=[pl.BlockSpec(memory_space=pl.ANY)],
            out_specs=pl.BlockSpec((GATHER_ROWS, d), lambda s, idx: (s, 0)),
            scratch_shapes=[pltpu.VMEM((2, GATHER_ROWS, d), F32), pltpu.SemaphoreType.DMA((2,))]),
        out_shape=jax.ShapeDtypeStruct((rows, d), BF16),
        compiler_params=_params("arbitrary"),
        name="gather_tokens",
    )(idx_flat, h)


def _ffn_accumulate(xe_ref, w1_ref, w3_ref, w2_ref, o_ref, width):
    x = xe_ref[0]
    h1 = jnp.dot(x, w1_ref[0, :, :width].astype(BF16), preferred_element_type=F32)
    h3 = jnp.dot(x, w3_ref[0, :, :width].astype(BF16), preferred_element_type=F32)
    hid = h1 * jax.nn.sigmoid(h1) * h3
    o_ref[0] += jnp.dot(hid.astype(BF16), w2_ref[0, :width, :].astype(BF16),
                        preferred_element_type=F32)


def _expert_kernel(xe_ref, w1_ref, w3_ref, w2_ref, gate_ref, o_ref, *, tf, tail):
    f = pl.program_id(1)
    last = pl.num_programs(1) - 1
    step = functools.partial(_ffn_accumulate, xe_ref, w1_ref, w3_ref, w2_ref, o_ref)

    @pl.when(f == 0)
    def _():
        o_ref[...] = jnp.zeros_like(o_ref)

    if tail == tf:
        step(tf)
    else:
        @pl.when(f < last)
        def _():
            step(tf)

        @pl.when(f == last)
        def _():
            step(tail)

    @pl.when(f == last)
    def _():
        o_ref[0] = o_ref[0] * gate_ref[0]


def expert_ffn(xe, w1, w3, w2, gate):
    e, cap, d = xe.shape
    ff = w1.shape[2]
    tf = 512
    nf = pl.cdiv(ff, tf)
    return pl.pallas_call(
        functools.partial(_expert_kernel, tf=tf, tail=ff - (nf - 1) * tf),
        grid=(e, nf),
        in_specs=[pl.BlockSpec((1, cap, d), lambda i, f: (i, 0, 0)),
                  pl.BlockSpec((1, d, tf), lambda i, f: (i, 0, f)),
                  pl.BlockSpec((1, d, tf), lambda i, f: (i, 0, f)),
                  pl.BlockSpec((1, tf, d), lambda i, f: (i, f, 0)),
                  pl.BlockSpec((1, cap, 1), lambda i, f: (i, 0, 0))],
        out_specs=pl.BlockSpec((1, cap, d), lambda i, f: (i, 0, 0)),
        out_shape=jax.ShapeDtypeStruct((e, cap, d), F32),
        compiler_params=_params("arbitrary", "arbitrary"),
        name="expert_ffn",
    )(xe, w1, w3, w2, gate.reshape(e, cap, 1))


def _combine_final_kernel(rows_ref, ptr_ref, tok_ref, ye_hbm, x_ref, mod_ref, g_ref, o_ref,
                          buf, sem, acc_ref, *, tm):
    i = pl.program_id(0)
    p0 = ptr_ref[i]
    p1 = ptr_ref[i + 1]
    c0 = p0 // PAIR_CHUNK
    c1 = jnp.where(p1 > p0, (p1 - 1) // PAIR_CHUNK + 1, c0)
    acc_ref[...] = jnp.zeros_like(acc_ref)

    @pl.when(c1 > c0)
    def _():
        _issue_row_gather(rows_ref, c0 * PAIR_CHUNK, PAIR_CHUNK, ye_hbm, buf.at[0], sem.at[0])

    def chunk(c, carry):
        slot = (c - c0) % 2

        @pl.when(c + 1 < c1)
        def _():
            _issue_row_gather(rows_ref, (c + 1) * PAIR_CHUNK, PAIR_CHUNK, ye_hbm,
                              buf.at[1 - slot], sem.at[1 - slot])

        _wait_row_gather(PAIR_CHUNK, ye_hbm, buf.at[slot], sem.at[slot])
        local = tok_ref[pl.ds(c, 1), :] - i * tm
        onehot = (lax.broadcasted_iota(jnp.int32, (tm, PAIR_CHUNK), 0) == local).astype(BF16)
        acc_ref[...] += jnp.dot(onehot, buf[slot].astype(BF16), preferred_element_type=F32)
        return carry

    lax.fori_loop(c0, c1, chunk, 0)
    x = x_ref[...] + mod_ref[0, 5:6, :] * acc_ref[...]
    ms = jnp.mean(x * x, axis=-1, keepdims=True)
    o_ref[...] = x * lax.rsqrt(ms + NORM_EPS) * g_ref[...]


def combine_final(x2d, ye_flat, pair_rows, pair_toks, tile_ptr, mod, final_g, rows_per_mod, mod_base):
    m, d = x2d.shape
    tm = COMBINE_ROWS
    n_pairs = pair_rows.shape[0]
    row_of = _mod_row_map(tm, rows_per_mod, mod_base)
    return pl.pallas_call(
        functools.partial(_combine_final_kernel, tm=tm),
        grid_spec=pltpu.PrefetchScalarGridSpec(
            num_scalar_prefetch=2,
            grid=(m // tm,),
            in_specs=[pl.BlockSpec((n_pairs // PAIR_CHUNK, PAIR_CHUNK), lambda i, r, p: (0, 0)),
                      pl.BlockSpec(memory_space=pl.ANY),
                      pl.BlockSpec((tm, d), lambda i, r, p: (i, 0)),
                      pl.BlockSpec((1, N_MOD, d), lambda i, r, p: (row_of(i), 0, 0)),
                      pl.BlockSpec((1, d), lambda i, r, p: (0, 0))],
            out_specs=pl.BlockSpec((tm, d), lambda i, r, p: (i, 0)),
            scratch_shapes=[pltpu.VMEM((2, PAIR_CHUNK, d), F32), pltpu.SemaphoreType.DMA((2,)),
                            pltpu.VMEM((tm, d), F32)]),
        out_shape=jax.ShapeDtypeStruct((m, d), F32),
        compiler_params=_params("arbitrary"),
        name="combine_final",
    )(pair_rows, tile_ptr, pair_toks.reshape(n_pairs // PAIR_CHUNK, PAIR_CHUNK), ye_flat,
      x2d, mod, final_g.reshape(1, d))


def _prepare_in_proj(w_in, shift_mu):
    d = w_in.shape[0]
    rkv, lora, gf = w_in[:, :G_OFF], w_in[:, G_OFF:SHIFT_W], w_in[:, SHIFT_W:]
    w_p = jnp.concatenate([rkv, gf, lora, jnp.zeros((d, LANES), w_in.dtype)], axis=1).astype(BF16)
    mu = jnp.concatenate([shift_mu[:G_OFF], jnp.zeros((2 * RWKV_W,), F32), shift_mu[G_OFF:],
                          jnp.zeros((LANES,), F32)])
    return w_p, mu.reshape(1, IN_PAD_W)


def _mix_inputs(x, mod, rows_per_mod, mod_base, grid_shift, p):
    b, t, d = x.shape
    x2d = x.reshape(b * t, d)
    proj = in_proj(x2d, mod, p["norm1_g"], p["w_in_p"], rows_per_mod, mod_base)
    proj = proj.reshape(b, t, IN_PAD_W)
    r, v, a, bonus, w, kd, bd = pre_scan(proj, p["mu"], p["up_w"], p["up_b"],
                                         p["k_k"], p["k_a"], p["r_k"], grid_shift)
    per_dir = [_scan_rows(w), _scan_rows(kd), _scan_rows(bd)]
    shared = [_scan_rows(a), _scan_rows(r), _scan_rows(v)]
    if 2 * b * N_HEADS == LANES:
        ops = relayout_merged(per_dir + shared)
    else:
        ops = list(relayout_to_scan(per_dir)) + list(relayout_to_scan(shared))
    return {"x2d": x2d, "proj": proj, "bonus": bonus, "ops": list(ops),
            "mod": (mod, rows_per_mod, mod_base), "shape": (b, t, d)}


def _route(st, y, p):
    b, t, d = st["shape"]
    mod_args = st["mod"]
    yr = post_scan(_from_scan_layout(y, b), st["bonus"], st["proj"], p["gn_w"], p["gn_b"])
    yf = fnet_mix(st["proj"])
    x1 = out_proj(yr.reshape(b * t, RWKV_W), yf.reshape(b * t, RWKV_W), p["w_out_bf"],
                  st["x2d"], *mod_args)
    h2, aff = norm2_router(x1, mod_args[0], p["norm2_g"], p["router_w"], *mod_args[1:])
    n = b * t
    cap = n * CAPACITY_FACTOR // N_EXPERTS
    gate, idx = lax.top_k(aff.T, cap)
    idx_flat = idx.reshape(-1).astype(jnp.int32)
    xe = gather_tokens(h2, idx_flat).reshape(N_EXPERTS, cap, d)
    return {"x1": x1, "xe": xe, "gate": gate, "idx_flat": idx_flat, "mod": mod_args,
            "shape": (b, t, d)}


def _combine(rt, ye, p):
    b, t, d = rt["shape"]
    n = b * t
    idx_flat = rt["idx_flat"]
    pair_toks, pair_rows = lax.sort_key_val(idx_flat, jnp.arange(idx_flat.shape[0], dtype=jnp.int32))
    tile_ptr = jnp.searchsorted(pair_toks, jnp.arange(0, n + 1, COMBINE_ROWS, dtype=jnp.int32),
                                side="left").astype(jnp.int32)
    mod, rows_per_mod, mod_base = rt["mod"]
    out = combine_final(rt["x1"], ye.reshape(-1, d), pair_rows, pair_toks, tile_ptr, mod,
                        p["final_norm_g"], rows_per_mod, mod_base)
    return out.reshape(b, t, d)


def _experts(p):
    return p["exp_w1"], p["exp_w3"], p["exp_w2"]


def _layer(x, mod, rows_per_mod, mod_base, s0, grid_shift, p):
    st = _mix_inputs(x, mod, rows_per_mod, mod_base, grid_shift, p)
    y, s_fin = wkv_scan(st["ops"], s0)
    rt = _route(st, y, p)
    ye = expert_ffn(rt["xe"], *_experts(p), rt["gate"])
    return _combine(rt, ye, p), s_fin


def kernel(x_prompt, x_sample, state_wkv, c, c_ctx, ada_w, ada_b, norm1_g, norm2_g, w_in, shift_mu,
           w_decay_up, w_decay0, w_iclr_up, w_iclr0, k_k, k_a, r_k, gn_w, gn_b, w_out, router_w,
           exp_w1, exp_w3, exp_w2, final_norm_g):
    depth = ada_w.shape[0]
    assert depth == 1, "single-layer trunk"
    l = 0
    bc, tc_len, d = x_prompt.shape
    bl, tl, _ = x_sample.shape
    cvecs = jnp.concatenate([c_ctx[None], c], axis=0)
    mod = ada_mod(cvecs, ada_w[l], ada_b[l]).reshape(1 + bl, N_MOD, d)
    up_w, up_b = _lora_up_tables(w_decay_up[l], w_decay0[l], w_iclr_up[l], w_iclr0[l])
    w_in_p, mu = _prepare_in_proj(w_in[l], shift_mu[l])
    p = {
        "norm1_g": norm1_g[l], "norm2_g": norm2_g[l], "final_norm_g": final_norm_g,
        "w_in_p": w_in_p, "mu": mu, "up_w": up_w, "up_b": up_b,
        "k_k": k_k[l], "k_a": k_a[l], "r_k": r_k[l], "gn_w": gn_w[l], "gn_b": gn_b[l],
        "w_out_bf": w_out[l].astype(BF16), "router_w": router_w[l],
        "exp_w1": exp_w1[l], "exp_w3": exp_w3[l], "exp_w2": exp_w2[l],
    }
    groups_c = 2 * bc * N_HEADS // LANES
    s0_ctx = jnp.zeros((groups_c, HEAD_N, HEAD_N, LANES), F32)
    ctx = _mix_inputs(x_prompt, mod, bc * tc_len, 0, False, p)
    y_ctx, s_ctx = wkv_scan(ctx["ops"], s0_ctx)
    ctx_rt = _route(ctx, y_ctx, p)
    s_ctx = s_ctx.transpose(1, 2, 0, 3).reshape(HEAD_N, HEAD_N, 2, bc, N_HEADS)
    new_state = s_ctx.transpose(3, 2, 4, 1, 0)[:, None]
    groups_l = 2 * bl * N_HEADS // LANES
    s0_lat = state_wkv[:, l].astype(F32).transpose(4, 3, 1, 0, 2)
    s0_lat = s0_lat.reshape(HEAD_N, HEAD_N, groups_l, LANES).transpose(2, 0, 1, 3)
    lat = _mix_inputs(x_sample, mod, tl, 1, True, p)
    ye_ctx = expert_ffn(ctx_rt["xe"], *_experts(p), ctx_rt["gate"])
    y_lat, _ = wkv_scan(lat["ops"], s0_lat)
    lat_rt = _route(lat, y_lat, p)
    ye_lat = expert_ffn(lat_rt["xe"], *_experts(p), lat_rt["gate"])
    return (_combine(ctx_rt, ye_ctx, p), _combine(lat_rt, ye_lat, p), new_state)
```

```python
import functools
import math

import jax
import jax.numpy as jnp
from jax import lax
from jax.experimental import pallas as pl
from jax.experimental.pallas import tpu as pltpu

F32 = jnp.float32
BF16 = jnp.bfloat16

D_MODEL = 2048
RWKV_W = 1024
HEAD_N = 64
N_HEADS = 16
LORA_R = 96
N_EXPERTS = 16
EXPERT_FF = 5504
CAPACITY_FACTOR = 2
N_MOD = 6
NORM_EPS = 1e-6
GN_EPS = 64e-5
FOURIER_GW = 256
N_FOURIER_GROUPS = 4

LANES = 128
HEAD_TILES = RWKV_W // LANES
SHIFT_W = 3 * RWKV_W + 4 * LORA_R
LORA_W = 4 * LORA_R
LORA_BLK = LORA_W + LANES
G_OFF = 3 * RWKV_W
F_OFF = G_OFF + RWKV_W
LORA_OFF = F_OFF + RWKV_W
IN_PAD_W = LORA_OFF + LORA_BLK
ROW_CHUNK = 256
VMEM_LIMIT = 56 * 1024 * 1024
DECAY_SCALE = math.exp(-0.5)


def _params(*sem):
    return pltpu.CompilerParams(dimension_semantics=sem, vmem_limit_bytes=VMEM_LIMIT)


def _seg_ones():
    r = lax.broadcasted_iota(jnp.int32, (LANES, LANES), 0) // HEAD_N
    c = lax.broadcasted_iota(jnp.int32, (LANES, LANES), 1) // HEAD_N
    return (r == c).astype(BF16)


def _split2(x):
    hi = x.astype(BF16)
    return hi, (x - hi.astype(F32)).astype(BF16)


def _split3(x):
    hi = x.astype(BF16)
    rem = x - hi.astype(F32)
    mid = rem.astype(BF16)
    return hi, mid, (rem - mid.astype(F32)).astype(BF16)


def _seg_sum(x, seg):
    return sum(jnp.dot(p, seg, preferred_element_type=F32) for p in _split3(x))


def _ada_kernel(c_ref, w_ref, b_ref, o_ref):
    c = c_ref[...]
    s = c * jax.nn.sigmoid(c)
    o_ref[...] = jnp.dot(s.astype(BF16), w_ref[...].astype(BF16),
                         preferred_element_type=F32) + b_ref[...]


def ada_mod(cvecs, ada_w, ada_b):
    rows, d = cvecs.shape
    n = ada_w.shape[1]
    tn = 1024
    return pl.pallas_call(
        _ada_kernel,
        grid=(n // tn,),
        in_specs=[pl.BlockSpec((rows, d), lambda j: (0, 0)),
                  pl.BlockSpec((d, tn), lambda j: (0, j)),
                  pl.BlockSpec((1, tn), lambda j: (0, j))],
        out_specs=pl.BlockSpec((rows, tn), lambda j: (0, j)),
        out_shape=jax.ShapeDtypeStruct((rows, n), F32),
        compiler_params=_params("arbitrary"),
        name="ada_mod",
    )(cvecs, ada_w, ada_b.reshape(1, n))


def _mod_row_map(tm, rows_per_mod, mod_base):
    return lambda i: mod_base + (i * tm) // rows_per_mod


def _rms_modulate(x, g, scale, shift):
    ms = jnp.mean(x * x, axis=-1, keepdims=True)
    return x * lax.rsqrt(ms + NORM_EPS) * g * (1.0 + scale) + shift


def _inproj_kernel(x_ref, mod_ref, g_ref, w_ref, o_ref, h_ref, *, tm):
    @pl.when(pl.program_id(1) == 0)
    def _():
        def body(c, carry):
            rows = pl.ds(pl.multiple_of(c * ROW_CHUNK, ROW_CHUNK), ROW_CHUNK)
            h = _rms_modulate(x_ref[rows, :], g_ref[...], mod_ref[0, 1:2, :], mod_ref[0, 0:1, :])
            h_ref[rows, :] = h.astype(BF16)
            return carry
        lax.fori_loop(0, tm // ROW_CHUNK, body, 0)

    o_ref[...] = jnp.dot(h_ref[...], w_ref[...], preferred_element_type=F32)


def in_proj(x2d, mod, norm_g, w_p, rows_per_mod, mod_base):
    m, d = x2d.shape
    n = w_p.shape[1]
    tm, tn = min(1024, rows_per_mod), 512
    row_of = _mod_row_map(tm, rows_per_mod, mod_base)
    return pl.pallas_call(
        functools.partial(_inproj_kernel, tm=tm),
        grid=(m // tm, n // tn),
        in_specs=[pl.BlockSpec((tm, d), lambda i, j: (i, 0)),
                  pl.BlockSpec((1, N_MOD, d), lambda i, j: (row_of(i), 0, 0)),
                  pl.BlockSpec((1, d), lambda i, j: (0, 0)),
                  pl.BlockSpec((d, tn), lambda i, j: (0, j))],
        out_specs=pl.BlockSpec((tm, tn), lambda i, j: (i, j)),
        out_shape=jax.ShapeDtypeStruct((m, n), F32),
        scratch_shapes=[pltpu.VMEM((tm, d), BF16)],
        compiler_params=_params("arbitrary", "arbitrary"),
        name="in_proj",
    )(x2d, mod, norm_g.reshape(1, d), w_p)


def _shifted(ref, mu, c, n_chunks, t_len, grid_shift):
    ch = ROW_CHUNK
    base = pl.multiple_of(c * ch, ch)
    cur = ref[0, pl.ds(base, ch), :]
    row = lax.broadcasted_iota(jnp.int32, (ch, 1), 0)
    before = pltpu.roll(cur, 1, 0)
    after = pltpu.roll(cur, ch - 1, 0)
    if not grid_shift:
        prev = jnp.where(row == 0, 0.0, before)
        nxt = jnp.where(row == ch - 1, 0.0, after)
        mixed = 0.5 * (prev + nxt)
    else:
        gw = 64
        left = jnp.where(row % gw == 0, 0.0, before)
        right = jnp.where(row % gw == gw - 1, 0.0, after)
        up_start = pl.multiple_of(jnp.maximum(base - gw, 0), gw)
        dn_start = pl.multiple_of(jnp.minimum(base + ch, t_len - gw), gw)
        up_halo = jnp.where(c > 0, ref[0, pl.ds(up_start, gw), :], 0.0)
        dn_halo = jnp.where(c < n_chunks - 1, ref[0, pl.ds(dn_start, gw), :], 0.0)
        up = jnp.concatenate([up_halo, cur[: ch - gw]], axis=0)
        down = jnp.concatenate([cur[gw:], dn_halo], axis=0)
        mixed = 0.25 * (up + down + left + right)
    return cur + mu * (mixed - cur)


def _prescan_kernel(r_ref, k_ref, v_ref, lora_ref, mur_ref, muk_ref, muv_ref, mul_ref,
                    uph_ref, upl_ref, bias_ref, kk_ref, ka_ref, rk_ref,
                    r_o, v_o, a_o, bonus_o, w_o, kd_o, bd_o, lh_scr, ll_scr,
                    *, t_len, grid_shift, n_sub):
    n_chunks = t_len // ROW_CHUNK
    seg = _seg_ones()
    for sub in range(n_sub):
        c = pl.program_id(2) * n_sub + sub
        rows = pl.ds(pl.multiple_of(c * ROW_CHUNK, ROW_CHUNK), ROW_CHUNK)
        cols = slice(sub * ROW_CHUNK, (sub + 1) * ROW_CHUNK)
        sh = functools.partial(_shifted, c=c, n_chunks=n_chunks, t_len=t_len, grid_shift=grid_shift)

        @pl.when(pl.program_id(1) == 0)
        def _():
            lane = lax.broadcasted_iota(jnp.int32, (1, LORA_BLK), 1)
            lora = sh(lora_ref, mul_ref[...])
            lora = jnp.where(lane < 2 * LORA_R, jnp.tanh(lora), lora)
            hi, lo = _split2(lora)
            lh_scr[rows, :] = hi
            ll_scr[rows, :] = lo

        r = sh(r_ref, mur_ref[...])
        k = sh(k_ref, muk_ref[...])
        v = sh(v_ref, muv_ref[...])
        l_hi = lh_scr[rows, :]
        l_lo = ll_scr[rows, :]
        raw = (jnp.dot(l_hi, uph_ref[0], preferred_element_type=F32)
               + jnp.dot(l_hi, upl_ref[0], preferred_element_type=F32)
               + jnp.dot(l_lo, uph_ref[0], preferred_element_type=F32)) + bias_ref[0]
        kk = k * kk_ref[...]
        kk = kk * lax.rsqrt(_seg_sum(kk * kk, seg) + 1e-12)
        kd_sum = jnp.zeros_like(k)
        for d in range(2):
            decay = jnp.exp(-DECAY_SCALE * jax.nn.sigmoid(raw[:, d * LANES:(d + 1) * LANES]))
            iclr = jax.nn.sigmoid(raw[:, (2 + d) * LANES:(3 + d) * LANES])
            kd = k * (1.0 + (iclr - 1.0) * ka_ref[...])
            w_o[d, 0, :, cols] = decay.T
            kd_o[d, 0, :, cols] = kd.T
            bd_o[d, 0, :, cols] = (kk * iclr).T
            kd_sum = kd_sum + kd
        r_o[0, :, cols] = r.T
        v_o[0, :, cols] = v.T
        a_o[0, :, cols] = (-kk).T
        bonus_o[0, cols, :] = _seg_sum(r * (0.5 * kd_sum) * rk_ref[...], seg) * v


def pre_scan(proj, mu, up_w, up_b, k_k, k_a, r_k, grid_shift):
    b, t, _ = proj.shape
    up_hi, up_lo = _split2(up_w)
    if not grid_shift:
        assert t == ROW_CHUNK, "sequence shift handles one chunk per sequence"
    assert t % ROW_CHUNK == 0
    col = lambda off: pl.BlockSpec((1, t, LANES), lambda i, j, c: (i, 0, off + j))
    vec = lambda off: pl.BlockSpec((1, LANES), lambda i, j, c: (0, off + j))
    n_sub = next(n for n in (4, 2, 1) if t % (n * ROW_CHUNK) == 0)
    rows_step = n_sub * ROW_CHUNK
    out_t = pl.BlockSpec((1, LANES, rows_step), lambda i, j, c: (i, j, c))
    out_t2 = pl.BlockSpec((2, 1, LANES, rows_step), lambda i, j, c: (0, i, j, c))
    s_t = jax.ShapeDtypeStruct((b, RWKV_W, t), F32)
    s_t2 = jax.ShapeDtypeStruct((2, b, RWKV_W, t), F32)
    return pl.pallas_call(
        functools.partial(_prescan_kernel, t_len=t, grid_shift=grid_shift, n_sub=n_sub),
        grid=(b, HEAD_TILES, t // rows_step),
        in_specs=[col(0), col(HEAD_TILES), col(2 * HEAD_TILES),
                  pl.BlockSpec((1, t, LORA_BLK), lambda i, j, c: (i, 0, LORA_OFF // LORA_BLK)),
                  vec(0), vec(HEAD_TILES), vec(2 * HEAD_TILES),
                  pl.BlockSpec((1, LORA_BLK), lambda i, j, c: (0, LORA_OFF // LORA_BLK)),
                  pl.BlockSpec((1, LORA_BLK, 4 * LANES), lambda i, j, c: (j, 0, 0)),
                  pl.BlockSpec((1, LORA_BLK, 4 * LANES), lambda i, j, c: (j, 0, 0)),
                  pl.BlockSpec((1, 1, 4 * LANES), lambda i, j, c: (j, 0, 0)),
                  vec(0), vec(0), vec(0)],
        out_specs=[out_t, out_t, out_t,
                   pl.BlockSpec((1, rows_step, LANES), lambda i, j, c: (i, c, j)),
                   out_t2, out_t2, out_t2],
        out_shape=[s_t, s_t, s_t, jax.ShapeDtypeStruct((b, t, RWKV_W), F32), s_t2, s_t2, s_t2],
        scratch_shapes=[pltpu.VMEM((t, LORA_BLK), BF16), pltpu.VMEM((t, LORA_BLK), BF16)],
        compiler_params=_params("arbitrary", "arbitrary", "arbitrary"),
        name="pre_scan",
    )(proj, proj, proj, proj, mu, mu, mu, mu, up_hi, up_lo, up_b,
      k_k.reshape(1, RWKV_W), k_a.reshape(1, RWKV_W), r_k.reshape(1, RWKV_W))


def _lora_up_tables(w_decay_up, w_decay0, w_iclr_up, w_iclr0):
    mats = [w_decay_up[0], w_decay_up[1], w_iclr_up[0], w_iclr_up[1]]
    bias = [w_decay0[0], w_decay0[1], w_iclr0[0], w_iclr0[1]]
    up = jnp.zeros((HEAD_TILES, LORA_BLK, 4 * LANES), F32)
    for q, m in enumerate(mats):
        blk = m.astype(F32).reshape(LORA_R, HEAD_TILES, LANES).transpose(1, 0, 2)
        up = up.at[:, q * LORA_R:(q + 1) * LORA_R, q * LANES:(q + 1) * LANES].set(blk)
    b = jnp.stack([v.astype(F32).reshape(HEAD_TILES, LANES) for v in bias], axis=1)
    return up, b.reshape(HEAD_TILES, 1, 4 * LANES)


def _scan_chunk(ins, y_ref, yb_ref, s_scr, wr_scr, *, tc, backward=None):
    mixed = yb_ref is not None
    tt = (lambda t: t) if mixed else (lambda t: jnp.where(backward, tc - 1 - t, t))

    def view(i):
        ref = ins[i]
        return (lambda t: ref[tt(t)]), (lambda t, k: ref[tt(t), k:k + 1, :])

    (w_full, w_row), (k_full, k_row), (b_full, b_row), (_, a_row), (r_full, _), (v_full, _) = (
        view(i) for i in range(6))

    wr_scr[0] = w_full(0) * r_full(0)
    sa = jnp.zeros((HEAD_N, LANES), F32)
    y0 = jnp.zeros((HEAD_N, LANES), F32)
    for k in range(HEAD_N):
        s = s_scr[k]
        sa = sa + s * a_row(0, k)
        y0 = y0 + s * wr_scr[0, k:k + 1, :]

    def step(t, carry):
        sa, y0 = carry
        r_t = r_full(t)
        v_t = v_full(t)
        br = jnp.sum(b_full(t) * r_t, axis=0, keepdims=True)
        kr = jnp.sum(k_full(t) * r_t, axis=0, keepdims=True)
        y = y0 + sa * br + v_t * kr
        y_ref[tt(t)] = y
        if mixed:
            yb_ref[tc - 1 - t] = y
        tn = jnp.minimum(t + 1, tc - 1)
        slot = (t + 1) % 2
        wr_scr[slot] = w_full(tn) * r_full(tn)
        sa_n = jnp.zeros((HEAD_N, LANES), F32)
        y0_n = jnp.zeros((HEAD_N, LANES), F32)
        for k in range(HEAD_N):
            s = s_scr[k] * w_row(t, k) + sa * b_row(t, k) + v_t * k_row(t, k)
            s_scr[k] = s
            sa_n = sa_n + s * a_row(tn, k)
            y0_n = y0_n + s * wr_scr[slot, k:k + 1, :]
        return sa_n, y0_n

    lax.fori_loop(0, tc, step, (sa, y0))


def _scan_kernel(*refs, tc, mixed):
    ins, rest = refs[:6], refs[6:]
    if mixed:
        s0_ref, y_ref, yb_ref, sfin_ref, s_scr, wr_scr = rest
        backward = None
    else:
        s0_ref, y_ref, sfin_ref, s_scr, wr_scr = rest
        yb_ref = None
        backward = pl.program_id(0) >= pl.num_programs(0) // 2
    c = pl.program_id(1)

    @pl.when(c == 0)
    def _():
        s_scr[...] = s0_ref[0]

    _scan_chunk(ins, y_ref, yb_ref, s_scr, wr_scr, tc=tc, backward=backward)

    @pl.when(c == pl.num_programs(1) - 1)
    def _():
        sfin_ref[0] = s_scr[...]


def wkv_scan(ops, s0):
    t, _, lanes = ops[0].shape
    groups = lanes // LANES
    mixed = groups == 1
    tc = 64
    nc = t // tc
    st = pl.BlockSpec((1, HEAD_N, HEAD_N, LANES), lambda g, c: (g, 0, 0, 0))
    y_shape = jax.ShapeDtypeStruct((t, HEAD_N, lanes), F32)
    if mixed:
        fwd = pl.BlockSpec((tc, HEAD_N, LANES), lambda g, c: (c, 0, g))
        bwd = pl.BlockSpec((tc, HEAD_N, LANES), lambda g, c: (nc - 1 - c, 0, g))
        in_specs, operands = [fwd] * 6 + [st], list(ops) + [s0]
        out_specs, out_shape = [fwd, bwd, st], [y_shape, y_shape]
    else:
        half_g = groups // 2
        tmap = lambda g, c: jnp.where(g >= half_g, nc - 1 - c, c)
        seq = pl.BlockSpec((tc, HEAD_N, LANES), lambda g, c: (tmap(g, c), 0, g))
        shared = pl.BlockSpec((tc, HEAD_N, LANES), lambda g, c: (tmap(g, c), 0, g % half_g))
        in_specs = [seq if o.shape[2] == lanes else shared for o in ops] + [st]
        operands = list(ops) + [s0]
        out_specs, out_shape = [seq, st], [y_shape]
    out_shape.append(jax.ShapeDtypeStruct((groups, HEAD_N, HEAD_N, LANES), F32))
    res = pl.pallas_call(
        functools.partial(_scan_kernel, tc=tc, mixed=mixed),
        grid=(groups, nc),
        in_specs=in_specs,
        out_specs=out_specs,
        out_shape=out_shape,
        scratch_shapes=[pltpu.VMEM((HEAD_N, HEAD_N, LANES), F32),
                        pltpu.VMEM((2, HEAD_N, LANES), F32)],
        compiler_params=_params("arbitrary", "arbitrary"),
        name="wkv_scan",
    )(*operands)
    half = lanes // 2
    if mixed:
        y_f, y_b, s_fin = res
        return y_f[..., :half] + y_b[..., half:], s_fin
    y, s_fin = res
    return y[..., :half] + y[..., half:], s_fin


K_BLOCK = 8


def _relayout_kernel(*refs, n_ops, rb):
    ins, outs, scr = refs[:n_ops], refs[n_ops:2 * n_ops], refs[2 * n_ops]
    for x_ref, o_ref in zip(ins, outs):
        x2 = x_ref.reshape(rb * K_BLOCK, LANES)
        for r0 in range(0, rb, LANES):
            for kk in range(K_BLOCK):
                m = x2[pl.ds(r0 * K_BLOCK + kk, LANES, stride=K_BLOCK), :]
                scr[pl.ds(kk, LANES, stride=K_BLOCK), :] = m.T
            o_ref[:, :, r0:r0 + LANES] = scr[...].reshape(LANES, K_BLOCK, LANES)


def relayout_to_scan(xs):
    rows, _, t = xs[0].shape
    assert rows % LANES == 0
    rb = min(rows, 1024)
    n = len(xs)
    return pl.pallas_call(
        functools.partial(_relayout_kernel, n_ops=n, rb=rb),
        grid=(rows // rb, HEAD_N // K_BLOCK, t // LANES),
        in_specs=[pl.BlockSpec((rb, K_BLOCK, LANES), lambda r, k, c: (r, k, c))] * n,
        out_specs=[pl.BlockSpec((LANES, K_BLOCK, rb), lambda r, k, c: (c, k, r))] * n,
        out_shape=[jax.ShapeDtypeStruct((t, HEAD_N, rows), F32)] * n,
        scratch_shapes=[pltpu.VMEM((LANES * K_BLOCK, LANES), F32)],
        compiler_params=_params("arbitrary", "arbitrary", "arbitrary"),
        name="relayout_to_scan",
    )(*xs)


def _relayout_merged_kernel(*refs, n_ops):
    n = n_ops
    fw, bw, outs, scr = refs[:n], refs[n:2 * n], refs[2 * n:3 * n], refs[3 * n]
    half = LANES // 2
    r = lax.broadcasted_iota(jnp.int32, (LANES, LANES), 0)
    c = lax.broadcasted_iota(jnp.int32, (LANES, LANES), 1)
    flip = (r + c == LANES - 1).astype(BF16)
    for f_ref, b_ref, o_ref in zip(fw, bw, outs):
        f2 = f_ref.reshape(half * K_BLOCK, LANES)
        b2 = b_ref.reshape(half * K_BLOCK, LANES)
        for kk in range(K_BLOCK):
            m_f = f2[pl.ds(kk, half, stride=K_BLOCK), :]
            m_b = b2[pl.ds(kk, half, stride=K_BLOCK), :]
            m_b = sum(jnp.dot(part, flip, preferred_element_type=F32) for part in _split3(m_b))
            scr[pl.ds(kk, LANES, stride=K_BLOCK), :] = jnp.concatenate([m_f, m_b], axis=0).T
        o_ref[...] = scr[...].reshape(LANES, K_BLOCK, LANES)


def relayout_merged(xs):
    t = xs[0].shape[2]
    nt = t // LANES
    half = LANES // 2
    assert all(x.shape[0] in (LANES, half) for x in xs)
    fwd = [pl.BlockSpec((half, K_BLOCK, LANES), lambda k, c: (0, k, c))] * len(xs)
    bwd = [pl.BlockSpec((half, K_BLOCK, LANES),
                        functools.partial(lambda blk, k, c: (blk, k, nt - 1 - c), x.shape[0] // half - 1))
           for x in xs]
    n = len(xs)
    return pl.pallas_call(
        functools.partial(_relayout_merged_kernel, n_ops=n),
        grid=(HEAD_N // K_BLOCK, nt),
        in_specs=fwd + bwd,
        out_specs=[pl.BlockSpec((LANES, K_BLOCK, LANES), lambda k, c: (c, k, 0))] * n,
        out_shape=[jax.ShapeDtypeStruct((t, HEAD_N, LANES), F32)] * n,
        scratch_shapes=[pltpu.VMEM((LANES * K_BLOCK, LANES), F32)],
        compiler_params=_params("arbitrary", "arbitrary"),
        name="relayout_merged",
    )(*xs, *xs)


def _scan_rows(x):
    return x.reshape(-1, HEAD_N, x.shape[-1])


def _from_scan_layout(y, b):
    t = y.shape[0]
    return y.reshape(t, HEAD_N, b, N_HEADS).transpose(2, 0, 3, 1).reshape(b, t, RWKV_W)


def _postscan_kernel(y_ref, bonus_ref, g_ref, gw_ref, gb_ref, o_ref):
    seg = _seg_ones()
    for j in range(HEAD_TILES):
        cols = slice(j * LANES, (j + 1) * LANES)
        y = y_ref[0, :, cols]
        mu = _seg_sum(y, seg) * (1.0 / HEAD_N)
        d = y - mu
        var = _seg_sum(d * d, seg) * (1.0 / HEAD_N)
        yn = d * lax.rsqrt(var + GN_EPS) * gw_ref[:, cols] + gb_ref[:, cols]
        o_ref[0, :, cols] = ((yn + bonus_ref[0, :, cols]) * jax.nn.sigmoid(g_ref[0, :, cols])).astype(BF16)


def post_scan(y, bonus, proj, gn_w, gn_b):
    b, t, _ = y.shape
    ch = ROW_CHUNK
    blk = pl.BlockSpec((1, ch, RWKV_W), lambda i, c: (i, c, 0))
    vec = pl.BlockSpec((1, RWKV_W), lambda i, c: (0, 0))
    return pl.pallas_call(
        _postscan_kernel,
        grid=(b, t // ch),
        in_specs=[blk, blk,
                  pl.BlockSpec((1, ch, RWKV_W), lambda i, c: (i, c, G_OFF // RWKV_W)),
                  vec, vec],
        out_specs=blk,
        out_shape=jax.ShapeDtypeStruct((b, t, RWKV_W), BF16),
        compiler_params=_params("arbitrary", "arbitrary"),
        name="post_scan",
    )(y, bonus, proj, gn_w.reshape(1, RWKV_W), gn_b.reshape(1, RWKV_W))


def _fnet_kernel(f_ref, cc_ref, ct_ref, o_ref, xcs_ref, *, t_len, scale):
    @pl.when(pl.program_id(2) == 0)
    def _():
        xc = jnp.dot(f_ref[0].astype(BF16), cc_ref[...], preferred_element_type=F32)
        xcs_ref[0:t_len, :] = xc[:, :FOURIER_GW].astype(BF16)
        xcs_ref[t_len:2 * t_len, :] = xc[:, FOURIER_GW:].astype(BF16)

    out = jnp.dot(ct_ref[...], xcs_ref[...], preferred_element_type=F32)
    o_ref[0] = (out * scale).astype(BF16)


def _fnet_short_kernel(f_ref, cc_ref, ct_ref, o_ref, *, t_len, scale):
    gw = FOURIER_GW
    for g in range(N_FOURIER_GROUPS):
        cols = slice(g * gw, (g + 1) * gw)
        xc = jnp.dot(f_ref[0, :, cols].astype(BF16), cc_ref[...], preferred_element_type=F32)
        out = (jnp.dot(ct_ref[:, :t_len], xc[:, :gw].astype(BF16), preferred_element_type=F32)
               + jnp.dot(ct_ref[:, t_len:], xc[:, gw:].astype(BF16), preferred_element_type=F32))
        o_ref[0, :, cols] = (out * scale).astype(BF16)


def _dft_tables(n):
    j = lax.broadcasted_iota(jnp.int32, (n, n), 0)
    k = lax.broadcasted_iota(jnp.int32, (n, n), 1)
    ang = ((j * k) % n).astype(F32) * (2.0 * math.pi / n)
    return jnp.cos(ang), jnp.sin(ang)


def fnet_mix(proj):
    b, t, _ = proj.shape
    cc, sc = _dft_tables(FOURIER_GW)
    ct, st = _dft_tables(t)
    cc2 = jnp.concatenate([cc, sc], axis=1).astype(BF16)
    ct2 = jnp.concatenate([ct, -st], axis=1).astype(BF16)
    scale = 1.0 / math.sqrt(t * FOURIER_GW)
    if t <= 512:
        blk = pl.BlockSpec((1, t, RWKV_W), lambda i: (i, 0, 0))
        return pl.pallas_call(
            functools.partial(_fnet_short_kernel, t_len=t, scale=scale),
            grid=(b,),
            in_specs=[pl.BlockSpec((1, t, RWKV_W), lambda i: (i, 0, F_OFF // RWKV_W)),
                      pl.BlockSpec((FOURIER_GW, 2 * FOURIER_GW), lambda i: (0, 0)),
                      pl.BlockSpec((t, 2 * t), lambda i: (0, 0))],
            out_specs=blk,
            out_shape=jax.ShapeDtypeStruct((b, t, RWKV_W), BF16),
            compiler_params=_params("arbitrary"),
            name="fnet_mix",
        )(proj, cc2, ct2)
    tq = min(512, t)
    return pl.pallas_call(
        functools.partial(_fnet_kernel, t_len=t, scale=scale),
        grid=(b, N_FOURIER_GROUPS, t // tq),
        in_specs=[pl.BlockSpec((1, t, FOURIER_GW), lambda i, g, q: (i, 0, F_OFF // FOURIER_GW + g)),
                  pl.BlockSpec((FOURIER_GW, 2 * FOURIER_GW), lambda i, g, q: (0, 0)),
                  pl.BlockSpec((tq, 2 * t), lambda i, g, q: (q, 0))],
        out_specs=pl.BlockSpec((1, tq, FOURIER_GW), lambda i, g, q: (i, q, g)),
        out_shape=jax.ShapeDtypeStruct((b, t, RWKV_W), BF16),
        scratch_shapes=[pltpu.VMEM((2 * t, FOURIER_GW), BF16)],
        compiler_params=_params("arbitrary", "arbitrary", "arbitrary"),
        name="fnet_mix",
    )(proj, cc2, ct2)


def _outproj_kernel(yr_ref, yf_ref, w_ref, x_ref, mod_ref, o_ref):
    m = jnp.dot(yr_ref[...], w_ref[0:RWKV_W, :], preferred_element_type=F32)
    m = m + jnp.dot(yf_ref[...], w_ref[RWKV_W:2 * RWKV_W, :], preferred_element_type=F32)
    o_ref[...] = x_ref[...] + mod_ref[0, 2:3, :] * m


def out_proj(yr, yf, w_bf, x2d, mod, rows_per_mod, mod_base):
    m, d = x2d.shape
    tm, tn = min(1024, rows_per_mod), 512
    row_of = _mod_row_map(tm, rows_per_mod, mod_base)
    return pl.pallas_call(
        _outproj_kernel,
        grid=(m // tm, d // tn),
        in_specs=[pl.BlockSpec((tm, RWKV_W), lambda i, j: (i, 0)),
                  pl.BlockSpec((tm, RWKV_W), lambda i, j: (i, 0)),
                  pl.BlockSpec((2 * RWKV_W, tn), lambda i, j: (0, j)),
                  pl.BlockSpec((tm, tn), lambda i, j: (i, j)),
                  pl.BlockSpec((1, N_MOD, tn), lambda i, j: (row_of(i), 0, j))],
        out_specs=pl.BlockSpec((tm, tn), lambda i, j: (i, j)),
        out_shape=jax.ShapeDtypeStruct((m, d), F32),
        compiler_params=_params("arbitrary", "arbitrary"),
        name="out_proj",
    )(yr, yf, w_bf, x2d, mod)


def _norm2_router_kernel(x_ref, mod_ref, g_ref, rwh_ref, rwl_ref, h_ref, aff_ref):
    h = _rms_modulate(x_ref[...], g_ref[...], mod_ref[0, 4:5, :], mod_ref[0, 3:4, :])
    h_ref[...] = h
    h_hi, h_lo = _split2(h)
    logits = (jnp.dot(h_hi, rwh_ref[...], preferred_element_type=F32)
              + jnp.dot(h_hi, rwl_ref[...], preferred_element_type=F32)
              + jnp.dot(h_lo, rwh_ref[...], preferred_element_type=F32))
    z = logits - jnp.max(logits, axis=-1, keepdims=True)
    e = jnp.exp(z)
    aff_ref[...] = e / jnp.sum(e, axis=-1, keepdims=True)


def norm2_router(x2d, mod, norm_g, router_w, rows_per_mod, mod_base):
    m, d = x2d.shape
    tm = ROW_CHUNK
    row_of = _mod_row_map(tm, rows_per_mod, mod_base)
    rw_hi, rw_lo = _split2(router_w)
    return pl.pallas_call(
        _norm2_router_kernel,
        grid=(m // tm,),
        in_specs=[pl.BlockSpec((tm, d), lambda i: (i, 0)),
                  pl.BlockSpec((1, N_MOD, d), lambda i: (row_of(i), 0, 0)),
                  pl.BlockSpec((1, d), lambda i: (0, 0)),
                  pl.BlockSpec((d, N_EXPERTS), lambda i: (0, 0)),
                  pl.BlockSpec((d, N_EXPERTS), lambda i: (0, 0))],
        out_specs=[pl.BlockSpec((tm, d), lambda i: (i, 0)),
                   pl.BlockSpec((tm, N_EXPERTS), lambda i: (i, 0))],
        out_shape=[jax.ShapeDtypeStruct((m, d), F32),
                   jax.ShapeDtypeStruct((m, N_EXPERTS), F32)],
        compiler_params=_params("arbitrary"),
        name="norm2_router",
    )(x2d, mod, norm_g.reshape(1, d), rw_hi, rw_lo)


GATHER_ROWS = 256
PAIR_CHUNK = LANES
COMBINE_ROWS = 256


def _issue_row_gather(idx_ref, base, n_rows, src_hbm, dst, sem):
    def body(r, carry):
        row = idx_ref[base + r]
        pltpu.make_async_copy(src_hbm.at[pl.ds(row, 1), :], dst.at[pl.ds(r, 1), :], sem).start()
        return carry
    lax.fori_loop(0, n_rows, body, 0, unroll=8)


def _wait_row_gather(n_rows, src_hbm, dst, sem):
    pltpu.make_async_copy(src_hbm.at[pl.ds(0, n_rows), :], dst, sem).wait()


def _gather_kernel(idx_ref, h_hbm, o_ref, buf, sem):
    s = pl.program_id(0)
    slot = s % 2

    @pl.when(s == 0)
    def _():
        _issue_row_gather(idx_ref, 0, GATHER_ROWS, h_hbm, buf.at[0], sem.at[0])

    @pl.when(s + 1 < pl.num_programs(0))
    def _():
        _issue_row_gather(idx_ref, (s + 1) * GATHER_ROWS, GATHER_ROWS, h_hbm,
                          buf.at[1 - slot], sem.at[1 - slot])

    _wait_row_gather(GATHER_ROWS, h_hbm, buf.at[slot], sem.at[slot])
    o_ref[...] = buf[slot].astype(BF16)


def gather_tokens(h, idx_flat):
    n, d = h.shape
    rows = idx_flat.shape[0]
    return pl.pallas_call(
        _gather_kernel,
        grid_spec=pltpu.PrefetchScalarGridSpec(
            num_scalar_prefetch=1,
            grid=(rows // GATHER_ROWS,),
            in_specs=[pl.BlockSpec(memory_space=pl.ANY)],
            out_specs=pl.BlockSpec((GATHER_ROWS, d), lambda s, idx: (s, 0)),
            scratch_shapes=[pltpu.VMEM((2, GATHER_ROWS, d), F32), pltpu.SemaphoreType.DMA((2,))]),
        out_shape=jax.ShapeDtypeStruct((rows, d), BF16),
        compiler_params=_params("arbitrary"),
        name="gather_tokens",
    )(idx_flat, h)


def _ffn_accumulate(xe_ref, w1_ref, w3_ref, w2_ref, o_ref, width):
    x = xe_ref[0]
    h1 = jnp.dot(x, w1_ref[0, :, :width].astype(BF16), preferred_element_type=F32)
    h3 = jnp.dot(x, w3_ref[0, :, :width].astype(BF16), preferred_element_type=F32)
    hid = h1 * jax.nn.sigmoid(h1) * h3
    o_ref[0] += jnp.dot(hid.astype(BF16), w2_ref[0, :width, :].astype(BF16),
                        preferred_element_type=F32)


def _expert_kernel(xe_ref, w1_ref, w3_ref, w2_ref, gate_ref, o_ref, *, tf, tail):
    f = pl.program_id(1)
    last = pl.num_programs(1) - 1
    step = functools.partial(_ffn_accumulate, xe_ref, w1_ref, w3_ref, w2_ref, o_ref)

    @pl.when(f == 0)
    def _():
        o_ref[...] = jnp.zeros_like(o_ref)

    if tail == tf:
        step(tf)
    else:
        @pl.when(f < last)
        def _():
            step(tf)

        @pl.when(f == last)
        def _():
            step(tail)

    @pl.when(f == last)
    def _():
        o_ref[0] = o_ref[0] * gate_ref[0]


def expert_ffn(xe, w1, w3, w2, gate):
    e, cap, d = xe.shape
    ff = w1.shape[2]
    tf = 512
    nf = pl.cdiv(ff, tf)
    return pl.pallas_call(
        functools.partial(_expert_kernel, tf=tf, tail=ff - (nf - 1) * tf),
        grid=(e, nf),
        in_specs=[pl.BlockSpec((1, cap, d), lambda i, f: (i, 0, 0)),
                  pl.BlockSpec((1, d, tf), lambda i, f: (i, 0, f)),
                  pl.BlockSpec((1, d, tf), lambda i, f: (i, 0, f)),
                  pl.BlockSpec((1, tf, d), lambda i, f: (i, f, 0)),
                  pl.BlockSpec((1, cap, 1), lambda i, f: (i, 0, 0))],
        out_specs=pl.BlockSpec((1, cap, d), lambda i, f: (i, 0, 0)),
        out_shape=jax.ShapeDtypeStruct((e, cap, d), F32),
        compiler_params=_params("arbitrary", "arbitrary"),
        name="expert_ffn",
    )(xe, w1, w3, w2, gate.reshape(e, cap, 1))


def _combine_final_kernel(rows_ref, ptr_ref, tok_ref, ye_hbm, x_ref, mod_ref, g_ref, o_ref,
                          buf, sem, acc_ref, *, tm):
    i = pl.program_id(0)
    p0 = ptr_ref[i]
    p1 = ptr_ref[i + 1]
    c0 = p0 // PAIR_CHUNK
    c1 = jnp.where(p1 > p0, (p1 - 1) // PAIR_CHUNK + 1, c0)
    acc_ref[...] = jnp.zeros_like(acc_ref)

    @pl.when(c1 > c0)
    def _():
        _issue_row_gather(rows_ref, c0 * PAIR_CHUNK, PAIR_CHUNK, ye_hbm, buf.at[0], sem.at[0])

    def chunk(c, carry):
        slot = (c - c0) % 2

        @pl.when(c + 1 < c1)
        def _():
            _issue_row_gather(rows_ref, (c + 1) * PAIR_CHUNK, PAIR_CHUNK, ye_hbm,
                              buf.at[1 - slot], sem.at[1 - slot])

        _wait_row_gather(PAIR_CHUNK, ye_hbm, buf.at[slot], sem.at[slot])
        local = tok_ref[pl.ds(c, 1), :] - i * tm
        onehot = (lax.broadcasted_iota(jnp.int32, (tm, PAIR_CHUNK), 0) == local).astype(BF16)
        acc_ref[...] += jnp.dot(onehot, buf[slot].astype(BF16), preferred_element_type=F32)
        return carry

    lax.fori_loop(c0, c1, chunk, 0)
    x = x_ref[...] + mod_ref[0, 5:6, :] * acc_ref[...]
    ms = jnp.mean(x * x, axis=-1, keepdims=True)
    o_ref[...] = x * lax.rsqrt(ms + NORM_EPS) * g_ref[...]


def combine_final(x2d, ye_flat, pair_rows, pair_toks, tile_ptr, mod, final_g, rows_per_mod, mod_base):
    m, d = x2d.shape
    tm = COMBINE_ROWS
    n_pairs = pair_rows.shape[0]
    row_of = _mod_row_map(tm, rows_per_mod, mod_base)
    return pl.pallas_call(
        functools.partial(_combine_final_kernel, tm=tm),
        grid_spec=pltpu.PrefetchScalarGridSpec(
            num_scalar_prefetch=2,
            grid=(m // tm,),
            in_specs=[pl.BlockSpec((n_pairs // PAIR_CHUNK, PAIR_CHUNK), lambda i, r, p: (0, 0)),
                      pl.BlockSpec(memory_space=pl.ANY),
                      pl.BlockSpec((tm, d), lambda i, r, p: (i, 0)),
                      pl.BlockSpec((1, N_MOD, d), lambda i, r, p: (row_of(i), 0, 0)),
                      pl.BlockSpec((1, d), lambda i, r, p: (0, 0))],
            out_specs=pl.BlockSpec((tm, d), lambda i, r, p: (i, 0)),
            scratch_shapes=[pltpu.VMEM((2, PAIR_CHUNK, d), F32), pltpu.SemaphoreType.DMA((2,)),
                            pltpu.VMEM((tm, d), F32)]),
        out_shape=jax.ShapeDtypeStruct((m, d), F32),
        compiler_params=_params("arbitrary"),
        name="combine_final",
    )(pair_rows, tile_ptr, pair_toks.reshape(n_pairs // PAIR_CHUNK, PAIR_CHUNK), ye_flat,
      x2d, mod, final_g.reshape(1, d))


def _prepare_in_proj(w_in, shift_mu):
    d = w_in.shape[0]
    rkv, lora, gf = w_in[:, :G_OFF], w_in[:, G_OFF:SHIFT_W], w_in[:, SHIFT_W:]
    w_p = jnp.concatenate([rkv, gf, lora, jnp.zeros((d, LANES), w_in.dtype)], axis=1).astype(BF16)
    mu = jnp.concatenate([shift_mu[:G_OFF], jnp.zeros((2 * RWKV_W,), F32), shift_mu[G_OFF:],
                          jnp.zeros((LANES,), F32)])
    return w_p, mu.reshape(1, IN_PAD_W)


def _mix_inputs(x, mod, rows_per_mod, mod_base, grid_shift, p):
    b, t, d = x.shape
    x2d = x.reshape(b * t, d)
    proj = in_proj(x2d, mod, p["norm1_g"], p["w_in_p"], rows_per_mod, mod_base)
    proj = proj.reshape(b, t, IN_PAD_W)
    r, v, a, bonus, w, kd, bd = pre_scan(proj, p["mu"], p["up_w"], p["up_b"],
                                         p["k_k"], p["k_a"], p["r_k"], grid_shift)
    per_dir = [_scan_rows(w), _scan_rows(kd), _scan_rows(bd)]
    shared = [_scan_rows(a), _scan_rows(r), _scan_rows(v)]
    if 2 * b * N_HEADS == LANES:
        ops = relayout_merged(per_dir + shared)
    else:
        ops = list(relayout_to_scan(per_dir)) + list(relayout_to_scan(shared))
    return {"x2d": x2d, "proj": proj, "bonus": bonus, "ops": list(ops),
            "mod": (mod, rows_per_mod, mod_base), "shape": (b, t, d)}


def _route(st, y, p):
    b, t, d = st["shape"]
    mod_args = st["mod"]
    yr = post_scan(_from_scan_layout(y, b), st["bonus"], st["proj"], p["gn_w"], p["gn_b"])
    yf = fnet_mix(st["proj"])
    x1 = out_proj(yr.reshape(b * t, RWKV_W), yf.reshape(b * t, RWKV_W), p["w_out_bf"],
                  st["x2d"], *mod_args)
    h2, aff = norm2_router(x1, mod_args[0], p["norm2_g"], p["router_w"], *mod_args[1:])
    n = b * t
    cap = n * CAPACITY_FACTOR // N_EXPERTS
    gate, idx = lax.top_k(aff.T, cap)
    idx_flat = idx.reshape(-1).astype(jnp.int32)
    xe = gather_tokens(h2, idx_flat).reshape(N_EXPERTS, cap, d)
    return {"x1": x1, "xe": xe, "gate": gate, "idx_flat": idx_flat, "mod": mod_args,
            "shape": (b, t, d)}


def _combine(rt, ye, p):
    b, t, d = rt["shape"]
    n = b * t
    idx_flat = rt["idx_flat"]
    pair_toks, pair_rows = lax.sort_key_val(idx_flat, jnp.arange(idx_flat.shape[0], dtype=jnp.int32))
    tile_ptr = jnp.searchsorted(pair_toks, jnp.arange(0, n + 1, COMBINE_ROWS, dtype=jnp.int32),
                                side="left").astype(jnp.int32)
    mod, rows_per_mod, mod_base = rt["mod"]
    out = combine_final(rt["x1"], ye.reshape(-1, d), pair_rows, pair_toks, tile_ptr, mod,
                        p["final_norm_g"], rows_per_mod, mod_base)
    return out.reshape(b, t, d)


def _experts(p):
    return p["exp_w1"], p["exp_w3"], p["exp_w2"]


def _layer(x, mod, rows_per_mod, mod_base, s0, grid_shift, p):
    st = _mix_inputs(x, mod, rows_per_mod, mod_base, grid_shift, p)
    y, s_fin = wkv_scan(st["ops"], s0)
    rt = _route(st, y, p)
    ye = expert_ffn(rt["xe"], *_experts(p), rt["gate"])
    return _combine(rt, ye, p), s_fin


def kernel(x_prompt, x_sample, state_wkv, c, c_ctx, ada_w, ada_b, norm1_g, norm2_g, w_in, shift_mu,
           w_decay_up, w_decay0, w_iclr_up, w_iclr0, k_k, k_a, r_k, gn_w, gn_b, w_out, router_w,
           exp_w1, exp_w3, exp_w2, final_norm_g):
    depth = ada_w.shape[0]
    assert depth == 1, "single-layer trunk"
    l = 0
    bc, tc_len, d = x_prompt.shape
    bl, tl, _ = x_sample.shape
    cvecs = jnp.concatenate([c_ctx[None], c], axis=0)
    mod = ada_mod(cvecs, ada_w[l], ada_b[l]).reshape(1 + bl, N_MOD, d)
    up_w, up_b = _lora_up_tables(w_decay_up[l], w_decay0[l], w_iclr_up[l], w_iclr0[l])
    w_in_p, mu = _prepare_in_proj(w_in[l], shift_mu[l])
    p = {
        "norm1_g": norm1_g[l], "norm2_g": norm2_g[l], "final_norm_g": final_norm_g,
        "w_in_p": w_in_p, "mu": mu, "up_w": up_w, "up_b": up_b,
        "k_k": k_k[l], "k_a": k_a[l], "r_k": r_k[l], "gn_w": gn_w[l], "gn_b": gn_b[l],
        "w_out_bf": w_out[l].astype(BF16), "router_w": router_w[l],
        "exp_w1": exp_w1[l], "exp_w3": exp_w3[l], "exp_w2": exp_w2[l],
    }
    groups_c = 2 * bc * N_HEADS // LANES
    s0_ctx = jnp.zeros((groups_c, HEAD_N, HEAD_N, LANES), F32)
    ctx = _mix_inputs(x_prompt, mod, bc * tc_len, 0, False, p)
    y_ctx, s_ctx = wkv_scan(ctx["ops"], s0_ctx)
    ctx_rt = _route(ctx, y_ctx, p)
    s_ctx = s_ctx.transpose(1, 2, 0, 3).reshape(HEAD_N, HEAD_N, 2, bc, N_HEADS)
    new_state = s_ctx.transpose(3, 2, 4, 1, 0)[:, None]
    groups_l = 2 * bl * N_HEADS // LANES
    s0_lat = state_wkv[:, l].astype(F32).transpose(4, 3, 1, 0, 2)
    s0_lat = s0_lat.reshape(HEAD_N, HEAD_N, groups_l, LANES).transpose(2, 0, 1, 3)
    lat = _mix_inputs(x_sample, mod, tl, 1, True, p)
    ye_ctx = expert_ffn(ctx_rt["xe"], *_experts(p), ctx_rt["gate"])
    y_lat, _ = wkv_scan(lat["ops"], s0_lat)
    lat_rt = _route(lat, y_lat, p)
    ye_lat = expert_ffn(lat_rt["xe"], *_experts(p), lat_rt["gate"])
    return (_combine(ctx_rt, ye_ctx, p), _combine(lat_rt, ye_lat, p), new_state)
```

```python
import functools
import math

import jax
import jax.numpy as jnp
from jax import lax
from jax.experimental import pallas as pl
from jax.experimental.pallas import tpu as pltpu

F32 = jnp.float32
BF16 = jnp.bfloat16

D_MODEL = 2048
RWKV_W = 1024
HEAD_N = 64
N_HEADS = 16
LORA_R = 96
N_EXPERTS = 16
EXPERT_FF = 5504
CAPACITY_FACTOR = 2
N_MOD = 6
NORM_EPS = 1e-6
GN_EPS = 64e-5
FOURIER_GW = 256
N_FOURIER_GROUPS = 4

LANES = 128
HEAD_TILES = RWKV_W // LANES
SHIFT_W = 3 * RWKV_W + 4 * LORA_R
LORA_W = 4 * LORA_R
LORA_BLK = LORA_W + LANES
G_OFF = 3 * RWKV_W
F_OFF = G_OFF + RWKV_W
LORA_OFF = F_OFF + RWKV_W
IN_PAD_W = LORA_OFF + LORA_BLK
ROW_CHUNK = 256
VMEM_LIMIT = 56 * 1024 * 1024
DECAY_SCALE = math.exp(-0.5)


def _params(*sem):
    return pltpu.CompilerParams(dimension_semantics=sem, vmem_limit_bytes=VMEM_LIMIT)


def _seg_ones():
    r = lax.broadcasted_iota(jnp.int32, (LANES, LANES), 0) // HEAD_N
    c = lax.broadcasted_iota(jnp.int32, (LANES, LANES), 1) // HEAD_N
    return (r == c).astype(BF16)


def _split2(x):
    hi = x.astype(BF16)
    return hi, (x - hi.astype(F32)).astype(BF16)


def _split3(x):
    hi = x.astype(BF16)
    rem = x - hi.astype(F32)
    mid = rem.astype(BF16)
    return hi, mid, (rem - mid.astype(F32)).astype(BF16)


def _seg_sum(x, seg):
    return sum(jnp.dot(p, seg, preferred_element_type=F32) for p in _split3(x))


def _ada_kernel(c_ref, w_ref, b_ref, o_ref):
    c = c_ref[...]
    s = c * jax.nn.sigmoid(c)
    o_ref[...] = jnp.dot(s.astype(BF16), w_ref[...].astype(BF16),
                         preferred_element_type=F32) + b_ref[...]


def ada_mod(cvecs, ada_w, ada_b):
    rows, d = cvecs.shape
    n = ada_w.shape[1]
    tn = 1024
    return pl.pallas_call(
        _ada_kernel,
        grid=(n // tn,),
        in_specs=[pl.BlockSpec((rows, d), lambda j: (0, 0)),
                  pl.BlockSpec((d, tn), lambda j: (0, j)),
                  pl.BlockSpec((1, tn), lambda j: (0, j))],
        out_specs=pl.BlockSpec((rows, tn), lambda j: (0, j)),
        out_shape=jax.ShapeDtypeStruct((rows, n), F32),
        compiler_params=_params("arbitrary"),
        name="ada_mod",
    )(cvecs, ada_w, ada_b.reshape(1, n))


def _mod_row_map(tm, rows_per_mod, mod_base):
    return lambda i: mod_base + (i * tm) // rows_per_mod


def _rms_modulate(x, g, scale, shift):
    ms = jnp.mean(x * x, axis=-1, keepdims=True)
    return x * lax.rsqrt(ms + NORM_EPS) * g * (1.0 + scale) + shift


def _inproj_kernel(x_ref, mod_ref, g_ref, w_ref, o_ref, h_ref, *, tm):
    @pl.when(pl.program_id(1) == 0)
    def _():
        def body(c, carry):
            rows = pl.ds(pl.multiple_of(c * ROW_CHUNK, ROW_CHUNK), ROW_CHUNK)
            h = _rms_modulate(x_ref[rows, :], g_ref[...], mod_ref[0, 1:2, :], mod_ref[0, 0:1, :])
            h_ref[rows, :] = h.astype(BF16)
            return carry
        lax.fori_loop(0, tm // ROW_CHUNK, body, 0)

    o_ref[...] = jnp.dot(h_ref[...], w_ref[...], preferred_element_type=F32)


def in_proj(x2d, mod, norm_g, w_p, rows_per_mod, mod_base):
    m, d = x2d.shape
    n = w_p.shape[1]
    tm, tn = min(1024, rows_per_mod), 512
    row_of = _mod_row_map(tm, rows_per_mod, mod_base)
    return pl.pallas_call(
        functools.partial(_inproj_kernel, tm=tm),
        grid=(m // tm, n // tn),
        in_specs=[pl.BlockSpec((tm, d), lambda i, j: (i, 0)),
                  pl.BlockSpec((1, N_MOD, d), lambda i, j: (row_of(i), 0, 0)),
                  pl.BlockSpec((1, d), lambda i, j: (0, 0)),
                  pl.BlockSpec((d, tn), lambda i, j: (0, j))],
        out_specs=pl.BlockSpec((tm, tn), lambda i, j: (i, j)),
        out_shape=jax.ShapeDtypeStruct((m, n), F32),
        scratch_shapes=[pltpu.VMEM((tm, d), BF16)],
        compiler_params=_params("arbitrary", "arbitrary"),
        name="in_proj",
    )(x2d, mod, norm_g.reshape(1, d), w_p)


def _shifted(ref, mu, c, n_chunks, t_len, grid_shift):
    ch = ROW_CHUNK
    base = pl.multiple_of(c * ch, ch)
    cur = ref[0, pl.ds(base, ch), :]
    row = lax.broadcasted_iota(jnp.int32, (ch, 1), 0)
    before = pltpu.roll(cur, 1, 0)
    after = pltpu.roll(cur, ch - 1, 0)
    if not grid_shift:
        prev = jnp.where(row == 0, 0.0, before)
        nxt = jnp.where(row == ch - 1, 0.0, after)
        mixed = 0.5 * (prev + nxt)
    else:
        gw = 64
        left = jnp.where(row % gw == 0, 0.0, before)
        right = jnp.where(row % gw == gw - 1, 0.0, after)
        up_start = pl.multiple_of(jnp.maximum(base - gw, 0), gw)
        dn_start = pl.multiple_of(jnp.minimum(base + ch, t_len - gw), gw)
        up_halo = jnp.where(c > 0, ref[0, pl.ds(up_start, gw), :], 0.0)
        dn_halo = jnp.where(c < n_chunks - 1, ref[0, pl.ds(dn_start, gw), :], 0.0)
        up = jnp.concatenate([up_halo, cur[: ch - gw]], axis=0)
        down = jnp.concatenate([cur[gw:], dn_halo], axis=0)
        mixed = 0.25 * (up + down + left + right)
    return cur + mu * (mixed - cur)


def _prescan_kernel(r_ref, k_ref, v_ref, lora_ref, mur_ref, muk_ref, muv_ref, mul_ref,
                    uph_ref, upl_ref, bias_ref, kk_ref, ka_ref, rk_ref,
                    r_o, v_o, a_o, bonus_o, w_o, kd_o, bd_o, lh_scr, ll_scr,
                    *, t_len, grid_shift, n_sub):
    n_chunks = t_len // ROW_CHUNK
    seg = _seg_ones()
    for sub in range(n_sub):
        c = pl.program_id(2) * n_sub + sub
        rows = pl.ds(pl.multiple_of(c * ROW_CHUNK, ROW_CHUNK), ROW_CHUNK)
        cols = slice(sub * ROW_CHUNK, (sub + 1) * ROW_CHUNK)
        sh = functools.partial(_shifted, c=c, n_chunks=n_chunks, t_len=t_len, grid_shift=grid_shift)

        @pl.when(pl.program_id(1) == 0)
        def _():
            lane = lax.broadcasted_iota(jnp.int32, (1, LORA_BLK), 1)
            lora = sh(lora_ref, mul_ref[...])
            lora = jnp.where(lane < 2 * LORA_R, jnp.tanh(lora), lora)
            hi, lo = _split2(lora)
            lh_scr[rows, :] = hi
            ll_scr[rows, :] = lo

        r = sh(r_ref, mur_ref[...])
        k = sh(k_ref, muk_ref[...])
        v = sh(v_ref, muv_ref[...])
        l_hi = lh_scr[rows, :]
        l_lo = ll_scr[rows, :]
        raw = (jnp.dot(l_hi, uph_ref[0], preferred_element_type=F32)
               + jnp.dot(l_hi, upl_ref[0], preferred_element_type=F32)
               + jnp.dot(l_lo, uph_ref[0], preferred_element_type=F32)) + bias_ref[0]
        kk = k * kk_ref[...]
        kk = kk * lax.rsqrt(_seg_sum(kk * kk, seg) + 1e-12)
        kd_sum = jnp.zeros_like(k)
        for d in range(2):
            decay = jnp.exp(-DECAY_SCALE * jax.nn.sigmoid(raw[:, d * LANES:(d + 1) * LANES]))
            iclr = jax.nn.sigmoid(raw[:, (2 + d) * LANES:(3 + d) * LANES])
            kd = k * (1.0 + (iclr - 1.0) * ka_ref[...])
            w_o[d, 0, :, cols] = decay.T
            kd_o[d, 0, :, cols] = kd.T
            bd_o[d, 0, :, cols] = (kk * iclr).T
            kd_sum = kd_sum + kd
        r_o[0, :, cols] = r.T
        v_o[0, :, cols] = v.T
        a_o[0, :, cols] = (-kk).T
        bonus_o[0, cols, :] = _seg_sum(r * (0.5 * kd_sum) * rk_ref[...], seg) * v


def pre_scan(proj, mu, up_w, up_b, k_k, k_a, r_k, grid_shift):
    b, t, _ = proj.shape
    up_hi, up_lo = _split2(up_w)
    if not grid_shift:
        assert t == ROW_CHUNK, "sequence shift handles one chunk per sequence"
    assert t % ROW_CHUNK == 0
    col = lambda off: pl.BlockSpec((1, t, LANES), lambda i, j, c: (i, 0, off + j))
    vec = lambda off: pl.BlockSpec((1, LANES), lambda i, j, c: (0, off + j))
    n_sub = next(n for n in (4, 2, 1) if t % (n * ROW_CHUNK) == 0)
    rows_step = n_sub * ROW_CHUNK
    out_t = pl.BlockSpec((1, LANES, rows_step), lambda i, j, c: (i, j, c))
    out_t2 = pl.BlockSpec((2, 1, LANES, rows_step), lambda i, j, c: (0, i, j, c))
    s_t = jax.ShapeDtypeStruct((b, RWKV_W, t), F32)
    s_t2 = jax.ShapeDtypeStruct((2, b, RWKV_W, t), F32)
    return pl.pallas_call(
        functools.partial(_prescan_kernel, t_len=t, grid_shift=grid_shift, n_sub=n_sub),
        grid=(b, HEAD_TILES, t // rows_step),
        in_specs=[col(0), col(HEAD_TILES), col(2 * HEAD_TILES),
                  pl.BlockSpec((1, t, LORA_BLK), lambda i, j, c: (i, 0, LORA_OFF // LORA_BLK)),
                  vec(0), vec(HEAD_TILES), vec(2 * HEAD_TILES),
                  pl.BlockSpec((1, LORA_BLK), lambda i, j, c: (0, LORA_OFF // LORA_BLK)),
                  pl.BlockSpec((1, LORA_BLK, 4 * LANES), lambda i, j, c: (j, 0, 0)),
                  pl.BlockSpec((1, LORA_BLK, 4 * LANES), lambda i, j, c: (j, 0, 0)),
                  pl.BlockSpec((1, 1, 4 * LANES), lambda i, j, c: (j, 0, 0)),
                  vec(0), vec(0), vec(0)],
        out_specs=[out_t, out_t, out_t,
                   pl.BlockSpec((1, rows_step, LANES), lambda i, j, c: (i, c, j)),
                   out_t2, out_t2, out_t2],
        out_shape=[s_t, s_t, s_t, jax.ShapeDtypeStruct((b, t, RWKV_W), F32), s_t2, s_t2, s_t2],
        scratch_shapes=[pltpu.VMEM((t, LORA_BLK), BF16), pltpu.VMEM((t, LORA_BLK), BF16)],
        compiler_params=_params("arbitrary", "arbitrary", "arbitrary"),
        name="pre_scan",
    )(proj, proj, proj, proj, mu, mu, mu, mu, up_hi, up_lo, up_b,
      k_k.reshape(1, RWKV_W), k_a.reshape(1, RWKV_W), r_k.reshape(1, RWKV_W))


def _lora_up_tables(w_decay_up, w_decay0, w_iclr_up, w_iclr0):
    mats = [w_decay_up[0], w_decay_up[1], w_iclr_up[0], w_iclr_up[1]]
    bias = [w_decay0[0], w_decay0[1], w_iclr0[0], w_iclr0[1]]
    up = jnp.zeros((HEAD_TILES, LORA_BLK, 4 * LANES), F32)
    for q, m in enumerate(mats):
        blk = m.astype(F32).reshape(LORA_R, HEAD_TILES, LANES).transpose(1, 0, 2)
        up = up.at[:, q * LORA_R:(q + 1) * LORA_R, q * LANES:(q + 1) * LANES].set(blk)
    b = jnp.stack([v.astype(F32).reshape(HEAD_TILES, LANES) for v in bias], axis=1)
    return up, b.reshape(HEAD_TILES, 1, 4 * LANES)


def _scan_chunk(ins, y_ref, yb_ref, s_scr, wr_scr, *, tc, backward=None):
    mixed = yb_ref is not None
    tt = (lambda t: t) if mixed else (lambda t: jnp.where(backward, tc - 1 - t, t))

    def view(i):
        ref = ins[i]
        return (lambda t: ref[tt(t)]), (lambda t, k: ref[tt(t), k:k + 1, :])

    (w_full, w_row), (k_full, k_row), (b_full, b_row), (_, a_row), (r_full, _), (v_full, _) = (
        view(i) for i in range(6))

    wr_scr[0] = w_full(0) * r_full(0)
    sa = jnp.zeros((HEAD_N, LANES), F32)
    y0 = jnp.zeros((HEAD_N, LANES), F32)
    for k in range(HEAD_N):
        s = s_scr[k]
        sa = sa + s * a_row(0, k)
        y0 = y0 + s * wr_scr[0, k:k + 1, :]

    def step(t, carry):
        sa, y0 = carry
        r_t = r_full(t)
        v_t = v_full(t)
        br = jnp.sum(b_full(t) * r_t, axis=0, keepdims=True)
        kr = jnp.sum(k_full(t) * r_t, axis=0, keepdims=True)
        y = y0 + sa * br + v_t * kr
        y_ref[tt(t)] = y
        if mixed:
            yb_ref[tc - 1 - t] = y
        tn = jnp.minimum(t + 1, tc - 1)
        slot = (t + 1) % 2
        wr_scr[slot] = w_full(tn) * r_full(tn)
        sa_n = jnp.zeros((HEAD_N, LANES), F32)
        y0_n = jnp.zeros((HEAD_N, LANES), F32)
        for k in range(HEAD_N):
            s = s_scr[k] * w_row(t, k) + sa * b_row(t, k) + v_t * k_row(t, k)
            s_scr[k] = s
            sa_n = sa_n + s * a_row(tn, k)
            y0_n = y0_n + s * wr_scr[slot, k:k + 1, :]
        return sa_n, y0_n

    lax.fori_loop(0, tc, step, (sa, y0))


def _scan_kernel(*refs, tc, mixed):
    ins, rest = refs[:6], refs[6:]
    if mixed:
        s0_ref, y_ref, yb_ref, sfin_ref, s_scr, wr_scr = rest
        backward = None
    else:
        s0_ref, y_ref, sfin_ref, s_scr, wr_scr = rest
        yb_ref = None
        backward = pl.program_id(0) >= pl.num_programs(0) // 2
    c = pl.program_id(1)

    @pl.when(c == 0)
    def _():
        s_scr[...] = s0_ref[0]

    _scan_chunk(ins, y_ref, yb_ref, s_scr, wr_scr, tc=tc, backward=backward)

    @pl.when(c == pl.num_programs(1) - 1)
    def _():
        sfin_ref[0] = s_scr[...]


def wkv_scan(ops, s0):
    t, _, lanes = ops[0].shape
    groups = lanes // LANES
    mixed = groups == 1
    tc = 64
    nc = t // tc
    st = pl.BlockSpec((1, HEAD_N, HEAD_N, LANES), lambda g, c: (g, 0, 0, 0))
    y_shape = jax.ShapeDtypeStruct((t, HEAD_N, lanes), F32)
    if mixed:
        fwd = pl.BlockSpec((tc, HEAD_N, LANES), lambda g, c: (c, 0, g))
        bwd = pl.BlockSpec((tc, HEAD_N, LANES), lambda g, c: (nc - 1 - c, 0, g))
        in_specs, operands = [fwd] * 6 + [st], list(ops) + [s0]
        out_specs, out_shape = [fwd, bwd, st], [y_shape, y_shape]
    else:
        half_g = groups // 2
        tmap = lambda g, c: jnp.where(g >= half_g, nc - 1 - c, c)
        seq = pl.BlockSpec((tc, HEAD_N, LANES), lambda g, c: (tmap(g, c), 0, g))
        shared = pl.BlockSpec((tc, HEAD_N, LANES), lambda g, c: (tmap(g, c), 0, g % half_g))
        in_specs = [seq if o.shape[2] == lanes else shared for o in ops] + [st]
        operands = list(ops) + [s0]
        out_specs, out_shape = [seq, st], [y_shape]
    out_shape.append(jax.ShapeDtypeStruct((groups, HEAD_N, HEAD_N, LANES), F32))
    res = pl.pallas_call(
        functools.partial(_scan_kernel, tc=tc, mixed=mixed),
        grid=(groups, nc),
        in_specs=in_specs,
        out_specs=out_specs,
        out_shape=out_shape,
        scratch_shapes=[pltpu.VMEM((HEAD_N, HEAD_N, LANES), F32),
                        pltpu.VMEM((2, HEAD_N, LANES), F32)],
        compiler_params=_params("arbitrary", "arbitrary"),
        name="wkv_scan",
    )(*operands)
    half = lanes // 2
    if mixed:
        y_f, y_b, s_fin = res
        return y_f[..., :half] + y_b[..., half:], s_fin
    y, s_fin = res
    return y[..., :half] + y[..., half:], s_fin


K_BLOCK = 8


def _relayout_kernel(*refs, n_ops, rb):
    ins, outs, scr = refs[:n_ops], refs[n_ops:2 * n_ops], refs[2 * n_ops]
    for x_ref, o_ref in zip(ins, outs):
        x2 = x_ref.reshape(rb * K_BLOCK, LANES)
        for r0 in range(0, rb, LANES):
            for kk in range(K_BLOCK):
                m = x2[pl.ds(r0 * K_BLOCK + kk, LANES, stride=K_BLOCK), :]
                scr[pl.ds(kk, LANES, stride=K_BLOCK), :] = m.T
            o_ref[:, :, r0:r0 + LANES] = scr[...].reshape(LANES, K_BLOCK, LANES)


def relayout_to_scan(xs):
    rows, _, t = xs[0].shape
    assert rows % LANES == 0
    rb = min(rows, 1024)
    n = len(xs)
    return pl.pallas_call(
        functools.partial(_relayout_kernel, n_ops=n, rb=rb),
        grid=(rows // rb, HEAD_N // K_BLOCK, t // LANES),
        in_specs=[pl.BlockSpec((rb, K_BLOCK, LANES), lambda r, k, c: (r, k, c))] * n,
        out_specs=[pl.BlockSpec((LANES, K_BLOCK, rb), lambda r, k, c: (c, k, r))] * n,
        out_shape=[jax.ShapeDtypeStruct((t, HEAD_N, rows), F32)] * n,
        scratch_shapes=[pltpu.VMEM((LANES * K_BLOCK, LANES), F32)],
        compiler_params=_params("arbitrary", "arbitrary", "arbitrary"),
        name="relayout_to_scan",
    )(*xs)


def _relayout_merged_kernel(*refs, n_ops):
    n = n_ops
    fw, bw, outs, scr = refs[:n], refs[n:2 * n], refs[2 * n:3 * n], refs[3 * n]
    half = LANES // 2
    r = lax.broadcasted_iota(jnp.int32, (LANES, LANES), 0)
    c = lax.broadcasted_iota(jnp.int32, (LANES, LANES), 1)
    flip = (r + c == LANES - 1).astype(BF16)
    for f_ref, b_ref, o_ref in zip(fw, bw, outs):
        f2 = f_ref.reshape(half * K_BLOCK, LANES)
        b2 = b_ref.reshape(half * K_BLOCK, LANES)
        for kk in range(K_BLOCK):
            m_f = f2[pl.ds(kk, half, stride=K_BLOCK), :]
            m_b = b2[pl.ds(kk, half, stride=K_BLOCK), :]
            m_b = sum(jnp.dot(part, flip, preferred_element_type=F32) for part in _split3(m_b))
            scr[pl.ds(kk, LANES, stride=K_BLOCK), :] = jnp.concatenate([m_f, m_b], axis=0).T
        o_ref[...] = scr[...].reshape(LANES, K_BLOCK, LANES)


def relayout_merged(xs):
    t = xs[0].shape[2]
    nt = t // LANES
    half = LANES // 2
    assert all(x.shape[0] in (LANES, half) for x in xs)
    fwd = [pl.BlockSpec((half, K_BLOCK, LANES), lambda k, c: (0, k, c))] * len(xs)
    bwd = [pl.BlockSpec((half, K_BLOCK, LANES),
                        functools.partial(lambda blk, k, c: (blk, k, nt - 1 - c), x.shape[0] // half - 1))
           for x in xs]
    n = len(xs)
    return pl.pallas_call(
        functools.partial(_relayout_merged_kernel, n_ops=n),
        grid=(HEAD_N // K_BLOCK, nt),
        in_specs=fwd + bwd,
        out_specs=[pl.BlockSpec((LANES, K_BLOCK, LANES), lambda k, c: (c, k, 0))] * n,
        out_shape=[jax.ShapeDtypeStruct((t, HEAD_N, LANES), F32)] * n,
        scratch_shapes=[pltpu.VMEM((LANES * K_BLOCK, LANES), F32)],
        compiler_params=_params("arbitrary", "arbitrary"),
        name="relayout_merged",
    )(*xs, *xs)


def _scan_rows(x):
    return x.reshape(-1, HEAD_N, x.shape[-1])


def _from_scan_layout(y, b):
    t = y.shape[0]
    return y.reshape(t, HEAD_N, b, N_HEADS).transpose(2, 0, 3, 1).reshape(b, t, RWKV_W)


def _postscan_kernel(y_ref, bonus_ref, g_ref, gw_ref, gb_ref, o_ref):
    seg = _seg_ones()
    for j in range(HEAD_TILES):
        cols = slice(j * LANES, (j + 1) * LANES)
        y = y_ref[0, :, cols]
        mu = _seg_sum(y, seg) * (1.0 / HEAD_N)
        d = y - mu
        var = _seg_sum(d * d, seg) * (1.0 / HEAD_N)
        yn = d * lax.rsqrt(var + GN_EPS) * gw_ref[:, cols] + gb_ref[:, cols]
        o_ref[0, :, cols] = ((yn + bonus_ref[0, :, cols]) * jax.nn.sigmoid(g_ref[0, :, cols])).astype(BF16)


def post_scan(y, bonus, proj, gn_w, gn_b):
    b, t, _ = y.shape
    ch = ROW_CHUNK
    blk = pl.BlockSpec((1, ch, RWKV_W), lambda i, c: (i, c, 0))
    vec = pl.BlockSpec((1, RWKV_W), lambda i, c: (0, 0))
    return pl.pallas_call(
        _postscan_kernel,
        grid=(b, t // ch),
        in_specs=[blk, blk,
                  pl.BlockSpec((1, ch, RWKV_W), lambda i, c: (i, c, G_OFF // RWKV_W)),
                  vec, vec],
        out_specs=blk,
        out_shape=jax.ShapeDtypeStruct((b, t, RWKV_W), BF16),
        compiler_params=_params("arbitrary", "arbitrary"),
        name="post_scan",
    )(y, bonus, proj, gn_w.reshape(1, RWKV_W), gn_b.reshape(1, RWKV_W))


def _fnet_kernel(f_ref, cc_ref, ct_ref, o_ref, xcs_ref, *, t_len, scale):
    @pl.when(pl.program_id(2) == 0)
    def _():
        xc = jnp.dot(f_ref[0].astype(BF16), cc_ref[...], preferred_element_type=F32)
        xcs_ref[0:t_len, :] = xc[:, :FOURIER_GW].astype(BF16)
        xcs_ref[t_len:2 * t_len, :] = xc[:, FOURIER_GW:].astype(BF16)

    out = jnp.dot(ct_ref[...], xcs_ref[...], preferred_element_type=F32)
    o_ref[0] = (out * scale).astype(BF16)


def _fnet_short_kernel(f_ref, cc_ref, ct_ref, o_ref, *, t_len, scale):
    gw = FOURIER_GW
    for g in range(N_FOURIER_GROUPS):
        cols = slice(g * gw, (g + 1) * gw)
        xc = jnp.dot(f_ref[0, :, cols].astype(BF16), cc_ref[...], preferred_element_type=F32)
        out = (jnp.dot(ct_ref[:, :t_len], xc[:, :gw].astype(BF16), preferred_element_type=F32)
               + jnp.dot(ct_ref[:, t_len:], xc[:, gw:].astype(BF16), preferred_element_type=F32))
        o_ref[0, :, cols] = (out * scale).astype(BF16)


def _dft_tables(n):
    j = lax.broadcasted_iota(jnp.int32, (n, n), 0)
    k = lax.broadcasted_iota(jnp.int32, (n, n), 1)
    ang = ((j * k) % n).astype(F32) * (2.0 * math.pi / n)
    return jnp.cos(ang), jnp.sin(ang)


def fnet_mix(proj):
    b, t, _ = proj.shape
    cc, sc = _dft_tables(FOURIER_GW)
    ct, st = _dft_tables(t)
    cc2 = jnp.concatenate([cc, sc], axis=1).astype(BF16)
    ct2 = jnp.concatenate([ct, -st], axis=1).astype(BF16)
    scale = 1.0 / math.sqrt(t * FOURIER_GW)
    if t <= 512:
        blk = pl.BlockSpec((1, t, RWKV_W), lambda i: (i, 0, 0))
        return pl.pallas_call(
            functools.partial(_fnet_short_kernel, t_len=t, scale=scale),
            grid=(b,),
            in_specs=[pl.BlockSpec((1, t, RWKV_W), lambda i: (i, 0, F_OFF // RWKV_W)),
                      pl.BlockSpec((FOURIER_GW, 2 * FOURIER_GW), lambda i: (0, 0)),
                      pl.BlockSpec((t, 2 * t), lambda i: (0, 0))],
            out_specs=blk,
            out_shape=jax.ShapeDtypeStruct((b, t, RWKV_W), BF16),
            compiler_params=_params("arbitrary"),
            name="fnet_mix",
        )(proj, cc2, ct2)
    tq = min(512, t)
    return pl.pallas_call(
        functools.partial(_fnet_kernel, t_len=t, scale=scale),
        grid=(b, N_FOURIER_GROUPS, t // tq),
        in_specs=[pl.BlockSpec((1, t, FOURIER_GW), lambda i, g, q: (i, 0, F_OFF // FOURIER_GW + g)),
                  pl.BlockSpec((FOURIER_GW, 2 * FOURIER_GW), lambda i, g, q: (0, 0)),
                  pl.BlockSpec((tq, 2 * t), lambda i, g, q: (q, 0))],
        out_specs=pl.BlockSpec((1, tq, FOURIER_GW), lambda i, g, q: (i, q, g)),
        out_shape=jax.ShapeDtypeStruct((b, t, RWKV_W), BF16),
        scratch_shapes=[pltpu.VMEM((2 * t, FOURIER_GW), BF16)],
        compiler_params=_params("arbitrary", "arbitrary", "arbitrary"),
        name="fnet_mix",
    )(proj, cc2, ct2)


def _outproj_kernel(yr_ref, yf_ref, w_ref, x_ref, mod_ref, o_ref):
    m = jnp.dot(yr_ref[...], w_ref[0:RWKV_W, :], preferred_element_type=F32)
    m = m + jnp.dot(yf_ref[...], w_ref[RWKV_W:2 * RWKV_W, :], preferred_element_type=F32)
    o_ref[...] = x_ref[...] + mod_ref[0, 2:3, :] * m


def out_proj(yr, yf, w_bf, x2d, mod, rows_per_mod, mod_base):
    m, d = x2d.shape
    tm, tn = min(1024, rows_per_mod), 512
    row_of = _mod_row_map(tm, rows_per_mod, mod_base)
    return pl.pallas_call(
        _outproj_kernel,
        grid=(m // tm, d // tn),
        in_specs=[pl.BlockSpec((tm, RWKV_W), lambda i, j: (i, 0)),
                  pl.BlockSpec((tm, RWKV_W), lambda i, j: (i, 0)),
                  pl.BlockSpec((2 * RWKV_W, tn), lambda i, j: (0, j)),
                  pl.BlockSpec((tm, tn), lambda i, j: (i, j)),
                  pl.BlockSpec((1, N_MOD, tn), lambda i, j: (row_of(i), 0, j))],
        out_specs=pl.BlockSpec((tm, tn), lambda i, j: (i, j)),
        out_shape=jax.ShapeDtypeStruct((m, d), F32),
        compiler_params=_params("arbitrary", "arbitrary"),
        name="out_proj",
    )(yr, yf, w_bf, x2d, mod)


def _norm2_router_kernel(x_ref, mod_ref, g_ref, rwh_ref, rwl_ref, h_ref, aff_ref):
    h = _rms_modulate(x_ref[...], g_ref[...], mod_ref[0, 4:5, :], mod_ref[0, 3:4, :])
    h_ref[...] = h
    h_hi, h_lo = _split2(h)
    logits = (jnp.dot(h_hi, rwh_ref[...], preferred_element_type=F32)
              + jnp.dot(h_hi, rwl_ref[...], preferred_element_type=F32)
              + jnp.dot(h_lo, rwh_ref[...], preferred_element_type=F32))
    z = logits - jnp.max(logits, axis=-1, keepdims=True)
    e = jnp.exp(z)
    aff_ref[...] = e / jnp.sum(e, axis=-1, keepdims=True)


def norm2_router(x2d, mod, norm_g, router_w, rows_per_mod, mod_base):
    m, d = x2d.shape
    tm = ROW_CHUNK
    row_of = _mod_row_map(tm, rows_per_mod, mod_base)
    rw_hi, rw_lo = _split2(router_w)
    return pl.pallas_call(
        _norm2_router_kernel,
        grid=(m // tm,),
        in_specs=[pl.BlockSpec((tm, d), lambda i: (i, 0)),
                  pl.BlockSpec((1, N_MOD, d), lambda i: (row_of(i), 0, 0)),
                  pl.BlockSpec((1, d), lambda i: (0, 0)),
                  pl.BlockSpec((d, N_EXPERTS), lambda i: (0, 0)),
                  pl.BlockSpec((d, N_EXPERTS), lambda i: (0, 0))],
        out_specs=[pl.BlockSpec((tm, d), lambda i: (i, 0)),
                   pl.BlockSpec((tm, N_EXPERTS), lambda i: (i, 0))],
        out_shape=[jax.ShapeDtypeStruct((m, d), F32),
                   jax.ShapeDtypeStruct((m, N_EXPERTS), F32)],
        compiler_params=_params("arbitrary"),
        name="norm2_router",
    )(x2d, mod, norm_g.reshape(1, d), rw_hi, rw_lo)


GATHER_ROWS = 256
PAIR_CHUNK = LANES
COMBINE_ROWS = 256


def _issue_row_gather(idx_ref, base, n_rows, src_hbm, dst, sem):
    def body(i, carry):
        for prio in range(2):
            r = 2 * i + prio
            row = idx_ref[base + r]
            pltpu.make_async_copy(src_hbm.at[pl.ds(row, 1), :], dst.at[pl.ds(r, 1), :],
                                  sem).start(priority=prio)
        return carry
    lax.fori_loop(0, n_rows // 2, body, 0, unroll=4)


def _wait_row_gather(n_rows, src_hbm, dst, sem):
    pltpu.make_async_copy(src_hbm.at[pl.ds(0, n_rows), :], dst, sem).wait()


def _gather_kernel(idx_ref, h_hbm, o_ref, buf, sem):
    s = pl.program_id(0)
    slot = s % 2

    @pl.when(s == 0)
    def _():
        _issue_row_gather(idx_ref, 0, GATHER_ROWS, h_hbm, buf.at[0], sem.at[0])

    @pl.when(s + 1 < pl.num_programs(0))
    def _():
        _issue_row_gather(idx_ref, (s + 1) * GATHER_ROWS, GATHER_ROWS, h_hbm,
                          buf.at[1 - slot], sem.at[1 - slot])

    _wait_row_gather(GATHER_ROWS, h_hbm, buf.at[slot], sem.at[slot])
    o_ref[...] = buf[slot].astype(BF16)


def gather_tokens(h, idx_flat):
    n, d = h.shape
    rows = idx_flat.shape[0]
    return pl.pallas_call(
        _gather_kernel,
        grid_spec=pltpu.PrefetchScalarGridSpec(
            num_scalar_prefetch=1,
            grid=(rows // GATHER_ROWS,),
            in_specs=[pl.BlockSpec(memory_space=pl.ANY)],
            out_specs=pl.BlockSpec((GATHER_ROWS, d), lambda s, idx: (s, 0)),
            scratch_shapes=[pltpu.VMEM((2, GATHER_ROWS, d), F32), pltpu.SemaphoreType.DMA((2,))]),
        out_shape=jax.ShapeDtypeStruct((rows, d), BF16),
        compiler_params=_params("arbitrary"),
        name="gather_tokens",
    )(idx_flat, h)


def _ffn_accumulate(xe_ref, w1_ref, w3_ref, w2_ref, o_ref, width):
    x = xe_ref[0]
    h1 = jnp.dot(x, w1_ref[0, :, :width].astype(BF16), preferred_element_type=F32)
    h3 = jnp.dot(x, w3_ref[0, :, :width].astype(BF16), preferred_element_type=F32)
    hid = h1 * jax.nn.sigmoid(h1) * h3
    o_ref[0] += jnp.dot(hid.astype(BF16), w2_ref[0, :width, :].astype(BF16),
                        preferred_element_type=F32)


def _expert_kernel(xe_ref, w1_ref, w3_ref, w2_ref, gate_ref, o_ref, *, tf, tail):
    f = pl.program_id(1)
    last = pl.num_programs(1) - 1
    step = functools.partial(_ffn_accumulate, xe_ref, w1_ref, w3_ref, w2_ref, o_ref)

    @pl.when(f == 0)
    def _():
        o_ref[...] = jnp.zeros_like(o_ref)

    if tail == tf:
        step(tf)
    else:
        @pl.when(f < last)
        def _():
            step(tf)

        @pl.when(f == last)
        def _():
            step(tail)

    @pl.when(f == last)
    def _():
        o_ref[0] = o_ref[0] * gate_ref[0]


def expert_ffn(xe, w1, w3, w2, gate):
    e, cap, d = xe.shape
    ff = w1.shape[2]
    tf = 512
    nf = pl.cdiv(ff, tf)
    return pl.pallas_call(
        functools.partial(_expert_kernel, tf=tf, tail=ff - (nf - 1) * tf),
        grid=(e, nf),
        in_specs=[pl.BlockSpec((1, cap, d), lambda i, f: (i, 0, 0)),
                  pl.BlockSpec((1, d, tf), lambda i, f: (i, 0, f)),
                  pl.BlockSpec((1, d, tf), lambda i, f: (i, 0, f)),
                  pl.BlockSpec((1, tf, d), lambda i, f: (i, f, 0)),
                  pl.BlockSpec((1, cap, 1), lambda i, f: (i, 0, 0))],
        out_specs=pl.BlockSpec((1, cap, d), lambda i, f: (i, 0, 0)),
        out_shape=jax.ShapeDtypeStruct((e, cap, d), F32),
        compiler_params=_params("arbitrary", "arbitrary"),
        name="expert_ffn",
    )(xe, w1, w3, w2, gate.reshape(e, cap, 1))


def _combine_final_kernel(rows_ref, ptr_ref, tok_ref, ye_hbm, x_ref, mod_ref, g_ref, o_ref,
                          buf, sem, acc_ref, *, tm):
    i = pl.program_id(0)
    p0 = ptr_ref[i]
    p1 = ptr_ref[i + 1]
    c0 = p0 // PAIR_CHUNK
    c1 = jnp.where(p1 > p0, (p1 - 1) // PAIR_CHUNK + 1, c0)
    acc_ref[...] = jnp.zeros_like(acc_ref)

    @pl.when(c1 > c0)
    def _():
        _issue_row_gather(rows_ref, c0 * PAIR_CHUNK, PAIR_CHUNK, ye_hbm, buf.at[0], sem.at[0])

    def chunk(c, carry):
        slot = (c - c0) % 2

        @pl.when(c + 1 < c1)
        def _():
            _issue_row_gather(rows_ref, (c + 1) * PAIR_CHUNK, PAIR_CHUNK, ye_hbm,
                              buf.at[1 - slot], sem.at[1 - slot])

        _wait_row_gather(PAIR_CHUNK, ye_hbm, buf.at[slot], sem.at[slot])
        local = tok_ref[pl.ds(c, 1), :] - i * tm
        onehot = (lax.broadcasted_iota(jnp.int32, (tm, PAIR_CHUNK), 0) == local).astype(BF16)
        acc_ref[...] += jnp.dot(onehot, buf[slot].astype(BF16), preferred_element_type=F32)
        return carry

    lax.fori_loop(c0, c1, chunk, 0)
    x = x_ref[...] + mod_ref[0, 5:6, :] * acc_ref[...]
    ms = jnp.mean(x * x, axis=-1, keepdims=True)
    o_ref[...] = x * lax.rsqrt(ms + NORM_EPS) * g_ref[...]


def combine_final(x2d, ye_flat, pair_rows, pair_toks, tile_ptr, mod, final_g, rows_per_mod, mod_base):
    m, d = x2d.shape
    tm = COMBINE_ROWS
    n_pairs = pair_rows.shape[0]
    row_of = _mod_row_map(tm, rows_per_mod, mod_base)
    return pl.pallas_call(
        functools.partial(_combine_final_kernel, tm=tm),
        grid_spec=pltpu.PrefetchScalarGridSpec(
            num_scalar_prefetch=2,
            grid=(m // tm,),
            in_specs=[pl.BlockSpec((n_pairs // PAIR_CHUNK, PAIR_CHUNK), lambda i, r, p: (0, 0)),
                      pl.BlockSpec(memory_space=pl.ANY),
                      pl.BlockSpec((tm, d), lambda i, r, p: (i, 0)),
                      pl.BlockSpec((1, N_MOD, d), lambda i, r, p: (row_of(i), 0, 0)),
                      pl.BlockSpec((1, d), lambda i, r, p: (0, 0))],
            out_specs=pl.BlockSpec((tm, d), lambda i, r, p: (i, 0)),
            scratch_shapes=[pltpu.VMEM((2, PAIR_CHUNK, d), F32), pltpu.SemaphoreType.DMA((2,)),
                            pltpu.VMEM((tm, d), F32)]),
        out_shape=jax.ShapeDtypeStruct((m, d), F32),
        compiler_params=_params("arbitrary"),
        name="combine_final",
    )(pair_rows, tile_ptr, pair_toks.reshape(n_pairs // PAIR_CHUNK, PAIR_CHUNK), ye_flat,
      x2d, mod, final_g.reshape(1, d))


def _prepare_in_proj(w_in, shift_mu):
    d = w_in.shape[0]
    rkv, lora, gf = w_in[:, :G_OFF], w_in[:, G_OFF:SHIFT_W], w_in[:, SHIFT_W:]
    w_p = jnp.concatenate([rkv, gf, lora, jnp.zeros((d, LANES), w_in.dtype)], axis=1).astype(BF16)
    mu = jnp.concatenate([shift_mu[:G_OFF], jnp.zeros((2 * RWKV_W,), F32), shift_mu[G_OFF:],
                          jnp.zeros((LANES,), F32)])
    return w_p, mu.reshape(1, IN_PAD_W)


def _mix_inputs(x, mod, rows_per_mod, mod_base, grid_shift, p):
    b, t, d = x.shape
    x2d = x.reshape(b * t, d)
    proj = in_proj(x2d, mod, p["norm1_g"], p["w_in_p"], rows_per_mod, mod_base)
    proj = proj.reshape(b, t, IN_PAD_W)
    r, v, a, bonus, w, kd, bd = pre_scan(proj, p["mu"], p["up_w"], p["up_b"],
                                         p["k_k"], p["k_a"], p["r_k"], grid_shift)
    per_dir = [_scan_rows(w), _scan_rows(kd), _scan_rows(bd)]
    shared = [_scan_rows(a), _scan_rows(r), _scan_rows(v)]
    if 2 * b * N_HEADS == LANES:
        ops = relayout_merged(per_dir + shared)
    else:
        ops = list(relayout_to_scan(per_dir)) + list(relayout_to_scan(shared))
    return {"x2d": x2d, "proj": proj, "bonus": bonus, "ops": list(ops),
            "mod": (mod, rows_per_mod, mod_base), "shape": (b, t, d)}


def _route(st, y, p):
    b, t, d = st["shape"]
    mod_args = st["mod"]
    yr = post_scan(_from_scan_layout(y, b), st["bonus"], st["proj"], p["gn_w"], p["gn_b"])
    yf = fnet_mix(st["proj"])
    x1 = out_proj(yr.reshape(b * t, RWKV_W), yf.reshape(b * t, RWKV_W), p["w_out_bf"],
                  st["x2d"], *mod_args)
    h2, aff = norm2_router(x1, mod_args[0], p["norm2_g"], p["router_w"], *mod_args[1:])
    n = b * t
    cap = n * CAPACITY_FACTOR // N_EXPERTS
    gate, idx = lax.top_k(aff.T, cap)
    idx_flat = idx.reshape(-1).astype(jnp.int32)
    xe = gather_tokens(h2, idx_flat).reshape(N_EXPERTS, cap, d)
    return {"x1": x1, "xe": xe, "gate": gate, "idx_flat": idx_flat, "mod": mod_args,
            "shape": (b, t, d)}


def _combine(rt, ye, p):
    b, t, d = rt["shape"]
    n = b * t
    idx_flat = rt["idx_flat"]
    pair_toks, pair_rows = lax.sort_key_val(idx_flat, jnp.arange(idx_flat.shape[0], dtype=jnp.int32))
    tile_ptr = jnp.searchsorted(pair_toks, jnp.arange(0, n + 1, COMBINE_ROWS, dtype=jnp.int32),
                                side="left").astype(jnp.int32)
    mod, rows_per_mod, mod_base = rt["mod"]
    out = combine_final(rt["x1"], ye.reshape(-1, d), pair_rows, pair_toks, tile_ptr, mod,
                        p["final_norm_g"], rows_per_mod, mod_base)
    return out.reshape(b, t, d)


def _experts(p):
    return p["exp_w1"], p["exp_w3"], p["exp_w2"]


def _layer(x, mod, rows_per_mod, mod_base, s0, grid_shift, p):
    st = _mix_inputs(x, mod, rows_per_mod, mod_base, grid_shift, p)
    y, s_fin = wkv_scan(st["ops"], s0)
    rt = _route(st, y, p)
    ye = expert_ffn(rt["xe"], *_experts(p), rt["gate"])
    return _combine(rt, ye, p), s_fin


def kernel(x_prompt, x_sample, state_wkv, c, c_ctx, ada_w, ada_b, norm1_g, norm2_g, w_in, shift_mu,
           w_decay_up, w_decay0, w_iclr_up, w_iclr0, k_k, k_a, r_k, gn_w, gn_b, w_out, router_w,
           exp_w1, exp_w3, exp_w2, final_norm_g):
    depth = ada_w.shape[0]
    assert depth == 1, "single-layer trunk"
    l = 0
    bc, tc_len, d = x_prompt.shape
    bl, tl, _ = x_sample.shape
    cvecs = jnp.concatenate([c_ctx[None], c], axis=0)
    mod = ada_mod(cvecs, ada_w[l], ada_b[l]).reshape(1 + bl, N_MOD, d)
    up_w, up_b = _lora_up_tables(w_decay_up[l], w_decay0[l], w_iclr_up[l], w_iclr0[l])
    w_in_p, mu = _prepare_in_proj(w_in[l], shift_mu[l])
    p = {
        "norm1_g": norm1_g[l], "norm2_g": norm2_g[l], "final_norm_g": final_norm_g,
        "w_in_p": w_in_p, "mu": mu, "up_w": up_w, "up_b": up_b,
        "k_k": k_k[l], "k_a": k_a[l], "r_k": r_k[l], "gn_w": gn_w[l], "gn_b": gn_b[l],
        "w_out_bf": w_out[l].astype(BF16), "router_w": router_w[l],
        "exp_w1": exp_w1[l], "exp_w3": exp_w3[l], "exp_w2": exp_w2[l],
    }
    groups_c = 2 * bc * N_HEADS // LANES
    s0_ctx = jnp.zeros((groups_c, HEAD_N, HEAD_N, LANES), F32)
    ctx = _mix_inputs(x_prompt, mod, bc * tc_len, 0, False, p)
    y_ctx, s_ctx = wkv_scan(ctx["ops"], s0_ctx)
    ctx_rt = _route(ctx, y_ctx, p)
    s_ctx = s_ctx.transpose(1, 2, 0, 3).reshape(HEAD_N, HEAD_N, 2, bc, N_HEADS)
    new_state = s_ctx.transpose(3, 2, 4, 1, 0)[:, None]
    groups_l = 2 * bl * N_HEADS // LANES
    s0_lat = state_wkv[:, l].astype(F32).transpose(4, 3, 1, 0, 2)
    s0_lat = s0_lat.reshape(HEAD_N, HEAD_N, groups_l, LANES).transpose(2, 0, 1, 3)
    lat = _mix_inputs(x_sample, mod, tl, 1, True, p)
    ye_ctx = expert_ffn(ctx_rt["xe"], *_experts(p), ctx_rt["gate"])
    y_lat, _ = wkv_scan(lat["ops"], s0_lat)
    lat_rt = _route(lat, y_lat, p)
    ye_lat = expert_ffn(lat_rt["xe"], *_experts(p), lat_rt["gate"])
    return (_combine(ctx_rt, ye_ctx, p), _combine(lat_rt, ye_lat, p), new_state)
```
